```python
import jax, jax.numpy as jnp
from jax import lax
import numpy as np

D_MODEL = 1024
BATCH = 8
SEQ = 4096
DEPTH = 1

ATT_HEADS = 8
ATT_KV_HEADS = 2
ATT_HEAD_DIM = 64
IDX_HEADS = 8
IDX_HEAD_DIM = 64
TOPK_MAX = 256
Q_BLOCK = 128
DN_HEADS = 4
DN_HEAD_DIM = 128
DN_CONV = 4
DN_CHUNK = 64
PEER_HEADS = 8
PEER_N_KEYS = 128
PEER_N_EXPERTS = PEER_N_KEYS * PEER_N_KEYS
PEER_KEY_DIM = 256
PEER_TOPK = 16
PEER_TOKEN_BLOCK = 128
EPS = 1e-6

ATT_Q_W = ATT_HEADS * ATT_HEAD_DIM
ATT_KV_W = ATT_KV_HEADS * ATT_HEAD_DIM
IDX_Q_W = IDX_HEADS * IDX_HEAD_DIM
DN_W = DN_HEADS * DN_HEAD_DIM
IN_SPLITS = (ATT_Q_W, ATT_KV_W, ATT_KV_W,
             IDX_Q_W, IDX_HEAD_DIM, IDX_HEADS,
             DN_W, DN_W, DN_W, DN_W,
             DN_HEADS, DN_HEADS,
             D_MODEL, D_MODEL)
IN_WIDTH = sum(IN_SPLITS)

kernel_name = 'hybrid_dsa_gdn_peer_block'


def rms_norm(x, gain):
    xf = x.astype(jnp.float32)
    y = xf * lax.rsqrt(jnp.mean(xf * xf, axis=-1, keepdims=True) + EPS)
    return (y * gain.astype(jnp.float32)).astype(x.dtype)


def l2_norm(x):
    return x * lax.rsqrt(jnp.sum(x * x, axis=-1, keepdims=True) + EPS)


def causal_depthwise_conv(x, w):
    k_width, chans = w.shape
    return lax.conv_general_dilated(x, w[:, None, :], window_strides=(1,), padding=[(k_width - 1, 0)],
                                    dimension_numbers=('NWC', 'WIO', 'NWC'), feature_group_count=chans)


def dsa_attention(q, k, v, iq, ik, iw, k_sel):
    f32 = jnp.float32
    b, t = q.shape[:2]
    nb = t // Q_BLOCK
    grp = ATT_HEADS // ATT_KV_HEADS
    scale = ATT_HEAD_DIM ** -0.5
    key_pos = jnp.arange(t)
    ik32 = ik.astype(f32)

    def to_blocks(a):
        return jnp.moveaxis(a.reshape((b, nb, Q_BLOCK) + a.shape[2:]), 1, 0)

    def block(args):
        qb, iqb, iwb, start = args
        qpos = start + jnp.arange(Q_BLOCK)
        causal = key_pos[None, :] <= qpos[:, None]
        dots = jnp.einsum('bqhd,bsd->bqhs', iqb.astype(f32), ik32)
        score = jnp.einsum('bqh,bqhs->bqs', iwb.astype(f32), jax.nn.relu(dots))
        score = jnp.where(causal[None], score, -jnp.inf)
        _, sel = lax.top_k(score, k_sel)
        valid = sel <= qpos[None, :, None]
        k_g = jax.vmap(lambda kb, ib: kb[ib])(k, sel)
        v_g = jax.vmap(lambda vb, ib: vb[ib])(v, sel)
        qg = qb.reshape(b, Q_BLOCK, ATT_KV_HEADS, grp, ATT_HEAD_DIM).astype(f32)
        logits = jnp.einsum('bqhgd,bqkhd->bqhgk', qg, k_g.astype(f32)) * scale
        logits = jnp.where(valid[:, :, None, None, :], logits, -1e30)
        p = jax.nn.softmax(logits, axis=-1)
        o = jnp.einsum('bqhgk,bqkhd->bqhgd', p, v_g.astype(f32))
        return o.reshape(b, Q_BLOCK, ATT_Q_W).astype(qb.dtype)

    starts = jnp.arange(nb) * Q_BLOCK
    out = lax.map(block, (to_blocks(q), to_blocks(iq), to_blocks(iw), starts))
    return jnp.moveaxis(out, 0, 1).reshape(b, t, ATT_Q_W)


def gated_deltanet(q, k, v, g, beta):
    f32 = jnp.float32
    out_dtype = q.dtype
    b, t, h, d = q.shape
    c = DN_CHUNK
    n = t // c
    q = l2_norm(q.astype(f32)) * (d ** -0.5)
    k = l2_norm(k.astype(f32))
    v = v.astype(f32)

    def chunk(a):
        a = a.reshape((b, n, c, h) + a.shape[3:])
        return jnp.moveaxis(a, 3, 1)

    q, k, v = chunk(q), chunk(k), chunk(v)
    g = jnp.cumsum(chunk(g.astype(f32)), axis=-1)
    beta = chunk(beta.astype(f32))
    tri = jnp.tril(jnp.ones((c, c), bool))
    strict = jnp.tril(jnp.ones((c, c), bool), -1)
    decay = jnp.exp(jnp.where(tri, g[..., :, None] - g[..., None, :], -jnp.inf))
    k_beta = k * beta[..., None]
    a_mat = jnp.where(strict, jnp.einsum('bhnid,bhnjd->bhnij', k_beta, k) * decay, 0.0)
    rhs = jnp.concatenate([v * beta[..., None], k_beta * jnp.exp(g)[..., None]], axis=-1)
    sol = lax.linalg.triangular_solve(a_mat, rhs, left_side=True, lower=True, unit_diagonal=True)
    u, w = sol[..., :d], sol[..., d:]
    intra = jnp.where(tri, jnp.einsum('bhnid,bhnjd->bhnij', q, k) * decay, 0.0)

    def step(state, inp):
        q_i, k_i, u_i, w_i, g_i, att_i = inp
        v_new = u_i - jnp.einsum('bhcd,bhde->bhce', w_i, state)
        o = (jnp.einsum('bhcd,bhde->bhce', q_i * jnp.exp(g_i)[..., None], state)
             + jnp.einsum('bhij,bhje->bhie', att_i, v_new))
        g_last = g_i[..., -1]
        state = (state * jnp.exp(g_last)[..., None, None]
                 + jnp.einsum('bhcd,bhce->bhde', k_i * jnp.exp(g_last[..., None] - g_i)[..., None], v_new))
        return state, o

    xs = tuple(jnp.moveaxis(a, 2, 0) for a in (q, k, u, w, g, intra))
    state0 = jnp.zeros((b, h, d, d), f32)
    _, o = lax.scan(step, state0, xs)
    o = jnp.moveaxis(o, 0, 2).reshape(b, h, t, d).transpose(0, 2, 1, 3)
    return o.astype(out_dtype)


def peer(hn, w_query, sub_keys, u_tab, v_tab):
    f32 = jnp.float32
    b, t, dm = hn.shape
    ntok = b * t
    hf = hn.reshape(ntok, dm)
    q = (hf @ w_query).reshape(ntok, PEER_HEADS, 2, PEER_KEY_DIM // 2)
    s = jnp.einsum('nhpd,hpkd->nhpk', q.astype(f32), sub_keys.astype(f32))
    top_s, top_i = lax.top_k(s, PEER_TOPK)
    cand_s = (top_s[:, :, 0, :, None] + top_s[:, :, 1, None, :]).reshape(ntok, PEER_HEADS, -1)
    cand_i = (top_i[:, :, 0, :, None] * PEER_N_KEYS + top_i[:, :, 1, None, :]).reshape(ntok, PEER_HEADS, -1)
    best_s, pos = lax.top_k(cand_s, PEER_TOPK)
    expert = jnp.take_along_axis(cand_i, pos, axis=-1)
    gate = jax.nn.softmax(best_s, axis=-1)
    nblk = ntok // PEER_TOKEN_BLOCK

    def block(args):
        hb, eb, gb = args
        act = jax.nn.gelu(jnp.einsum('nhkd,nd->nhk', u_tab[eb], hb).astype(f32), approximate=False)
        coef = (gb * act).astype(hb.dtype)
        return jnp.einsum('nhk,nhkd->nd', coef, v_tab[eb])

    out = lax.map(block, (hf.reshape(nblk, PEER_TOKEN_BLOCK, dm),
                          expert.reshape(nblk, PEER_TOKEN_BLOCK, PEER_HEADS, PEER_TOPK),
                          gate.reshape(nblk, PEER_TOKEN_BLOCK, PEER_HEADS, PEER_TOPK)))
    return out.reshape(b, t, dm)


def setup_inputs(seed: int = 0) -> dict:
    key = jax.random.key(seed)
    ks = jax.random.split(key, 17)
    f32 = jnp.float32
    nl = DEPTH

    def nrm(k, shape, scale):
        return jax.random.normal(k, shape, f32) * scale

    return {
        'x': nrm(ks[0], (BATCH, SEQ, D_MODEL), 1.0),
        'norm1_gain': 1.0 + nrm(ks[1], (nl, D_MODEL), 0.01),
        'w_in': nrm(ks[2], (nl, D_MODEL, IN_WIDTH), D_MODEL ** -0.5),
        'q_norm_gain': 1.0 + nrm(ks[3], (nl, ATT_HEAD_DIM), 0.01),
        'k_norm_gain': 1.0 + nrm(ks[4], (nl, ATT_HEAD_DIM), 0.01),
        'dn_conv_w': nrm(ks[5], (nl, DN_CONV, 3 * DN_W), DN_CONV ** -0.5),
        'dn_a_log': jnp.log(jax.random.uniform(ks[6], (nl, DN_HEADS), f32, 1.0, 16.0)),
        'dn_dt_bias': nrm(ks[7], (nl, DN_HEADS), 0.1),
        'dn_out_norm_gain': 1.0 + nrm(ks[8], (nl, DN_HEAD_DIM), 0.01),
        'w_att_branch': nrm(ks[9], (nl, ATT_Q_W, D_MODEL), ATT_Q_W ** -0.5),
        'w_dn_branch': nrm(ks[10], (nl, DN_W, D_MODEL), DN_W ** -0.5),
        'w_o': nrm(ks[11], (nl, D_MODEL, D_MODEL), D_MODEL ** -0.5),
        'norm2_gain': 1.0 + nrm(ks[12], (nl, D_MODEL), 0.01),
        'peer_w_query': nrm(ks[13], (nl, D_MODEL, PEER_HEADS * PEER_KEY_DIM), D_MODEL ** -0.5),
        'peer_sub_keys': nrm(ks[14], (nl, PEER_HEADS, 2, PEER_N_KEYS, PEER_KEY_DIM // 2), (PEER_KEY_DIM // 2) ** -0.5),
        'peer_u': nrm(ks[15], (nl, PEER_N_EXPERTS, D_MODEL), D_MODEL ** -0.5),
        'peer_v': nrm(ks[16], (nl, PEER_N_EXPERTS, D_MODEL), PEER_HEADS ** -0.5),
    }


def reference(x, norm1_gain, w_in, q_norm_gain, k_norm_gain, dn_conv_w, dn_a_log, dn_dt_bias,
              dn_out_norm_gain, w_att_branch, w_dn_branch, w_o, norm2_gain, peer_w_query,
              peer_sub_keys, peer_u, peer_v):
    b, t, _ = x.shape
    k_sel = min(TOPK_MAX, t // 4)
    cuts = np.cumsum(IN_SPLITS)[:-1].tolist()
    for layer in range(DEPTH):
        h = rms_norm(x, norm1_gain[layer])
        proj = h @ w_in[layer]
        (aq, ak, av, iq, ik, iw, dq, dk, dv, dz, da, db, ga, gb) = jnp.split(proj, cuts, axis=-1)

        aq = rms_norm(aq.reshape(b, t, ATT_HEADS, ATT_HEAD_DIM), q_norm_gain[layer])
        ak = rms_norm(ak.reshape(b, t, ATT_KV_HEADS, ATT_HEAD_DIM), k_norm_gain[layer])
        av = av.reshape(b, t, ATT_KV_HEADS, ATT_HEAD_DIM)
        iq = iq.reshape(b, t, IDX_HEADS, IDX_HEAD_DIM)
        y_att = dsa_attention(aq, ak, av, iq, ik, iw, k_sel)

        qkv = jax.nn.silu(causal_depthwise_conv(jnp.concatenate([dq, dk, dv], axis=-1), dn_conv_w[layer]))
        dq, dk, dv = jnp.split(qkv, 3, axis=-1)
        shp = (b, t, DN_HEADS, DN_HEAD_DIM)
        decay = -jnp.exp(dn_a_log[layer].astype(jnp.float32)) * jax.nn.softplus(
            da.astype(jnp.float32) + dn_dt_bias[layer].astype(jnp.float32))
        beta = jax.nn.sigmoid(db.astype(jnp.float32))
        o_dn = gated_deltanet(dq.reshape(shp), dk.reshape(shp), dv.reshape(shp), decay, beta)
        y_dn = (rms_norm(o_dn, dn_out_norm_gain[layer]) * jax.nn.silu(dz.reshape(shp))).reshape(b, t, DN_W)

        merged = (jax.nn.sigmoid(ga) * (y_att @ w_att_branch[layer])
                  + jax.nn.sigmoid(gb) * (y_dn @ w_dn_branch[layer]))
        x = x + merged @ w_o[layer]

        h2 = rms_norm(x, norm2_gain[layer])
        x = x + peer(h2, peer_w_query[layer], peer_sub_keys[layer], peer_u[layer], peer_v[layer])
    return x
```

```python
import functools

import jax
import jax.numpy as jnp
import numpy as np
from jax import lax
from jax.experimental import pallas as pl
from jax.experimental.pallas import tpu as pltpu

F32 = jnp.float32
BF16 = jnp.bfloat16
I32 = jnp.int32

ATT_HEADS = 8
ATT_KV_HEADS = 2
ATT_HEAD_DIM = 64
IDX_HEADS = 8
IDX_HEAD_DIM = 64
TOPK_MAX = 256
DN_HEADS = 4
DN_HEAD_DIM = 128
DN_CONV = 4
DN_CHUNK = 64
PEER_HEADS = 8
PEER_N_KEYS = 128
PEER_KEY_DIM = 256
PEER_TOPK = 16
EPS = 1e-6

ATT_Q_W = ATT_HEADS * ATT_HEAD_DIM
ATT_KV_W = ATT_KV_HEADS * ATT_HEAD_DIM
IDX_Q_W = IDX_HEADS * IDX_HEAD_DIM
DN_W = DN_HEADS * DN_HEAD_DIM

LANES = 128
SUBLANES = 8
VMEM_LIMIT = 56 * 1024 * 1024

NEG_INF_KEY = int(np.int32(np.uint32(0xFF800000) ^ np.uint32(0x7FFFFFFF)))
INT_MAX = int(np.iinfo(np.int32).max)

NT_DIMS = (((1,), (1,)), ((), ()))
TN_DIMS = (((0,), (0,)), ((), ()))


def _sigmoid(x):
    return 1.0 / (1.0 + jnp.exp(-x))


def _softplus(x):
    return jnp.maximum(x, 0.0) + jnp.log(1.0 + jnp.exp(-jnp.abs(x)))


def _dot(a, b):
    return jnp.dot(a.astype(BF16), b.astype(BF16), preferred_element_type=F32)


def _dot_nt(a, b):
    return lax.dot_general(a.astype(BF16), b.astype(BF16), NT_DIMS, preferred_element_type=F32)


def _split2(x):
    hi = x.astype(BF16)
    lo = (x - hi.astype(F32)).astype(BF16)
    return hi, lo


def _dot_hi(a, b):
    a1, a2 = _split2(a)
    b1, b2 = _split2(b)
    out = jnp.dot(a1, b1, preferred_element_type=F32)
    out = out + jnp.dot(a1, b2, preferred_element_type=F32)
    out = out + jnp.dot(a2, b1, preferred_element_type=F32)
    return out


def _dot_exact_rhs(a, b_exact):
    a1 = a.astype(BF16)
    r1 = a - a1.astype(F32)
    a2 = r1.astype(BF16)
    a3 = (r1 - a2.astype(F32)).astype(BF16)
    out = jnp.dot(a1, b_exact, preferred_element_type=F32)
    out = out + jnp.dot(a2, b_exact, preferred_element_type=F32)
    out = out + jnp.dot(a3, b_exact, preferred_element_type=F32)
    return out


def _dot_exact_lhs(a_exact, b):
    b1 = b.astype(BF16)
    r1 = b - b1.astype(F32)
    b2 = r1.astype(BF16)
    b3 = (r1 - b2.astype(F32)).astype(BF16)
    out = jnp.dot(a_exact, b1, preferred_element_type=F32)
    out = out + jnp.dot(a_exact, b2, preferred_element_type=F32)
    out = out + jnp.dot(a_exact, b3, preferred_element_type=F32)
    return out


C_AQ = 0
C_AK = C_AQ + ATT_Q_W
C_AV = C_AK + ATT_KV_W
C_IQ = C_AV + ATT_KV_W
C_SM = C_IQ + IDX_Q_W
C_DQKV = C_SM + LANES
C_DZ = C_DQKV + 3 * DN_W
C_GAB = C_DZ + DN_W
SM_IW = IDX_HEAD_DIM
SM_DAB = SM_IW + IDX_HEADS


def _in_proj_body(x_ref, g1_ref, w_ref, wabt_ref, qg_ref, kg_ref,
                  q_ref, k_ref, v_ref, iq_ref, ik_ref, iw_ref, dqkv_ref, dz_ref, dab_ref, dabt_ref, gab_ref):
    x = x_ref[...]
    ms = jnp.mean(x * x, axis=-1, keepdims=True)
    h = (x * lax.rsqrt(ms + EPS) * g1_ref[...]).astype(BF16)

    def proj(c0, c1):
        return jnp.dot(h, w_ref[:, c0:c1], preferred_element_type=F32)

    def head_norm(blk, gain):
        return blk * lax.rsqrt(jnp.mean(blk * blk, axis=-1, keepdims=True) + EPS) * gain

    aq = proj(C_AQ, C_AK)
    qg = qg_ref[...] * (ATT_HEAD_DIM ** -0.5)
    for hd in range(ATT_HEADS):
        sl = slice(hd * ATT_HEAD_DIM, (hd + 1) * ATT_HEAD_DIM)
        q_ref[:, sl] = head_norm(aq[:, sl], qg).astype(q_ref.dtype)
    ak = proj(C_AK, C_AV)
    kg = kg_ref[...]
    for hd in range(ATT_KV_HEADS):
        sl = slice(hd * ATT_HEAD_DIM, (hd + 1) * ATT_HEAD_DIM)
        k_ref[:, sl] = head_norm(ak[:, sl], kg).astype(k_ref.dtype)
    v_ref[...] = proj(C_AV, C_IQ).astype(v_ref.dtype)
    iq_ref[...] = proj(C_IQ, C_SM).astype(iq_ref.dtype)
    sm = proj(C_SM, C_DQKV)
    ik_ref[...] = sm[:, :IDX_HEAD_DIM].astype(ik_ref.dtype)
    iw_ref[...] = sm[:, SM_IW:SM_IW + IDX_HEADS]
    dab_ref[...] = sm[:, SM_DAB:SM_DAB + 2 * DN_HEADS]
    dabt_ref[...] = lax.dot_general(wabt_ref[...], h, NT_DIMS, preferred_element_type=F32)
    dqkv_ref[...] = proj(C_DQKV, C_DZ)
    dz_ref[...] = proj(C_DZ, C_GAB)
    gab_ref[...] = proj(C_GAB, C_GAB + 2 * x.shape[1])


def _in_proj(xf, g1, w_in, q_gain, k_gain, tm):
    n, d = xf.shape
    cuts = np.cumsum([ATT_Q_W, ATT_KV_W, ATT_KV_W, IDX_Q_W, IDX_HEAD_DIM, IDX_HEADS,
                      DN_W, DN_W, DN_W, DN_W, DN_HEADS, DN_HEADS, d, d])[:-1].tolist()
    aq, ak, av, iq, ik, iw, dq, dk, dv, dz, da, db, ga, gb = jnp.split(w_in, cuts, axis=-1)
    pad = jnp.zeros((d, LANES - IDX_HEAD_DIM - IDX_HEADS - 2 * DN_HEADS), w_in.dtype)
    w_all = jnp.concatenate([aq, ak, av, iq, ik, iw, da, db, pad, dq, dk, dv, dz, ga, gb], axis=-1).astype(BF16)
    wabt = jnp.concatenate([da, db], axis=-1).T.astype(BF16)
    wtot = w_all.shape[1]
    row = lambda i: (i, 0)
    fixed = lambda i: (0, 0)
    out_shapes = (
        jax.ShapeDtypeStruct((n, ATT_Q_W), BF16),
        jax.ShapeDtypeStruct((n, ATT_KV_W), BF16),
        jax.ShapeDtypeStruct((n, ATT_KV_W), BF16),
        jax.ShapeDtypeStruct((n, IDX_Q_W), BF16),
        jax.ShapeDtypeStruct((n, IDX_HEAD_DIM), BF16),
        jax.ShapeDtypeStruct((n, IDX_HEADS), F32),
        jax.ShapeDtypeStruct((n, 3 * DN_W), F32),
        jax.ShapeDtypeStruct((n, DN_W), F32),
        jax.ShapeDtypeStruct((n, 2 * DN_HEADS), F32),
        jax.ShapeDtypeStruct((2 * DN_HEADS, n), F32),
        jax.ShapeDtypeStruct((n, 2 * d), F32),
    )
    out_specs = tuple(
        pl.BlockSpec((2 * DN_HEADS, tm), lambda i: (0, i)) if s.shape[0] != n
        else pl.BlockSpec((tm, s.shape[1]), row)
        for s in out_shapes)
    return pl.pallas_call(
        _in_proj_body,
        grid=(n // tm,),
        in_specs=[
            pl.BlockSpec((tm, d), row),
            pl.BlockSpec((1, d), fixed),
            pl.BlockSpec((d, wtot), fixed),
            pl.BlockSpec((2 * DN_HEADS, d), fixed),
            pl.BlockSpec((1, ATT_HEAD_DIM), fixed),
            pl.BlockSpec((1, ATT_HEAD_DIM), fixed),
        ],
        out_specs=out_specs,
        out_shape=out_shapes,
        compiler_params=pltpu.CompilerParams(dimension_semantics=("parallel",), vmem_limit_bytes=VMEM_LIMIT),
    )(xf, g1.reshape(1, d), w_all, wabt, q_gain.reshape(1, -1), k_gain.reshape(1, -1))


def _dsa_body(q_ref, iq_ref, iw_ref, k_ref, v_ref, ik_ref, o_ref,
              key_ref, m_ref, l_ref, acc_ref, *, tq, tk, ksel):
    i = pl.program_id(1)
    q0 = i * tq
    n_kb = (q0 + tq + tk - 1) // tk
    qpos = q0 + lax.broadcasted_iota(I32, (tq, 1), 0)
    iw = iw_ref[...]
    iq = iq_ref[...]

    def score_body(kb, carry):
        k0 = pl.multiple_of(kb * tk, tk)
        ikb = ik_ref[pl.ds(k0, tk), :]
        acc = jnp.zeros((tq, tk), F32)
        for hd in range(IDX_HEADS):
            sl = slice(hd * IDX_HEAD_DIM, (hd + 1) * IDX_HEAD_DIM)
            dots = lax.dot_general(iq[:, sl], ikb, NT_DIMS, preferred_element_type=F32)
            acc = acc + iw[:, hd:hd + 1] * jnp.maximum(dots, 0.0)
        bits = lax.bitcast_convert_type(acc, I32)
        keys = jnp.where(bits >= 0, bits, bits ^ INT_MAX)
        kpos = k0 + lax.broadcasted_iota(I32, (1, tk), 1)
        key_ref[kb] = jnp.where(kpos <= qpos, keys, NEG_INF_KEY)
        return carry

    lax.fori_loop(0, n_kb, score_body, 0)

    def count_ge(thr):
        def body(kb, c):
            ind = jnp.where(key_ref[kb] >= thr, 1.0, 0.0)
            part = ind[:, 0:LANES]
            for j in range(1, tk // LANES):
                part = part + ind[:, j * LANES:(j + 1) * LANES]
            return c + part
        c = lax.fori_loop(0, n_kb, body, jnp.zeros((tq, LANES), F32))
        return jnp.sum(c, axis=1, keepdims=True)

    def bisect_body(_, st):
        lo, hi, c_lo, c_hi = st
        mid = (lo >> 1) + (hi >> 1) + (lo & hi & 1)
        c = count_ge(mid)
        ge = c >= ksel
        return (jnp.where(ge, mid, lo), jnp.where(ge, hi, mid),
                jnp.where(ge, c, c_lo), jnp.where(ge, c_hi, c))

    lo0 = jnp.full((tq, 1), NEG_INF_KEY + 1, I32)
    hi0 = jnp.full((tq, 1), INT_MAX, I32)
    zero = jnp.zeros((tq, 1), F32)
    thr, _, c_thr, c_above = lax.fori_loop(0, 32, bisect_body, (lo0, hi0, zero, zero))

    need = ksel - c_above
    has_excess = jnp.max(jnp.where(c_thr > ksel, 1.0, 0.0)) > 0.0

    @pl.when(has_excess)
    def _():
        upper = jnp.where(lax.broadcasted_iota(I32, (tk, tk), 0) <= lax.broadcasted_iota(I32, (tk, tk), 1),
                          1.0, 0.0).astype(BF16)

        def body(kb, seen):
            keys = key_ref[kb]
            tie = keys == thr
            tie_f = jnp.where(tie, 1.0, 0.0)
            prefix = jnp.dot(tie_f.astype(BF16), upper, preferred_element_type=F32) + seen
            drop = jnp.logical_and(tie, prefix > need)
            key_ref[kb] = jnp.where(drop, NEG_INF_KEY, keys)
            return seen + jnp.sum(tie_f, axis=1, keepdims=True)

        lax.fori_loop(0, n_kb, body, jnp.zeros((tq, 1), F32))

    m_ref[...] = jnp.full(m_ref.shape, -1e30, F32)
    l_ref[...] = jnp.zeros(l_ref.shape, F32)
    acc_ref[...] = jnp.zeros(acc_ref.shape, F32)
    q = q_ref[...]
    grp = ATT_HEADS // ATT_KV_HEADS

    def att_body(kb, carry):
        k0 = pl.multiple_of(kb * tk, tk)
        kblk = k_ref[pl.ds(k0, tk), :]
        vblk = v_ref[pl.ds(k0, tk), :]
        sel = key_ref[kb] >= thr
        for hd in range(ATT_HEADS):
            g = hd // grp
            qs = slice(hd * ATT_HEAD_DIM, (hd + 1) * ATT_HEAD_DIM)
            ks = slice(g * ATT_HEAD_DIM, (g + 1) * ATT_HEAD_DIM)
            s = lax.dot_general(q[:, qs], kblk[:, ks], NT_DIMS, preferred_element_type=F32)
            s = jnp.where(sel, s, -1e30)
            m_old = m_ref[hd]
            m_new = jnp.maximum(m_old, jnp.max(s, axis=1, keepdims=True))
            p = jnp.exp(s - m_new)
            alpha = jnp.exp(m_old - m_new)
            l_ref[hd] = alpha * l_ref[hd] + jnp.sum(p, axis=1, keepdims=True)
            acc_ref[hd] = alpha * acc_ref[hd] + jnp.dot(p.astype(BF16), vblk[:, ks], preferred_element_type=F32)
            m_ref[hd] = m_new
        return carry

    lax.fori_loop(0, n_kb, att_body, 0)
    for hd in range(ATT_HEADS):
        qs = slice(hd * ATT_HEAD_DIM, (hd + 1) * ATT_HEAD_DIM)
        o_ref[:, qs] = (acc_ref[hd] / l_ref[hd]).astype(o_ref.dtype)


def _dsa(q, k, v, iq, ik, iw, b, t, tq, tk):
    n = b * t
    nq = t // tq
    ksel = min(TOPK_MAX, t // 4)
    qrow = lambda bi, i: (bi * nq + i, 0)
    brow = lambda bi, i: (bi, 0)
    return pl.pallas_call(
        functools.partial(_dsa_body, tq=tq, tk=tk, ksel=ksel),
        grid=(b, nq),
        in_specs=[
            pl.BlockSpec((tq, ATT_Q_W), qrow),
            pl.BlockSpec((tq, IDX_Q_W), qrow),
            pl.BlockSpec((tq, IDX_HEADS), qrow),
            pl.BlockSpec((t, ATT_KV_W), brow),
            pl.BlockSpec((t, ATT_KV_W), brow),
            pl.BlockSpec((t, IDX_HEAD_DIM), brow),
        ],
        out_specs=pl.BlockSpec((tq, ATT_Q_W), qrow),
        out_shape=jax.ShapeDtypeStruct((n, ATT_Q_W), BF16),
        scratch_shapes=[
            pltpu.VMEM((t // tk, tq, tk), I32),
            pltpu.VMEM((ATT_HEADS, tq, 1), F32),
            pltpu.VMEM((ATT_HEADS, tq, 1), F32),
            pltpu.VMEM((ATT_HEADS, tq, ATT_HEAD_DIM), F32),
        ],
        compiler_params=pltpu.CompilerParams(dimension_semantics=("parallel", "arbitrary"),
                                             vmem_limit_bytes=VMEM_LIMIT),
    )(q, iq, iw, k, v, ik)


def _inv_unit_lower(a, eye):
    x = eye - a
    p = a
    for _ in range(5):
        p = _dot_hi(p, p)
        x = x + _dot_hi(x, p)
    return x


def _deltanet_body(x_ref, dz_ref, dab_ref, dabt_ref, cw_ref, alog_r_ref, bias_r_ref, alog_c_ref, bias_c_ref,
                   og_ref, y_ref, carry_ref, state_ref, *, tb):
    c = DN_CHUNK
    d = DN_HEAD_DIM
    nh = DN_HEADS

    @pl.when(pl.program_id(1) == 0)
    def _():
        carry_ref[...] = jnp.zeros(carry_ref.shape, F32)
        state_ref[...] = jnp.zeros(state_ref.shape, F32)

    xb = x_ref[...]
    xx = jnp.concatenate([carry_ref[...], xb], axis=0)
    cw = cw_ref[...]
    off = SUBLANES - (DN_CONV - 1)
    conv = cw[0:1] * xx[off:off + tb]
    for j in range(1, DN_CONV):
        conv = conv + cw[j:j + 1] * xx[off + j:off + j + tb]
    carry_ref[...] = xb[tb - SUBLANES:tb]
    qkv = conv * _sigmoid(conv)

    dab = dab_ref[...]
    g_col = -jnp.exp(alog_r_ref[...]) * _softplus(dab[:, 0:nh] + bias_r_ref[...])
    beta_col = _sigmoid(dab[:, nh:2 * nh])
    g_row = -jnp.exp(alog_c_ref[...]) * _softplus(dabt_ref[0:nh, :] + bias_c_ref[...])

    ri = lax.broadcasted_iota(I32, (tb, tb), 0)
    ci = lax.broadcasted_iota(I32, (tb, tb), 1)
    same_chunk = (ri // c) == (ci // c)
    lower_blk = jnp.where(jnp.logical_and(same_chunk, ri >= ci), 1.0, 0.0).astype(BF16)
    upper_blk = jnp.where(jnp.logical_and(same_chunk, ri <= ci), 1.0, 0.0).astype(BF16)
    gc_col = _dot_exact_lhs(lower_blk, g_col)
    gc_row = _dot_exact_rhs(g_row, upper_blk)

    r64 = lax.broadcasted_iota(I32, (c, c), 0)
    c64 = lax.broadcasted_iota(I32, (c, c), 1)
    tri = r64 >= c64
    strict = r64 > c64
    eye = jnp.where(r64 == c64, 1.0, 0.0)
    og = og_ref[...]

    states = [state_ref[hd] for hd in range(nh)]
    for ch in range(tb // c):
        rs = slice(ch * c, (ch + 1) * c)
        for hd in range(nh):
            ls = slice(hd * d, (hd + 1) * d)
            qc = qkv[rs, hd * d:(hd + 1) * d]
            kc = qkv[rs, nh * d + hd * d:nh * d + (hd + 1) * d]
            vc = qkv[rs, 2 * nh * d + hd * d:2 * nh * d + (hd + 1) * d]
            qc = qc * lax.rsqrt(jnp.sum(qc * qc, axis=-1, keepdims=True) + EPS) * (d ** -0.5)
            kc = kc * lax.rsqrt(jnp.sum(kc * kc, axis=-1, keepdims=True) + EPS)
            bcol = beta_col[rs, hd:hd + 1]
            gcol = gc_col[rs, hd:hd + 1]
            grow = gc_row[hd:hd + 1, rs]
            decay = jnp.exp(jnp.where(tri, gcol - grow, -jnp.inf))
            kbeta = kc * bcol
            a_mat = jnp.where(strict, _dot_nt(kbeta, kc) * decay, 0.0)
            t_mat = _inv_unit_lower(a_mat, eye)
            eg = jnp.exp(gcol)
            rhs = jnp.concatenate([vc * bcol, kbeta * eg], axis=1)
            sol = _dot(t_mat, rhs)
            u, w = sol[:, :d], sol[:, d:]
            intra = jnp.where(tri, _dot_nt(qc, kc) * decay, 0.0)
            st = states[hd]
            both = _dot(jnp.concatenate([w, qc * eg], axis=0), st)
            v_new = u - both[:c]
            o = both[c:] + _dot(intra, v_new)
            g_last = gcol[c - 1:c]
            kdec = kc * jnp.exp(g_last - gcol)
            states[hd] = st * jnp.exp(g_last) + lax.dot_general(
                kdec.astype(BF16), v_new.astype(BF16), TN_DIMS, preferred_element_type=F32)
            on = o * lax.rsqrt(jnp.mean(o * o, axis=-1, keepdims=True) + EPS) * og
            z = dz_ref[rs, ls]
            y_ref[rs, ls] = (on * (z * _sigmoid(z))).astype(y_ref.dtype)
    for hd in range(nh):
        state_ref[hd] = states[hd]


def _deltanet(dqkv, dz, dab, dabt, conv_w, a_log, dt_bias, out_gain, b, t, tb):
    n = b * t
    nb = t // tb
    row = lambda bi, j: (bi * nb + j, 0)
    col = lambda bi, j: (0, bi * nb + j)
    fixed = lambda bi, j: (0, 0)
    nh = DN_HEADS
    return pl.pallas_call(
        functools.partial(_deltanet_body, tb=tb),
        grid=(b, nb),
        in_specs=[
            pl.BlockSpec((tb, 3 * DN_W), row),
            pl.BlockSpec((tb, DN_W), row),
            pl.BlockSpec((tb, 2 * nh), row),
            pl.BlockSpec((2 * nh, tb), col),
            pl.BlockSpec((DN_CONV, 3 * DN_W), fixed),
            pl.BlockSpec((1, nh), fixed),
            pl.BlockSpec((1, nh), fixed),
            pl.BlockSpec((nh, 1), fixed),
            pl.BlockSpec((nh, 1), fixed),
            pl.BlockSpec((1, DN_HEAD_DIM), fixed),
        ],
        out_specs=pl.BlockSpec((tb, DN_W), row),
        out_shape=jax.ShapeDtypeStruct((n, DN_W), BF16),
        scratch_shapes=[
            pltpu.VMEM((SUBLANES, 3 * DN_W), F32),
            pltpu.VMEM((nh, DN_HEAD_DIM, DN_HEAD_DIM), F32),
        ],
        compiler_params=pltpu.CompilerParams(dimension_semantics=("parallel", "arbitrary"),
                                             vmem_limit_bytes=VMEM_LIMIT),
    )(dqkv, dz, dab, dabt, conv_w, a_log.reshape(1, nh), dt_bias.reshape(1, nh),
      a_log.reshape(nh, 1), dt_bias.reshape(nh, 1), out_gain.reshape(1, -1))


def _merge_body(x_ref, ya_ref, yd_ref, gab_ref, wa_ref, wb_ref, wo_ref, o_ref):
    dm = x_ref.shape[1]
    gab = gab_ref[...]
    a = jnp.dot(ya_ref[...], wa_ref[...], preferred_element_type=F32)
    bb = jnp.dot(yd_ref[...], wb_ref[...], preferred_element_type=F32)
    merged = _sigmoid(gab[:, :dm]) * a + _sigmoid(gab[:, dm:]) * bb
    o_ref[...] = x_ref[...] + jnp.dot(merged.astype(BF16), wo_ref[...], preferred_element_type=F32)


def _merge(xf, y_att, y_dn, gab, w_a, w_b, w_o, tm):
    n, d = xf.shape
    row = lambda i: (i, 0)
    fixed = lambda i: (0, 0)
    return pl.pallas_call(
        _merge_body,
        grid=(n // tm,),
        in_specs=[
            pl.BlockSpec((tm, d), row),
            pl.BlockSpec((tm, ATT_Q_W), row),
            pl.BlockSpec((tm, DN_W), row),
            pl.BlockSpec((tm, 2 * d), row),
            pl.BlockSpec((ATT_Q_W, d), fixed),
            pl.BlockSpec((DN_W, d), fixed),
            pl.BlockSpec((d, d), fixed),
        ],
        out_specs=pl.BlockSpec((tm, d), row),
        out_shape=jax.ShapeDtypeStruct((n, d), F32),
        compiler_params=pltpu.CompilerParams(dimension_semantics=("parallel",), vmem_limit_bytes=VMEM_LIMIT),
    )(xf, y_att, y_dn, gab, w_a.astype(BF16), w_b.astype(BF16), w_o.astype(BF16))


def _top16_rows(s, payload):
    rows = lax.broadcasted_iota(I32, s.shape, 0)
    big = s.shape[0]
    vals, pays = [], []
    for _ in range(PEER_TOPK):
        m = jnp.max(s, axis=0, keepdims=True)
        am = jnp.min(jnp.where(s == m, rows, big), axis=0, keepdims=True)
        hit = rows == am
        vals.append(m)
        pays.append(jnp.sum(jnp.where(hit, payload, 0), axis=0, keepdims=True))
        s = jnp.where(hit, -jnp.inf, s)
    return jnp.concatenate(vals, axis=0), jnp.concatenate(pays, axis=0)


def _peer_route_body(x_ref, g2_ref, wq_ref, sk_ref, ids_ref, gates_ref, *, tm):
    x = x_ref[...]
    ms = jnp.mean(x * x, axis=-1, keepdims=True)
    h = (x * lax.rsqrt(ms + EPS) * g2_ref[...]).astype(BF16)
    q = jnp.dot(h, wq_ref[...], preferred_element_type=F32).astype(BF16)
    half = PEER_KEY_DIM // 2
    key_rows = lax.broadcasted_iota(I32, (PEER_N_KEYS, tm), 0)
    for hd in range(PEER_HEADS):
        tops = []
        for p in range(2):
            c0 = hd * PEER_KEY_DIM + p * half
            st = lax.dot_general(sk_ref[2 * hd + p], q[:, c0:c0 + half], NT_DIMS,
                                 preferred_element_type=F32)
            tops.append(_top16_rows(st, key_rows))
        (s0, i0), (s1, i1) = tops
        cand_s = jnp.concatenate([s0[r:r + 1] + s1 for r in range(PEER_TOPK)], axis=0)
        cand_e = jnp.concatenate([i0[r:r + 1] * PEER_N_KEYS + i1 for r in range(PEER_TOPK)], axis=0)
        best, expert = _top16_rows(cand_s, cand_e)
        e = jnp.exp(best - best[0:1])
        gate = e / jnp.sum(e, axis=0, keepdims=True)
        rs = slice(hd * PEER_TOPK, (hd + 1) * PEER_TOPK)
        ids_ref[rs, :] = expert
        gates_ref[rs, :] = gate


def _peer_route(x1, g2, w_query, sub_keys, tm):
    n, d = x1.shape
    nsel = PEER_HEADS * PEER_TOPK
    half = PEER_KEY_DIM // 2
    sk = sub_keys.reshape(PEER_HEADS * 2, PEER_N_KEYS, half).astype(BF16)
    return pl.pallas_call(
        functools.partial(_peer_route_body, tm=tm),
        grid=(n // tm,),
        in_specs=[
            pl.BlockSpec((tm, d), lambda i: (i, 0)),
            pl.BlockSpec((1, d), lambda i: (0, 0)),
            pl.BlockSpec((d, PEER_HEADS * PEER_KEY_DIM), lambda i: (0, 0)),
            pl.BlockSpec((PEER_HEADS * 2, PEER_N_KEYS, half), lambda i: (0, 0, 0)),
        ],
        out_specs=(pl.BlockSpec((nsel, tm), lambda i: (0, i)),
                   pl.BlockSpec((nsel, tm), lambda i: (0, i))),
        out_shape=(jax.ShapeDtypeStruct((nsel, n), I32), jax.ShapeDtypeStruct((nsel, n), F32)),
        compiler_params=pltpu.CompilerParams(dimension_semantics=("parallel",), vmem_limit_bytes=VMEM_LIMIT),
    )(x1, g2.reshape(1, d), w_query.astype(BF16), sk)


PEER_SLOTS = 4


def _peer_apply_body(ids_ref, x_ref, g2_ref, gates_ref, uv_hbm, o_ref, buf, coef_ref, sems, *, tb):
    nsel = PEER_HEADS * PEER_TOPK
    nchunk = x_ref.shape[1]
    dm = nchunk * LANES

    def row_copy(tok, kk, slot):
        return pltpu.make_async_copy(uv_hbm.at[ids_ref[tok, kk]], buf.at[slot, kk], sems.at[slot])

    def issue(tok, slot):
        def body(kk, carry):
            row_copy(tok, kk, slot).start()
            return carry
        lax.fori_loop(0, nsel, body, 0)

    def wait_all(slot):
        pltpu.make_async_copy(uv_hbm.at[pl.ds(0, nsel)], buf.at[slot], sems.at[slot]).wait()

    for s in range(PEER_SLOTS - 1):
        issue(s, s)

    eye = lax.broadcasted_iota(I32, (nsel, nsel), 0) == lax.broadcasted_iota(I32, (nsel, nsel), 1)
    g2 = g2_ref[...]

    def tok_body(t, carry):
        slot = lax.rem(t, PEER_SLOTS)
        nxt = t + PEER_SLOTS - 1

        @pl.when(nxt < tb)
        def _():
            issue(nxt, lax.rem(nxt, PEER_SLOTS))

        xt = x_ref[t]
        ssq = jnp.sum(jnp.sum(xt * xt, axis=1, keepdims=True), axis=0, keepdims=True)
        h8 = xt * lax.rsqrt(ssq * (1.0 / dm) + EPS) * g2
        grow = gates_ref[t]
        gcol = jnp.sum(jnp.where(eye, jnp.broadcast_to(grow, (nsel, nsel)), 0.0), axis=1, keepdims=True)

        wait_all(slot)
        tile = buf.at[slot]
        u = tile[:, 0:nchunk, :]
        act = jnp.sum(jnp.sum(u * h8[None], axis=1), axis=1, keepdims=True)
        gelu = 0.5 * act * (1.0 + lax.erf(act * (2.0 ** -0.5)))
        coef_ref[...] = jnp.broadcast_to(gcol * gelu, (nsel, LANES))
        acc = jnp.zeros((nchunk, LANES), F32)
        for kk in range(nsel):
            acc = acc + coef_ref[kk:kk + 1, :] * tile[kk, nchunk:2 * nchunk, :]
        o_ref[t] = xt + acc
        return carry

    lax.fori_loop(0, tb, tok_body, 0)


def _peer_apply(x1, g2, ids, gates, peer_u, peer_v, tb):
    n, d = x1.shape
    nsel = PEER_HEADS * PEER_TOPK
    ne = peer_u.shape[0]
    nchunk = d // LANES
    uv = jnp.concatenate([peer_u.reshape(ne, nchunk, LANES), peer_v.reshape(ne, nchunk, LANES)], axis=1)
    out = pl.pallas_call(
        functools.partial(_peer_apply_body, tb=tb),
        grid=(n // tb,),
        in_specs=[
            pl.BlockSpec((tb, nsel), lambda i: (i, 0), memory_space=pltpu.SMEM),
            pl.BlockSpec((tb, nchunk, LANES), lambda i: (i, 0, 0)),
            pl.BlockSpec((nchunk, LANES), lambda i: (0, 0)),
            pl.BlockSpec((tb, 1, nsel), lambda i: (i, 0, 0)),
            pl.BlockSpec(memory_space=pl.ANY),
        ],
        out_specs=pl.BlockSpec((tb, nchunk, LANES), lambda i: (i, 0, 0)),
        out_shape=jax.ShapeDtypeStruct((n, nchunk, LANES), F32),
        scratch_shapes=[
            pltpu.VMEM((PEER_SLOTS, nsel, 2 * nchunk, LANES), F32),
            pltpu.VMEM((nsel, LANES), F32),
            pltpu.SemaphoreType.DMA((PEER_SLOTS,)),
        ],
        compiler_params=pltpu.CompilerParams(dimension_semantics=("arbitrary",), vmem_limit_bytes=VMEM_LIMIT),
    )(ids, x1.reshape(n, nchunk, LANES), g2.reshape(nchunk, LANES), gates.reshape(n, 1, nsel), uv)
    return out.reshape(n, d)


def _block_sizes(t):
    return dict(
        tm_proj=256,
        tq=min(256, t),
        tk=min(512, t),
        tb_dn=min(256, t),
        tm_merge=512,
        tm_route=LANES,
        tb_peer=64,
    )


def kernel(x, norm1_gain, w_in, q_norm_gain, k_norm_gain, dn_conv_w, dn_a_log, dn_dt_bias, dn_out_norm_gain,
           w_att_branch, w_dn_branch, w_o, norm2_gain, peer_w_query, peer_sub_keys, peer_u, peer_v):
    b, t, d = x.shape
    n = b * t
    bs = _block_sizes(t)
    xf = x.reshape(n, d)
    for layer in range(w_in.shape[0]):
        (q, k, v, iq, ik, iw, dqkv, dz, dab, dabt, gab) = _in_proj(
            xf, norm1_gain[layer], w_in[layer], q_norm_gain[layer], k_norm_gain[layer], bs["tm_proj"])
        y_att = _dsa(q, k, v, iq, ik, iw, b, t, bs["tq"], bs["tk"])
        y_dn = _deltanet(dqkv, dz, dab, dabt, dn_conv_w[layer], dn_a_log[layer], dn_dt_bias[layer],
                         dn_out_norm_gain[layer], b, t, bs["tb_dn"])
        x1 = _merge(xf, y_att, y_dn, gab, w_att_branch[layer], w_dn_branch[layer], w_o[layer], bs["tm_merge"])
        ids_t, gates_t = _peer_route(x1, norm2_gain[layer], peer_w_query[layer], peer_sub_keys[layer],
                                     bs["tm_route"])
        xf = _peer_apply(x1, norm2_gain[layer], ids_t.T, gates_t.T, peer_u[layer], peer_v[layer], bs["tb_peer"])
    return xf.reshape(b, t, d)
```

```python
import functools

import jax
import jax.numpy as jnp
import numpy as np
from jax import lax
from jax.experimental import pallas as pl
from jax.experimental.pallas import tpu as pltpu

F32 = jnp.float32
BF16 = jnp.bfloat16
I32 = jnp.int32

ATT_HEADS = 8
ATT_KV_HEADS = 2
ATT_HEAD_DIM = 64
IDX_HEADS = 8
IDX_HEAD_DIM = 64
TOPK_MAX = 256
DN_HEADS = 4
DN_HEAD_DIM = 128
DN_CONV = 4
DN_CHUNK = 64
PEER_HEADS = 8
PEER_N_KEYS = 128
PEER_KEY_DIM = 256
PEER_TOPK = 16
EPS = 1e-6

ATT_Q_W = ATT_HEADS * ATT_HEAD_DIM
ATT_KV_W = ATT_KV_HEADS * ATT_HEAD_DIM
IDX_Q_W = IDX_HEADS * IDX_HEAD_DIM
DN_W = DN_HEADS * DN_HEAD_DIM

LANES = 128
SUBLANES = 8
VMEM_LIMIT = 56 * 1024 * 1024

NEG_INF_KEY = int(np.int32(np.uint32(0xFF800000) ^ np.uint32(0x7FFFFFFF)))
INT_MAX = int(np.iinfo(np.int32).max)

NT_DIMS = (((1,), (1,)), ((), ()))
TN_DIMS = (((0,), (0,)), ((), ()))


def _sigmoid(x):
    return 1.0 / (1.0 + jnp.exp(-x))


def _softplus(x):
    return jnp.maximum(x, 0.0) + jnp.log(1.0 + jnp.exp(-jnp.abs(x)))


def _dot(a, b):
    return jnp.dot(a.astype(BF16), b.astype(BF16), preferred_element_type=F32)


def _dot_nt(a, b):
    return lax.dot_general(a.astype(BF16), b.astype(BF16), NT_DIMS, preferred_element_type=F32)


def _split2(x):
    hi = x.astype(BF16)
    lo = (x - hi.astype(F32)).astype(BF16)
    return hi, lo


def _dot_hi(a, b):
    a1, a2 = _split2(a)
    b1, b2 = _split2(b)
    out = jnp.dot(a1, b1, preferred_element_type=F32)
    out = out + jnp.dot(a1, b2, preferred_element_type=F32)
    out = out + jnp.dot(a2, b1, preferred_element_type=F32)
    return out


def _dot_exact_rhs(a, b_exact):
    a1 = a.astype(BF16)
    r1 = a - a1.astype(F32)
    a2 = r1.astype(BF16)
    a3 = (r1 - a2.astype(F32)).astype(BF16)
    out = jnp.dot(a1, b_exact, preferred_element_type=F32)
    out = out + jnp.dot(a2, b_exact, preferred_element_type=F32)
    out = out + jnp.dot(a3, b_exact, preferred_element_type=F32)
    return out


def _dot_exact_lhs(a_exact, b):
    b1 = b.astype(BF16)
    r1 = b - b1.astype(F32)
    b2 = r1.astype(BF16)
    b3 = (r1 - b2.astype(F32)).astype(BF16)
    out = jnp.dot(a_exact, b1, preferred_element_type=F32)
    out = out + jnp.dot(a_exact, b2, preferred_element_type=F32)
    out = out + jnp.dot(a_exact, b3, preferred_element_type=F32)
    return out


C_AQ = 0
C_AK = C_AQ + ATT_Q_W
C_AV = C_AK + ATT_KV_W
C_IQ = C_AV + ATT_KV_W
C_SM = C_IQ + IDX_Q_W
C_DQKV = C_SM + LANES
C_DZ = C_DQKV + 3 * DN_W
C_GAB = C_DZ + DN_W
SM_IW = IDX_HEAD_DIM
SM_DAB = SM_IW + IDX_HEADS


def _in_proj_body(x_ref, g1_ref, w_ref, wabt_ref, qg_ref, kg_ref,
                  q_ref, k_ref, v_ref, iq_ref, ik_ref, iw_ref, dqkv_ref, dz_ref, dab_ref, dabt_ref, gab_ref):
    x = x_ref[...]
    ms = jnp.mean(x * x, axis=-1, keepdims=True)
    h = (x * lax.rsqrt(ms + EPS) * g1_ref[...]).astype(BF16)

    def proj(c0, c1):
        return jnp.dot(h, w_ref[:, c0:c1], preferred_element_type=F32)

    def head_norm(blk, gain):
        return blk * lax.rsqrt(jnp.mean(blk * blk, axis=-1, keepdims=True) + EPS) * gain

    aq = proj(C_AQ, C_AK)
    qg = qg_ref[...] * (ATT_HEAD_DIM ** -0.5)
    for hd in range(ATT_HEADS):
        sl = slice(hd * ATT_HEAD_DIM, (hd + 1) * ATT_HEAD_DIM)
        q_ref[:, sl] = head_norm(aq[:, sl], qg).astype(q_ref.dtype)
    ak = proj(C_AK, C_AV)
    kg = kg_ref[...]
    for hd in range(ATT_KV_HEADS):
        sl = slice(hd * ATT_HEAD_DIM, (hd + 1) * ATT_HEAD_DIM)
        k_ref[:, sl] = head_norm(ak[:, sl], kg).astype(k_ref.dtype)
    v_ref[...] = proj(C_AV, C_IQ).astype(v_ref.dtype)
    iq_ref[...] = proj(C_IQ, C_SM).astype(iq_ref.dtype)
    sm = proj(C_SM, C_DQKV)
    ik_ref[...] = sm[:, :IDX_HEAD_DIM].astype(ik_ref.dtype)
    iw_ref[...] = sm[:, SM_IW:SM_IW + IDX_HEADS]
    dab_ref[...] = sm[:, SM_DAB:SM_DAB + 2 * DN_HEADS]
    dabt_ref[...] = lax.dot_general(wabt_ref[...], h, NT_DIMS, preferred_element_type=F32)
    dqkv_ref[...] = proj(C_DQKV, C_DZ)
    dz_ref[...] = proj(C_DZ, C_GAB)
    gab_ref[...] = proj(C_GAB, C_GAB + 2 * x.shape[1])


def _in_proj(xf, g1, w_in, q_gain, k_gain, tm):
    n, d = xf.shape
    cuts = np.cumsum([ATT_Q_W, ATT_KV_W, ATT_KV_W, IDX_Q_W, IDX_HEAD_DIM, IDX_HEADS,
                      DN_W, DN_W, DN_W, DN_W, DN_HEADS, DN_HEADS, d, d])[:-1].tolist()
    aq, ak, av, iq, ik, iw, dq, dk, dv, dz, da, db, ga, gb = jnp.split(w_in, cuts, axis=-1)
    pad = jnp.zeros((d, LANES - IDX_HEAD_DIM - IDX_HEADS - 2 * DN_HEADS), w_in.dtype)
    w_all = jnp.concatenate([aq, ak, av, iq, ik, iw, da, db, pad, dq, dk, dv, dz, ga, gb], axis=-1).astype(BF16)
    wabt = jnp.concatenate([da, db], axis=-1).T.astype(BF16)
    wtot = w_all.shape[1]
    row = lambda i: (i, 0)
    fixed = lambda i: (0, 0)
    out_shapes = (
        jax.ShapeDtypeStruct((n, ATT_Q_W), BF16),
        jax.ShapeDtypeStruct((n, ATT_KV_W), BF16),
        jax.ShapeDtypeStruct((n, ATT_KV_W), BF16),
        jax.ShapeDtypeStruct((n, IDX_Q_W), BF16),
        jax.ShapeDtypeStruct((n, IDX_HEAD_DIM), BF16),
        jax.ShapeDtypeStruct((n, IDX_HEADS), F32),
        jax.ShapeDtypeStruct((n, 3 * DN_W), F32),
        jax.ShapeDtypeStruct((n, DN_W), F32),
        jax.ShapeDtypeStruct((n, 2 * DN_HEADS), F32),
        jax.ShapeDtypeStruct((2 * DN_HEADS, n), F32),
        jax.ShapeDtypeStruct((n, 2 * d), F32),
    )
    out_specs = tuple(
        pl.BlockSpec((2 * DN_HEADS, tm), lambda i: (0, i)) if s.shape[0] != n
        else pl.BlockSpec((tm, s.shape[1]), row)
        for s in out_shapes)
    return pl.pallas_call(
        _in_proj_body,
        grid=(n // tm,),
        in_specs=[
            pl.BlockSpec((tm, d), row),
            pl.BlockSpec((1, d), fixed),
            pl.BlockSpec((d, wtot), fixed),
            pl.BlockSpec((2 * DN_HEADS, d), fixed),
            pl.BlockSpec((1, ATT_HEAD_DIM), fixed),
            pl.BlockSpec((1, ATT_HEAD_DIM), fixed),
        ],
        out_specs=out_specs,
        out_shape=out_shapes,
        compiler_params=pltpu.CompilerParams(dimension_semantics=("parallel",), vmem_limit_bytes=VMEM_LIMIT),
    )(xf, g1.reshape(1, d), w_all, wabt, q_gain.reshape(1, -1), k_gain.reshape(1, -1))


def _dsa_body(q_ref, iq_ref, iw_ref, k_ref, v_ref, ik_ref, o_ref,
              key_ref, m_ref, l_ref, acc_ref, *, tq, tk, ksel):
    i = pl.program_id(1)
    q0 = i * tq
    n_kb = (q0 + tq + tk - 1) // tk
    qpos = q0 + lax.broadcasted_iota(I32, (tq, 1), 0)
    iw = iw_ref[...]
    iq = iq_ref[...]

    def score_body(kb, carry):
        k0 = pl.multiple_of(kb * tk, tk)
        ikb = ik_ref[pl.ds(k0, tk), :]
        acc = jnp.zeros((tq, tk), F32)
        for hd in range(IDX_HEADS):
            sl = slice(hd * IDX_HEAD_DIM, (hd + 1) * IDX_HEAD_DIM)
            dots = lax.dot_general(iq[:, sl], ikb, NT_DIMS, preferred_element_type=F32)
            acc = acc + iw[:, hd:hd + 1] * jnp.maximum(dots, 0.0)
        bits = lax.bitcast_convert_type(acc, I32)
        keys = jnp.where(bits >= 0, bits, bits ^ INT_MAX)
        kpos = k0 + lax.broadcasted_iota(I32, (1, tk), 1)
        key_ref[kb] = jnp.where(kpos <= qpos, keys, NEG_INF_KEY)
        return carry

    lax.fori_loop(0, n_kb, score_body, 0)

    def count_ge(thr):
        def body(kb, c):
            ind = jnp.where(key_ref[kb] >= thr, 1.0, 0.0)
            part = ind[:, 0:LANES]
            for j in range(1, tk // LANES):
                part = part + ind[:, j * LANES:(j + 1) * LANES]
            return c + part
        c = lax.fori_loop(0, n_kb, body, jnp.zeros((tq, LANES), F32))
        return jnp.sum(c, axis=1, keepdims=True)

    def bisect_body(_, st):
        lo, hi, c_lo, c_hi = st
        mid = (lo >> 1) + (hi >> 1) + (lo & hi & 1)
        c = count_ge(mid)
        ge = c >= ksel
        return (jnp.where(ge, mid, lo), jnp.where(ge, hi, mid),
                jnp.where(ge, c, c_lo), jnp.where(ge, c_hi, c))

    lo0 = jnp.full((tq, 1), NEG_INF_KEY + 1, I32)
    hi0 = jnp.full((tq, 1), INT_MAX, I32)
    zero = jnp.zeros((tq, 1), F32)
    thr, _, c_thr, c_above = lax.fori_loop(0, 32, bisect_body, (lo0, hi0, zero, zero))

    need = ksel - c_above
    has_excess = jnp.max(jnp.where(c_thr > ksel, 1.0, 0.0)) > 0.0

    @pl.when(has_excess)
    def _():
        upper = jnp.where(lax.broadcasted_iota(I32, (tk, tk), 0) <= lax.broadcasted_iota(I32, (tk, tk), 1),
                          1.0, 0.0).astype(BF16)

        def body(kb, seen):
            keys = key_ref[kb]
            tie = keys == thr
            tie_f = jnp.where(tie, 1.0, 0.0)
            prefix = jnp.dot(tie_f.astype(BF16), upper, preferred_element_type=F32) + seen
            drop = jnp.logical_and(tie, prefix > need)
            key_ref[kb] = jnp.where(drop, NEG_INF_KEY, keys)
            return seen + jnp.sum(tie_f, axis=1, keepdims=True)

        lax.fori_loop(0, n_kb, body, jnp.zeros((tq, 1), F32))

    m_ref[...] = jnp.full(m_ref.shape, -1e30, F32)
    l_ref[...] = jnp.zeros(l_ref.shape, F32)
    acc_ref[...] = jnp.zeros(acc_ref.shape, F32)
    q = q_ref[...]
    grp = ATT_HEADS // ATT_KV_HEADS

    def att_body(kb, carry):
        k0 = pl.multiple_of(kb * tk, tk)
        kblk = k_ref[pl.ds(k0, tk), :]
        vblk = v_ref[pl.ds(k0, tk), :]
        sel = key_ref[kb] >= thr
        for hd in range(ATT_HEADS):
            g = hd // grp
            qs = slice(hd * ATT_HEAD_DIM, (hd + 1) * ATT_HEAD_DIM)
            ks = slice(g * ATT_HEAD_DIM, (g + 1) * ATT_HEAD_DIM)
            s = lax.dot_general(q[:, qs], kblk[:, ks], NT_DIMS, preferred_element_type=F32)
            s = jnp.where(sel, s, -1e30)
            m_old = m_ref[hd]
            m_new = jnp.maximum(m_old, jnp.max(s, axis=1, keepdims=True))
            p = jnp.exp(s - m_new)
            alpha = jnp.exp(m_old - m_new)
            l_ref[hd] = alpha * l_ref[hd] + jnp.sum(p, axis=1, keepdims=True)
            acc_ref[hd] = alpha * acc_ref[hd] + jnp.dot(p.astype(BF16), vblk[:, ks], preferred_element_type=F32)
            m_ref[hd] = m_new
        return carry

    lax.fori_loop(0, n_kb, att_body, 0)
    for hd in range(ATT_HEADS):
        qs = slice(hd * ATT_HEAD_DIM, (hd + 1) * ATT_HEAD_DIM)
        o_ref[:, qs] = (acc_ref[hd] / l_ref[hd]).astype(o_ref.dtype)


def _dsa(q, k, v, iq, ik, iw, b, t, tq, tk):
    n = b * t
    nq = t // tq
    ksel = min(TOPK_MAX, t // 4)
    qrow = lambda bi, i: (bi * nq + i, 0)
    brow = lambda bi, i: (bi, 0)
    return pl.pallas_call(
        functools.partial(_dsa_body, tq=tq, tk=tk, ksel=ksel),
        grid=(b, nq),
        in_specs=[
            pl.BlockSpec((tq, ATT_Q_W), qrow),
            pl.BlockSpec((tq, IDX_Q_W), qrow),
            pl.BlockSpec((tq, IDX_HEADS), qrow),
            pl.BlockSpec((t, ATT_KV_W), brow),
            pl.BlockSpec((t, ATT_KV_W), brow),
            pl.BlockSpec((t, IDX_HEAD_DIM), brow),
        ],
        out_specs=pl.BlockSpec((tq, ATT_Q_W), qrow),
        out_shape=jax.ShapeDtypeStruct((n, ATT_Q_W), BF16),
        scratch_shapes=[
            pltpu.VMEM((t // tk, tq, tk), I32),
            pltpu.VMEM((ATT_HEADS, tq, 1), F32),
            pltpu.VMEM((ATT_HEADS, tq, 1), F32),
            pltpu.VMEM((ATT_HEADS, tq, ATT_HEAD_DIM), F32),
        ],
        compiler_params=pltpu.CompilerParams(dimension_semantics=("parallel", "arbitrary"),
                                             vmem_limit_bytes=VMEM_LIMIT),
    )(q, iq, iw, k, v, ik)


def _inv_unit_lower(a, eye):
    x = eye - a
    p = a
    for _ in range(5):
        p = _dot_hi(p, p)
        x = x + _dot_hi(x, p)
    return x


def _deltanet_body(x_ref, dz_ref, dab_ref, dabt_ref, cw_ref, alog_r_ref, bias_r_ref, alog_c_ref, bias_c_ref,
                   og_ref, y_ref, carry_ref, state_ref, *, tb):
    c = DN_CHUNK
    d = DN_HEAD_DIM
    nh = DN_HEADS

    @pl.when(pl.program_id(1) == 0)
    def _():
        carry_ref[...] = jnp.zeros(carry_ref.shape, F32)
        state_ref[...] = jnp.zeros(state_ref.shape, F32)

    xb = x_ref[...]
    xx = jnp.concatenate([carry_ref[...], xb], axis=0)
    cw = cw_ref[...]
    off = SUBLANES - (DN_CONV - 1)
    conv = cw[0:1] * xx[off:off + tb]
    for j in range(1, DN_CONV):
        conv = conv + cw[j:j + 1] * xx[off + j:off + j + tb]
    carry_ref[...] = xb[tb - SUBLANES:tb]
    qkv = conv * _sigmoid(conv)

    dab = dab_ref[...]
    g_col = -jnp.exp(alog_r_ref[...]) * _softplus(dab[:, 0:nh] + bias_r_ref[...])
    beta_col = _sigmoid(dab[:, nh:2 * nh])
    g_row = -jnp.exp(alog_c_ref[...]) * _softplus(dabt_ref[0:nh, :] + bias_c_ref[...])

    ri = lax.broadcasted_iota(I32, (tb, tb), 0)
    ci = lax.broadcasted_iota(I32, (tb, tb), 1)
    same_chunk = (ri // c) == (ci // c)
    lower_blk = jnp.where(jnp.logical_and(same_chunk, ri >= ci), 1.0, 0.0).astype(BF16)
    upper_blk = jnp.where(jnp.logical_and(same_chunk, ri <= ci), 1.0, 0.0).astype(BF16)
    gc_col = _dot_exact_lhs(lower_blk, g_col)
    gc_row = _dot_exact_rhs(g_row, upper_blk)

    r64 = lax.broadcasted_iota(I32, (c, c), 0)
    c64 = lax.broadcasted_iota(I32, (c, c), 1)
    tri = r64 >= c64
    strict = r64 > c64
    eye = jnp.where(r64 == c64, 1.0, 0.0)
    og = og_ref[...]

    states = [state_ref[hd] for hd in range(nh)]
    for ch in range(tb // c):
        rs = slice(ch * c, (ch + 1) * c)
        for hd in range(nh):
            ls = slice(hd * d, (hd + 1) * d)
            qc = qkv[rs, hd * d:(hd + 1) * d]
            kc = qkv[rs, nh * d + hd * d:nh * d + (hd + 1) * d]
            vc = qkv[rs, 2 * nh * d + hd * d:2 * nh * d + (hd + 1) * d]
            qc = qc * lax.rsqrt(jnp.sum(qc * qc, axis=-1, keepdims=True) + EPS) * (d ** -0.5)
            kc = kc * lax.rsqrt(jnp.sum(kc * kc, axis=-1, keepdims=True) + EPS)
            bcol = beta_col[rs, hd:hd + 1]
            gcol = gc_col[rs, hd:hd + 1]
            grow = gc_row[hd:hd + 1, rs]
            decay = jnp.exp(jnp.where(tri, gcol - grow, -jnp.inf))
            kbeta = kc * bcol
            a_mat = jnp.where(strict, _dot_nt(kbeta, kc) * decay, 0.0)
            t_mat = _inv_unit_lower(a_mat, eye)
            eg = jnp.exp(gcol)
            rhs = jnp.concatenate([vc * bcol, kbeta * eg], axis=1)
            sol = _dot(t_mat, rhs)
            u, w = sol[:, :d], sol[:, d:]
            intra = jnp.where(tri, _dot_nt(qc, kc) * decay, 0.0)
            st = states[hd]
            both = _dot(jnp.concatenate([w, qc * eg], axis=0), st)
            v_new = u - both[:c]
            o = both[c:] + _dot(intra, v_new)
            g_last = gcol[c - 1:c]
            kdec = kc * jnp.exp(g_last - gcol)
            states[hd] = st * jnp.exp(g_last) + lax.dot_general(
                kdec.astype(BF16), v_new.astype(BF16), TN_DIMS, preferred_element_type=F32)
            on = o * lax.rsqrt(jnp.mean(o * o, axis=-1, keepdims=True) + EPS) * og
            z = dz_ref[rs, ls]
            y_ref[rs, ls] = (on * (z * _sigmoid(z))).astype(y_ref.dtype)
    for hd in range(nh):
        state_ref[hd] = states[hd]


def _deltanet(dqkv, dz, dab, dabt, conv_w, a_log, dt_bias, out_gain, b, t, tb):
    n = b * t
    nb = t // tb
    row = lambda bi, j: (bi * nb + j, 0)
    col = lambda bi, j: (0, bi * nb + j)
    fixed = lambda bi, j: (0, 0)
    nh = DN_HEADS
    return pl.pallas_call(
        functools.partial(_deltanet_body, tb=tb),
        grid=(b, nb),
        in_specs=[
            pl.BlockSpec((tb, 3 * DN_W), row),
            pl.BlockSpec((tb, DN_W), row),
            pl.BlockSpec((tb, 2 * nh), row),
            pl.BlockSpec((2 * nh, tb), col),
            pl.BlockSpec((DN_CONV, 3 * DN_W), fixed),
            pl.BlockSpec((1, nh), fixed),
            pl.BlockSpec((1, nh), fixed),
            pl.BlockSpec((nh, 1), fixed),
            pl.BlockSpec((nh, 1), fixed),
            pl.BlockSpec((1, DN_HEAD_DIM), fixed),
        ],
        out_specs=pl.BlockSpec((tb, DN_W), row),
        out_shape=jax.ShapeDtypeStruct((n, DN_W), BF16),
        scratch_shapes=[
            pltpu.VMEM((SUBLANES, 3 * DN_W), F32),
            pltpu.VMEM((nh, DN_HEAD_DIM, DN_HEAD_DIM), F32),
        ],
        compiler_params=pltpu.CompilerParams(dimension_semantics=("parallel", "arbitrary"),
                                             vmem_limit_bytes=VMEM_LIMIT),
    )(dqkv, dz, dab, dabt, conv_w, a_log.reshape(1, nh), dt_bias.reshape(1, nh),
      a_log.reshape(nh, 1), dt_bias.reshape(nh, 1), out_gain.reshape(1, -1))


def _merge_body(x_ref, ya_ref, yd_ref, gab_ref, wa_ref, wb_ref, wo_ref, o_ref):
    dm = x_ref.shape[1]
    gab = gab_ref[...]
    a = jnp.dot(ya_ref[...], wa_ref[...], preferred_element_type=F32)
    bb = jnp.dot(yd_ref[...], wb_ref[...], preferred_element_type=F32)
    merged = _sigmoid(gab[:, :dm]) * a + _sigmoid(gab[:, dm:]) * bb
    o_ref[...] = x_ref[...] + jnp.dot(merged.astype(BF16), wo_ref[...], preferred_element_type=F32)


def _merge(xf, y_att, y_dn, gab, w_a, w_b, w_o, tm):
    n, d = xf.shape
    row = lambda i: (i, 0)
    fixed = lambda i: (0, 0)
    return pl.pallas_call(
        _merge_body,
        grid=(n // tm,),
        in_specs=[
            pl.BlockSpec((tm, d), row),
            pl.BlockSpec((tm, ATT_Q_W), row),
            pl.BlockSpec((tm, DN_W), row),
            pl.BlockSpec((tm, 2 * d), row),
            pl.BlockSpec((ATT_Q_W, d), fixed),
            pl.BlockSpec((DN_W, d), fixed),
            pl.BlockSpec((d, d), fixed),
        ],
        out_specs=pl.BlockSpec((tm, d), row),
        out_shape=jax.ShapeDtypeStruct((n, d), F32),
        compiler_params=pltpu.CompilerParams(dimension_semantics=("parallel",), vmem_limit_bytes=VMEM_LIMIT),
    )(xf, y_att, y_dn, gab, w_a.astype(BF16), w_b.astype(BF16), w_o.astype(BF16))


def _top16_rows(s, payload):
    rows = lax.broadcasted_iota(I32, s.shape, 0)
    big = s.shape[0]
    vals, pays = [], []
    for _ in range(PEER_TOPK):
        m = jnp.max(s, axis=0, keepdims=True)
        am = jnp.min(jnp.where(s == m, rows, big), axis=0, keepdims=True)
        hit = rows == am
        vals.append(m)
        pays.append(jnp.sum(jnp.where(hit, payload, 0), axis=0, keepdims=True))
        s = jnp.where(hit, -jnp.inf, s)
    return jnp.concatenate(vals, axis=0), jnp.concatenate(pays, axis=0)


def _peer_route_body(x_ref, g2_ref, wq_ref, sk_ref, ids_ref, gates_ref, *, tm):
    x = x_ref[...]
    ms = jnp.mean(x * x, axis=-1, keepdims=True)
    h = (x * lax.rsqrt(ms + EPS) * g2_ref[...]).astype(BF16)
    q = jnp.dot(h, wq_ref[...], preferred_element_type=F32).astype(BF16)
    half = PEER_KEY_DIM // 2
    key_rows = lax.broadcasted_iota(I32, (PEER_N_KEYS, tm), 0)
    for hd in range(PEER_HEADS):
        tops = []
        for p in range(2):
            c0 = hd * PEER_KEY_DIM + p * half
            st = lax.dot_general(sk_ref[2 * hd + p], q[:, c0:c0 + half], NT_DIMS,
                                 preferred_element_type=F32)
            tops.append(_top16_rows(st, key_rows))
        (s0, i0), (s1, i1) = tops
        cand_s = jnp.concatenate([s0[r:r + 1] + s1 for r in range(PEER_TOPK)], axis=0)
        cand_e = jnp.concatenate([i0[r:r + 1] * PEER_N_KEYS + i1 for r in range(PEER_TOPK)], axis=0)
        best, expert = _top16_rows(cand_s, cand_e)
        e = jnp.exp(best - best[0:1])
        gate = e / jnp.sum(e, axis=0, keepdims=True)
        rs = slice(hd * PEER_TOPK, (hd + 1) * PEER_TOPK)
        ids_ref[rs, :] = expert
        gates_ref[rs, :] = gate


def _peer_route(x1, g2, w_query, sub_keys, tm):
    n, d = x1.shape
    nsel = PEER_HEADS * PEER_TOPK
    half = PEER_KEY_DIM // 2
    sk = sub_keys.reshape(PEER_HEADS * 2, PEER_N_KEYS, half).astype(BF16)
    return pl.pallas_call(
        functools.partial(_peer_route_body, tm=tm),
        grid=(n // tm,),
        in_specs=[
            pl.BlockSpec((tm, d), lambda i: (i, 0)),
            pl.BlockSpec((1, d), lambda i: (0, 0)),
            pl.BlockSpec((d, PEER_HEADS * PEER_KEY_DIM), lambda i: (0, 0)),
            pl.BlockSpec((PEER_HEADS * 2, PEER_N_KEYS, half), lambda i: (0, 0, 0)),
        ],
        out_specs=(pl.BlockSpec((nsel, tm), lambda i: (0, i)),
                   pl.BlockSpec((nsel, tm), lambda i: (0, i))),
        out_shape=(jax.ShapeDtypeStruct((nsel, n), I32), jax.ShapeDtypeStruct((nsel, n), F32)),
        compiler_params=pltpu.CompilerParams(dimension_semantics=("parallel",), vmem_limit_bytes=VMEM_LIMIT),
    )(x1, g2.reshape(1, d), w_query.astype(BF16), sk)


PEER_SLOTS = 4


def _peer_apply_body(ids_ref, x_ref, g2_ref, gates_ref, uv_hbm, o_ref, buf, coef_ref, sems, *, tb):
    nsel = PEER_HEADS * PEER_TOPK
    nchunk = x_ref.shape[1]
    dm = nchunk * LANES

    def issue(tok, slot):
        for kk in range(nsel):
            pltpu.make_async_copy(uv_hbm.at[ids_ref[tok, kk]], buf.at[slot, kk], sems.at[slot]).start()

    def wait_all(slot):
        pltpu.make_async_copy(uv_hbm.at[pl.ds(0, nsel)], buf.at[slot], sems.at[slot]).wait()

    eye = lax.broadcasted_iota(I32, (nsel, nsel), 0) == lax.broadcasted_iota(I32, (nsel, nsel), 1)
    sub = lax.broadcasted_iota(I32, (SUBLANES, LANES), 0)
    masks = {k: (sub & k) == 0 for k in (4, 2, 1)}
    g2 = g2_ref[...]

    def merge(x, y, k):
        if k == 4:
            return jnp.where(masks[k], x, y) + pltpu.roll(jnp.where(masks[k], y, x), k, axis=0)
        return jnp.where(masks[k], x + pltpu.roll(x, SUBLANES - k, axis=0), y + pltpu.roll(y, k, axis=0))

    order = (0, 4, 2, 6, 1, 5, 3, 7)

    def compute(t, slot):
        xt = x_ref[t]
        ssq = jnp.sum(jnp.sum(xt * xt, axis=1, keepdims=True), axis=0, keepdims=True)
        h8 = xt * lax.rsqrt(ssq * (1.0 / dm) + EPS) * g2
        grow = gates_ref[t]
        gcol = jnp.sum(jnp.where(eye, jnp.broadcast_to(grow, (nsel, nsel)), 0.0), axis=1, keepdims=True)

        wait_all(slot)
        groups = []
        for grp in range(nsel // SUBLANES):
            p = [buf[slot, grp * SUBLANES + order.index(j), 0:nchunk, :] * h8 for j in range(SUBLANES)]
            q4 = [merge(p[2 * i], p[2 * i + 1], 4) for i in range(4)]
            q2 = [merge(q4[2 * i], q4[2 * i + 1], 2) for i in range(2)]
            groups.append(merge(q2[0], q2[1], 1))
        colsum = jnp.concatenate(groups, axis=0)
        act = jnp.sum(colsum, axis=1, keepdims=True)
        gelu = 0.5 * act * (1.0 + lax.erf(act * (2.0 ** -0.5)))
        coef_ref[...] = jnp.broadcast_to(gcol * gelu, (nsel, LANES))
        acc = jnp.zeros((nchunk, LANES), F32)
        for kk in range(nsel):
            acc = acc + coef_ref[kk:kk + 1, :] * buf[slot, kk, nchunk:2 * nchunk, :]
        o_ref[t] = xt + acc

    ahead = PEER_SLOTS - 1
    for s in range(ahead):
        issue(s, s)

    def group_body(g, carry):
        t0 = g * PEER_SLOTS
        for s in range(PEER_SLOTS):
            issue(t0 + s + ahead, (s + ahead) % PEER_SLOTS)
            compute(t0 + s, s)
        return carry

    lax.fori_loop(0, tb // PEER_SLOTS - 1, group_body, 0)
    t0 = tb - PEER_SLOTS
    issue(t0 + ahead, ahead % PEER_SLOTS)
    for s in range(PEER_SLOTS):
        compute(t0 + s, s)


def _peer_apply(x1, g2, ids, gates, peer_u, peer_v, tb):
    n, d = x1.shape
    nsel = PEER_HEADS * PEER_TOPK
    ne = peer_u.shape[0]
    nchunk = d // LANES
    uv = jnp.concatenate([peer_u.reshape(ne, nchunk, LANES), peer_v.reshape(ne, nchunk, LANES)], axis=1)
    out = pl.pallas_call(
        functools.partial(_peer_apply_body, tb=tb),
        grid=(n // tb,),
        in_specs=[
            pl.BlockSpec((tb, nsel), lambda i: (i, 0), memory_space=pltpu.SMEM),
            pl.BlockSpec((tb, nchunk, LANES), lambda i: (i, 0, 0)),
            pl.BlockSpec((nchunk, LANES), lambda i: (0, 0)),
            pl.BlockSpec((tb, 1, nsel), lambda i: (i, 0, 0)),
            pl.BlockSpec(memory_space=pl.ANY),
        ],
        out_specs=pl.BlockSpec((tb, nchunk, LANES), lambda i: (i, 0, 0)),
        out_shape=jax.ShapeDtypeStruct((n, nchunk, LANES), F32),
        scratch_shapes=[
            pltpu.VMEM((PEER_SLOTS, nsel, 2 * nchunk, LANES), F32),
            pltpu.VMEM((nsel, LANES), F32),
            pltpu.SemaphoreType.DMA((PEER_SLOTS,)),
        ],
        compiler_params=pltpu.CompilerParams(dimension_semantics=("arbitrary",), vmem_limit_bytes=VMEM_LIMIT),
    )(ids, x1.reshape(n, nchunk, LANES), g2.reshape(nchunk, LANES), gates.reshape(n, 1, nsel), uv)
    return out.reshape(n, d)


def _block_sizes(t):
    return dict(
        tm_proj=256,
        tq=min(256, t),
        tk=min(512, t),
        tb_dn=min(256, t),
        tm_merge=512,
        tm_route=LANES,
        tb_peer=128,
    )


def kernel(x, norm1_gain, w_in, q_norm_gain, k_norm_gain, dn_conv_w, dn_a_log, dn_dt_bias, dn_out_norm_gain,
           w_att_branch, w_dn_branch, w_o, norm2_gain, peer_w_query, peer_sub_keys, peer_u, peer_v):
    b, t, d = x.shape
    n = b * t
    bs = _block_sizes(t)
    xf = x.reshape(n, d)
    for layer in range(w_in.shape[0]):
        (q, k, v, iq, ik, iw, dqkv, dz, dab, dabt, gab) = _in_proj(
            xf, norm1_gain[layer], w_in[layer], q_norm_gain[layer], k_norm_gain[layer], bs["tm_proj"])
        y_att = _dsa(q, k, v, iq, ik, iw, b, t, bs["tq"], bs["tk"])
        y_dn = _deltanet(dqkv, dz, dab, dabt, dn_conv_w[layer], dn_a_log[layer], dn_dt_bias[layer],
                         dn_out_norm_gain[layer], b, t, bs["tb_dn"])
        x1 = _merge(xf, y_att, y_dn, gab, w_att_branch[layer], w_dn_branch[layer], w_o[layer], bs["tm_merge"])
        ids_t, gates_t = _peer_route(x1, norm2_gain[layer], peer_w_query[layer], peer_sub_keys[layer],
                                     bs["tm_route"])
        xf = _peer_apply(x1, norm2_gain[layer], ids_t.T, gates_t.T, peer_u[layer], peer_v[layer], bs["tb_peer"])
    return xf.reshape(b, t, d)
```

```python
import functools

import jax
import jax.numpy as jnp
import numpy as np
from jax import lax
from jax.experimental import pallas as pl
from jax.experimental.pallas import tpu as pltpu

F32 = jnp.float32
BF16 = jnp.bfloat16
I32 = jnp.int32

ATT_HEADS = 8
ATT_KV_HEADS = 2
ATT_HEAD_DIM = 64
IDX_HEADS = 8
IDX_HEAD_DIM = 64
TOPK_MAX = 256
DN_HEADS = 4
DN_HEAD_DIM = 128
DN_CONV = 4
DN_CHUNK = 64
PEER_HEADS = 8
PEER_N_KEYS = 128
PEER_KEY_DIM = 256
PEER_TOPK = 16
EPS = 1e-6

ATT_Q_W = ATT_HEADS * ATT_HEAD_DIM
ATT_KV_W = ATT_KV_HEADS * ATT_HEAD_DIM
IDX_Q_W = IDX_HEADS * IDX_HEAD_DIM
DN_W = DN_HEADS * DN_HEAD_DIM

LANES = 128
SUBLANES = 8
VMEM_LIMIT = 56 * 1024 * 1024

NEG_INF_KEY = int(np.int32(np.uint32(0xFF800000) ^ np.uint32(0x7FFFFFFF)))
INT_MAX = int(np.iinfo(np.int32).max)

NT_DIMS = (((1,), (1,)), ((), ()))
TN_DIMS = (((0,), (0,)), ((), ()))


def _sigmoid(x):
    return 1.0 / (1.0 + jnp.exp(-x))


def _softplus(x):
    return jnp.maximum(x, 0.0) + jnp.log(1.0 + jnp.exp(-jnp.abs(x)))


def _dot(a, b):
    return jnp.dot(a.astype(BF16), b.astype(BF16), preferred_element_type=F32)


def _dot_nt(a, b):
    return lax.dot_general(a.astype(BF16), b.astype(BF16), NT_DIMS, preferred_element_type=F32)


def _split2(x):
    hi = x.astype(BF16)
    lo = (x - hi.astype(F32)).astype(BF16)
    return hi, lo


def _dot_hi(a, b):
    a1, a2 = _split2(a)
    b1, b2 = _split2(b)
    out = jnp.dot(a1, b1, preferred_element_type=F32)
    out = out + jnp.dot(a1, b2, preferred_element_type=F32)
    out = out + jnp.dot(a2, b1, preferred_element_type=F32)
    return out


def _dot_exact_rhs(a, b_exact):
    a1 = a.astype(BF16)
    r1 = a - a1.astype(F32)
    a2 = r1.astype(BF16)
    a3 = (r1 - a2.astype(F32)).astype(BF16)
    out = jnp.dot(a1, b_exact, preferred_element_type=F32)
    out = out + jnp.dot(a2, b_exact, preferred_element_type=F32)
    out = out + jnp.dot(a3, b_exact, preferred_element_type=F32)
    return out


def _dot_exact_lhs(a_exact, b):
    b1 = b.astype(BF16)
    r1 = b - b1.astype(F32)
    b2 = r1.astype(BF16)
    b3 = (r1 - b2.astype(F32)).astype(BF16)
    out = jnp.dot(a_exact, b1, preferred_element_type=F32)
    out = out + jnp.dot(a_exact, b2, preferred_element_type=F32)
    out = out + jnp.dot(a_exact, b3, preferred_element_type=F32)
    return out


C_AQ = 0
C_AK = C_AQ + ATT_Q_W
C_AV = C_AK + ATT_KV_W
C_IQ = C_AV + ATT_KV_W
C_SM = C_IQ + IDX_Q_W
C_DQKV = C_SM + LANES
C_DZ = C_DQKV + 3 * DN_W
C_GAB = C_DZ + DN_W
SM_IW = IDX_HEAD_DIM
SM_DAB = SM_IW + IDX_HEADS


def _in_proj_body(x_ref, g1_ref, w_ref, wabt_ref, qg_ref, kg_ref,
                  q_ref, k_ref, v_ref, iq_ref, ik_ref, iw_ref, dqkv_ref, dz_ref, dab_ref, dabt_ref, gab_ref):
    x = x_ref[...]
    ms = jnp.mean(x * x, axis=-1, keepdims=True)
    h = (x * lax.rsqrt(ms + EPS) * g1_ref[...]).astype(BF16)

    def proj(c0, c1):
        return jnp.dot(h, w_ref[:, c0:c1], preferred_element_type=F32)

    def head_norm(blk, gain):
        return blk * lax.rsqrt(jnp.mean(blk * blk, axis=-1, keepdims=True) + EPS) * gain

    aq = proj(C_AQ, C_AK)
    qg = qg_ref[...] * (ATT_HEAD_DIM ** -0.5)
    for hd in range(ATT_HEADS):
        sl = slice(hd * ATT_HEAD_DIM, (hd + 1) * ATT_HEAD_DIM)
        q_ref[:, sl] = head_norm(aq[:, sl], qg).astype(q_ref.dtype)
    ak = proj(C_AK, C_AV)
    kg = kg_ref[...]
    for hd in range(ATT_KV_HEADS):
        sl = slice(hd * ATT_HEAD_DIM, (hd + 1) * ATT_HEAD_DIM)
        k_ref[:, sl] = head_norm(ak[:, sl], kg).astype(k_ref.dtype)
    v_ref[...] = proj(C_AV, C_IQ).astype(v_ref.dtype)
    iq_ref[...] = proj(C_IQ, C_SM).astype(iq_ref.dtype)
    sm = proj(C_SM, C_DQKV)
    ik_ref[...] = sm[:, :IDX_HEAD_DIM].astype(ik_ref.dtype)
    iw_ref[...] = sm[:, SM_IW:SM_IW + IDX_HEADS]
    dab_ref[...] = sm[:, SM_DAB:SM_DAB + 2 * DN_HEADS]
    dabt_ref[...] = lax.dot_general(wabt_ref[...], h, NT_DIMS, preferred_element_type=F32)
    dqkv_ref[...] = proj(C_DQKV, C_DZ)
    dz_ref[...] = proj(C_DZ, C_GAB)
    gab_ref[...] = proj(C_GAB, C_GAB + 2 * x.shape[1])


def _in_proj(xf, g1, w_in, q_gain, k_gain, tm):
    n, d = xf.shape
    cuts = np.cumsum([ATT_Q_W, ATT_KV_W, ATT_KV_W, IDX_Q_W, IDX_HEAD_DIM, IDX_HEADS,
                      DN_W, DN_W, DN_W, DN_W, DN_HEADS, DN_HEADS, d, d])[:-1].tolist()
    aq, ak, av, iq, ik, iw, dq, dk, dv, dz, da, db, ga, gb = jnp.split(w_in, cuts, axis=-1)
    pad = jnp.zeros((d, LANES - IDX_HEAD_DIM - IDX_HEADS - 2 * DN_HEADS), w_in.dtype)
    w_all = jnp.concatenate([aq, ak, av, iq, ik, iw, da, db, pad, dq, dk, dv, dz, ga, gb], axis=-1).astype(BF16)
    wabt = jnp.concatenate([da, db], axis=-1).T.astype(BF16)
    wtot = w_all.shape[1]
    row = lambda i: (i, 0)
    fixed = lambda i: (0, 0)
    out_shapes = (
        jax.ShapeDtypeStruct((n, ATT_Q_W), BF16),
        jax.ShapeDtypeStruct((n, ATT_KV_W), BF16),
        jax.ShapeDtypeStruct((n, ATT_KV_W), BF16),
        jax.ShapeDtypeStruct((n, IDX_Q_W), BF16),
        jax.ShapeDtypeStruct((n, IDX_HEAD_DIM), BF16),
        jax.ShapeDtypeStruct((n, IDX_HEADS), F32),
        jax.ShapeDtypeStruct((n, 3 * DN_W), F32),
        jax.ShapeDtypeStruct((n, DN_W), F32),
        jax.ShapeDtypeStruct((n, 2 * DN_HEADS), F32),
        jax.ShapeDtypeStruct((2 * DN_HEADS, n), F32),
        jax.ShapeDtypeStruct((n, 2 * d), F32),
    )
    out_specs = tuple(
        pl.BlockSpec((2 * DN_HEADS, tm), lambda i: (0, i)) if s.shape[0] != n
        else pl.BlockSpec((tm, s.shape[1]), row)
        for s in out_shapes)
    return pl.pallas_call(
        _in_proj_body,
        grid=(n // tm,),
        in_specs=[
            pl.BlockSpec((tm, d), row),
            pl.BlockSpec((1, d), fixed),
            pl.BlockSpec((d, wtot), fixed),
            pl.BlockSpec((2 * DN_HEADS, d), fixed),
            pl.BlockSpec((1, ATT_HEAD_DIM), fixed),
            pl.BlockSpec((1, ATT_HEAD_DIM), fixed),
        ],
        out_specs=out_specs,
        out_shape=out_shapes,
        compiler_params=pltpu.CompilerParams(dimension_semantics=("parallel",), vmem_limit_bytes=VMEM_LIMIT),
    )(xf, g1.reshape(1, d), w_all, wabt, q_gain.reshape(1, -1), k_gain.reshape(1, -1))


def _dsa_body(q_ref, iq_ref, iw_ref, k_ref, v_ref, ik_ref, o_ref,
              key_ref, m_ref, l_ref, acc_ref, *, tq, tk, ksel):
    i = pl.program_id(1)
    q0 = i * tq
    n_kb = (q0 + tq + tk - 1) // tk
    qpos = q0 + lax.broadcasted_iota(I32, (tq, 1), 0)
    iw = iw_ref[...]
    iq = iq_ref[...]

    def score_body(kb, carry):
        k0 = pl.multiple_of(kb * tk, tk)
        ikb = ik_ref[pl.ds(k0, tk), :]
        acc = jnp.zeros((tq, tk), F32)
        for hd in range(IDX_HEADS):
            sl = slice(hd * IDX_HEAD_DIM, (hd + 1) * IDX_HEAD_DIM)
            dots = lax.dot_general(iq[:, sl], ikb, NT_DIMS, preferred_element_type=F32)
            acc = acc + iw[:, hd:hd + 1] * jnp.maximum(dots, 0.0)
        bits = lax.bitcast_convert_type(acc, I32)
        keys = jnp.where(bits >= 0, bits, bits ^ INT_MAX)
        kpos = k0 + lax.broadcasted_iota(I32, (1, tk), 1)
        key_ref[kb] = jnp.where(kpos <= qpos, keys, NEG_INF_KEY)
        return carry

    lax.fori_loop(0, n_kb, score_body, 0)

    def count_ge(thr):
        def body(kb, c):
            ind = jnp.where(key_ref[kb] >= thr, 1.0, 0.0)
            part = ind[:, 0:LANES]
            for j in range(1, tk // LANES):
                part = part + ind[:, j * LANES:(j + 1) * LANES]
            return c + part
        c = lax.fori_loop(0, n_kb, body, jnp.zeros((tq, LANES), F32))
        return jnp.sum(c, axis=1, keepdims=True)

    def bisect_body(_, st):
        lo, hi, c_lo, c_hi = st
        mid = (lo >> 1) + (hi >> 1) + (lo & hi & 1)
        c = count_ge(mid)
        ge = c >= ksel
        return (jnp.where(ge, mid, lo), jnp.where(ge, hi, mid),
                jnp.where(ge, c, c_lo), jnp.where(ge, c_hi, c))

    lo0 = jnp.full((tq, 1), NEG_INF_KEY + 1, I32)
    hi0 = jnp.full((tq, 1), INT_MAX, I32)
    zero = jnp.zeros((tq, 1), F32)
    thr, _, c_thr, c_above = lax.fori_loop(0, 32, bisect_body, (lo0, hi0, zero, zero))

    need = ksel - c_above
    has_excess = jnp.max(jnp.where(c_thr > ksel, 1.0, 0.0)) > 0.0

    @pl.when(has_excess)
    def _():
        upper = jnp.where(lax.broadcasted_iota(I32, (tk, tk), 0) <= lax.broadcasted_iota(I32, (tk, tk), 1),
                          1.0, 0.0).astype(BF16)

        def body(kb, seen):
            keys = key_ref[kb]
            tie = keys == thr
            tie_f = jnp.where(tie, 1.0, 0.0)
            prefix = jnp.dot(tie_f.astype(BF16), upper, preferred_element_type=F32) + seen
            drop = jnp.logical_and(tie, prefix > need)
            key_ref[kb] = jnp.where(drop, NEG_INF_KEY, keys)
            return seen + jnp.sum(tie_f, axis=1, keepdims=True)

        lax.fori_loop(0, n_kb, body, jnp.zeros((tq, 1), F32))

    m_ref[...] = jnp.full(m_ref.shape, -1e30, F32)
    l_ref[...] = jnp.zeros(l_ref.shape, F32)
    acc_ref[...] = jnp.zeros(acc_ref.shape, F32)
    q = q_ref[...]
    grp = ATT_HEADS // ATT_KV_HEADS

    def att_body(kb, carry):
        k0 = pl.multiple_of(kb * tk, tk)
        kblk = k_ref[pl.ds(k0, tk), :]
        vblk = v_ref[pl.ds(k0, tk), :]
        sel = key_ref[kb] >= thr
        for hd in range(ATT_HEADS):
            g = hd // grp
            qs = slice(hd * ATT_HEAD_DIM, (hd + 1) * ATT_HEAD_DIM)
            ks = slice(g * ATT_HEAD_DIM, (g + 1) * ATT_HEAD_DIM)
            s = lax.dot_general(q[:, qs], kblk[:, ks], NT_DIMS, preferred_element_type=F32)
            s = jnp.where(sel, s, -1e30)
            m_old = m_ref[hd]
            m_new = jnp.maximum(m_old, jnp.max(s, axis=1, keepdims=True))
            p = jnp.exp(s - m_new)
            alpha = jnp.exp(m_old - m_new)
            l_ref[hd] = alpha * l_ref[hd] + jnp.sum(p, axis=1, keepdims=True)
            acc_ref[hd] = alpha * acc_ref[hd] + jnp.dot(p.astype(BF16), vblk[:, ks], preferred_element_type=F32)
            m_ref[hd] = m_new
        return carry

    lax.fori_loop(0, n_kb, att_body, 0)
    for hd in range(ATT_HEADS):
        qs = slice(hd * ATT_HEAD_DIM, (hd + 1) * ATT_HEAD_DIM)
        o_ref[:, qs] = (acc_ref[hd] / l_ref[hd]).astype(o_ref.dtype)


def _dsa(q, k, v, iq, ik, iw, b, t, tq, tk):
    n = b * t
    nq = t // tq
    ksel = min(TOPK_MAX, t // 4)
    qrow = lambda bi, i: (bi * nq + i, 0)
    brow = lambda bi, i: (bi, 0)
    return pl.pallas_call(
        functools.partial(_dsa_body, tq=tq, tk=tk, ksel=ksel),
        grid=(b, nq),
        in_specs=[
            pl.BlockSpec((tq, ATT_Q_W), qrow),
            pl.BlockSpec((tq, IDX_Q_W), qrow),
            pl.BlockSpec((tq, IDX_HEADS), qrow),
            pl.BlockSpec((t, ATT_KV_W), brow),
            pl.BlockSpec((t, ATT_KV_W), brow),
            pl.BlockSpec((t, IDX_HEAD_DIM), brow),
        ],
        out_specs=pl.BlockSpec((tq, ATT_Q_W), qrow),
        out_shape=jax.ShapeDtypeStruct((n, ATT_Q_W), BF16),
        scratch_shapes=[
            pltpu.VMEM((t // tk, tq, tk), I32),
            pltpu.VMEM((ATT_HEADS, tq, 1), F32),
            pltpu.VMEM((ATT_HEADS, tq, 1), F32),
            pltpu.VMEM((ATT_HEADS, tq, ATT_HEAD_DIM), F32),
        ],
        compiler_params=pltpu.CompilerParams(dimension_semantics=("parallel", "arbitrary"),
                                             vmem_limit_bytes=VMEM_LIMIT),
    )(q, iq, iw, k, v, ik)


def _inv_unit_lower(a, eye):
    x = eye - a
    p = a
    for _ in range(5):
        p = _dot_hi(p, p)
        x = x + _dot_hi(x, p)
    return x


def _deltanet_body(x_ref, dz_ref, dab_ref, dabt_ref, cw_ref, alog_r_ref, bias_r_ref, alog_c_ref, bias_c_ref,
                   og_ref, y_ref, carry_ref, state_ref, *, tb):
    c = DN_CHUNK
    d = DN_HEAD_DIM
    nh = DN_HEADS

    @pl.when(pl.program_id(1) == 0)
    def _():
        carry_ref[...] = jnp.zeros(carry_ref.shape, F32)
        state_ref[...] = jnp.zeros(state_ref.shape, F32)

    xb = x_ref[...]
    xx = jnp.concatenate([carry_ref[...], xb], axis=0)
    cw = cw_ref[...]
    off = SUBLANES - (DN_CONV - 1)
    conv = cw[0:1] * xx[off:off + tb]
    for j in range(1, DN_CONV):
        conv = conv + cw[j:j + 1] * xx[off + j:off + j + tb]
    carry_ref[...] = xb[tb - SUBLANES:tb]
    qkv = conv * _sigmoid(conv)

    dab = dab_ref[...]
    g_col = -jnp.exp(alog_r_ref[...]) * _softplus(dab[:, 0:nh] + bias_r_ref[...])
    beta_col = _sigmoid(dab[:, nh:2 * nh])
    g_row = -jnp.exp(alog_c_ref[...]) * _softplus(dabt_ref[0:nh, :] + bias_c_ref[...])

    ri = lax.broadcasted_iota(I32, (tb, tb), 0)
    ci = lax.broadcasted_iota(I32, (tb, tb), 1)
    same_chunk = (ri // c) == (ci // c)
    lower_blk = jnp.where(jnp.logical_and(same_chunk, ri >= ci), 1.0, 0.0).astype(BF16)
    upper_blk = jnp.where(jnp.logical_and(same_chunk, ri <= ci), 1.0, 0.0).astype(BF16)
    gc_col = _dot_exact_lhs(lower_blk, g_col)
    gc_row = _dot_exact_rhs(g_row, upper_blk)

    r64 = lax.broadcasted_iota(I32, (c, c), 0)
    c64 = lax.broadcasted_iota(I32, (c, c), 1)
    tri = r64 >= c64
    strict = r64 > c64
    eye = jnp.where(r64 == c64, 1.0, 0.0)
    og = og_ref[...]

    states = [state_ref[hd] for hd in range(nh)]
    for ch in range(tb // c):
        rs = slice(ch * c, (ch + 1) * c)
        for hd in range(nh):
            ls = slice(hd * d, (hd + 1) * d)
            qc = qkv[rs, hd * d:(hd + 1) * d]
            kc = qkv[rs, nh * d + hd * d:nh * d + (hd + 1) * d]
            vc = qkv[rs, 2 * nh * d + hd * d:2 * nh * d + (hd + 1) * d]
            qc = qc * lax.rsqrt(jnp.sum(qc * qc, axis=-1, keepdims=True) + EPS) * (d ** -0.5)
            kc = kc * lax.rsqrt(jnp.sum(kc * kc, axis=-1, keepdims=True) + EPS)
            bcol = beta_col[rs, hd:hd + 1]
            gcol = gc_col[rs, hd:hd + 1]
            grow = gc_row[hd:hd + 1, rs]
            decay = jnp.exp(jnp.where(tri, gcol - grow, -jnp.inf))
            kbeta = kc * bcol
            a_mat = jnp.where(strict, _dot_nt(kbeta, kc) * decay, 0.0)
            t_mat = _inv_unit_lower(a_mat, eye)
            eg = jnp.exp(gcol)
            rhs = jnp.concatenate([vc * bcol, kbeta * eg], axis=1)
            sol = _dot(t_mat, rhs)
            u, w = sol[:, :d], sol[:, d:]
            intra = jnp.where(tri, _dot_nt(qc, kc) * decay, 0.0)
            st = states[hd]
            both = _dot(jnp.concatenate([w, qc * eg], axis=0), st)
            v_new = u - both[:c]
            o = both[c:] + _dot(intra, v_new)
            g_last = gcol[c - 1:c]
            kdec = kc * jnp.exp(g_last - gcol)
            states[hd] = st * jnp.exp(g_last) + lax.dot_general(
                kdec.astype(BF16), v_new.astype(BF16), TN_DIMS, preferred_element_type=F32)
            on = o * lax.rsqrt(jnp.mean(o * o, axis=-1, keepdims=True) + EPS) * og
            z = dz_ref[rs, ls]
            y_ref[rs, ls] = (on * (z * _sigmoid(z))).astype(y_ref.dtype)
    for hd in range(nh):
        state_ref[hd] = states[hd]


def _deltanet(dqkv, dz, dab, dabt, conv_w, a_log, dt_bias, out_gain, b, t, tb):
    n = b * t
    nb = t // tb
    row = lambda bi, j: (bi * nb + j, 0)
    col = lambda bi, j: (0, bi * nb + j)
    fixed = lambda bi, j: (0, 0)
    nh = DN_HEADS
    return pl.pallas_call(
        functools.partial(_deltanet_body, tb=tb),
        grid=(b, nb),
        in_specs=[
            pl.BlockSpec((tb, 3 * DN_W), row),
            pl.BlockSpec((tb, DN_W), row),
            pl.BlockSpec((tb, 2 * nh), row),
            pl.BlockSpec((2 * nh, tb), col),
            pl.BlockSpec((DN_CONV, 3 * DN_W), fixed),
            pl.BlockSpec((1, nh), fixed),
            pl.BlockSpec((1, nh), fixed),
            pl.BlockSpec((nh, 1), fixed),
            pl.BlockSpec((nh, 1), fixed),
            pl.BlockSpec((1, DN_HEAD_DIM), fixed),
        ],
        out_specs=pl.BlockSpec((tb, DN_W), row),
        out_shape=jax.ShapeDtypeStruct((n, DN_W), BF16),
        scratch_shapes=[
            pltpu.VMEM((SUBLANES, 3 * DN_W), F32),
            pltpu.VMEM((nh, DN_HEAD_DIM, DN_HEAD_DIM), F32),
        ],
        compiler_params=pltpu.CompilerParams(dimension_semantics=("parallel", "arbitrary"),
                                             vmem_limit_bytes=VMEM_LIMIT),
    )(dqkv, dz, dab, dabt, conv_w, a_log.reshape(1, nh), dt_bias.reshape(1, nh),
      a_log.reshape(nh, 1), dt_bias.reshape(nh, 1), out_gain.reshape(1, -1))


def _merge_body(x_ref, ya_ref, yd_ref, gab_ref, wa_ref, wb_ref, wo_ref, o_ref):
    dm = x_ref.shape[1]
    gab = gab_ref[...]
    a = jnp.dot(ya_ref[...], wa_ref[...], preferred_element_type=F32)
    bb = jnp.dot(yd_ref[...], wb_ref[...], preferred_element_type=F32)
    merged = _sigmoid(gab[:, :dm]) * a + _sigmoid(gab[:, dm:]) * bb
    o_ref[...] = x_ref[...] + jnp.dot(merged.astype(BF16), wo_ref[...], preferred_element_type=F32)


def _merge(xf, y_att, y_dn, gab, w_a, w_b, w_o, tm):
    n, d = xf.shape
    row = lambda i: (i, 0)
    fixed = lambda i: (0, 0)
    return pl.pallas_call(
        _merge_body,
        grid=(n // tm,),
        in_specs=[
            pl.BlockSpec((tm, d), row),
            pl.BlockSpec((tm, ATT_Q_W), row),
            pl.BlockSpec((tm, DN_W), row),
            pl.BlockSpec((tm, 2 * d), row),
            pl.BlockSpec((ATT_Q_W, d), fixed),
            pl.BlockSpec((DN_W, d), fixed),
            pl.BlockSpec((d, d), fixed),
        ],
        out_specs=pl.BlockSpec((tm, d), row),
        out_shape=jax.ShapeDtypeStruct((n, d), F32),
        compiler_params=pltpu.CompilerParams(dimension_semantics=("parallel",), vmem_limit_bytes=VMEM_LIMIT),
    )(xf, y_att, y_dn, gab, w_a.astype(BF16), w_b.astype(BF16), w_o.astype(BF16))


def _top16_rows(s, payload):
    rows = lax.broadcasted_iota(I32, s.shape, 0)
    big = s.shape[0]
    vals, pays = [], []
    for _ in range(PEER_TOPK):
        m = jnp.max(s, axis=0, keepdims=True)
        am = jnp.min(jnp.where(s == m, rows, big), axis=0, keepdims=True)
        hit = rows == am
        vals.append(m)
        pays.append(jnp.sum(jnp.where(hit, payload, 0), axis=0, keepdims=True))
        s = jnp.where(hit, -jnp.inf, s)
    return jnp.concatenate(vals, axis=0), jnp.concatenate(pays, axis=0)


def _peer_route_body(x_ref, g2_ref, wq_ref, sk_ref, ids_ref, gates_ref, *, tm):
    x = x_ref[...]
    ms = jnp.mean(x * x, axis=-1, keepdims=True)
    h = (x * lax.rsqrt(ms + EPS) * g2_ref[...]).astype(BF16)
    q = jnp.dot(h, wq_ref[...], preferred_element_type=F32).astype(BF16)
    half = PEER_KEY_DIM // 2
    key_rows = lax.broadcasted_iota(I32, (PEER_N_KEYS, tm), 0)
    for hd in range(PEER_HEADS):
        tops = []
        for p in range(2):
            c0 = hd * PEER_KEY_DIM + p * half
            st = lax.dot_general(sk_ref[2 * hd + p], q[:, c0:c0 + half], NT_DIMS,
                                 preferred_element_type=F32)
            tops.append(_top16_rows(st, key_rows))
        (s0, i0), (s1, i1) = tops
        cand_s = jnp.concatenate([s0[r:r + 1] + s1 for r in range(PEER_TOPK)], axis=0)
        cand_e = jnp.concatenate([i0[r:r + 1] * PEER_N_KEYS + i1 for r in range(PEER_TOPK)], axis=0)
        best, expert = _top16_rows(cand_s, cand_e)
        e = jnp.exp(best - best[0:1])
        gate = e / jnp.sum(e, axis=0, keepdims=True)
        rs = slice(hd * PEER_TOPK, (hd + 1) * PEER_TOPK)
        ids_ref[rs, :] = expert
        gates_ref[rs, :] = gate


def _peer_route(x1, g2, w_query, sub_keys, tm):
    n, d = x1.shape
    nsel = PEER_HEADS * PEER_TOPK
    half = PEER_KEY_DIM // 2
    sk = sub_keys.reshape(PEER_HEADS * 2, PEER_N_KEYS, half).astype(BF16)
    return pl.pallas_call(
        functools.partial(_peer_route_body, tm=tm),
        grid=(n // tm,),
        in_specs=[
            pl.BlockSpec((tm, d), lambda i: (i, 0)),
            pl.BlockSpec((1, d), lambda i: (0, 0)),
            pl.BlockSpec((d, PEER_HEADS * PEER_KEY_DIM), lambda i: (0, 0)),
            pl.BlockSpec((PEER_HEADS * 2, PEER_N_KEYS, half), lambda i: (0, 0, 0)),
        ],
        out_specs=(pl.BlockSpec((nsel, tm), lambda i: (0, i)),
                   pl.BlockSpec((nsel, tm), lambda i: (0, i))),
        out_shape=(jax.ShapeDtypeStruct((nsel, n), I32), jax.ShapeDtypeStruct((nsel, n), F32)),
        compiler_params=pltpu.CompilerParams(dimension_semantics=("parallel",), vmem_limit_bytes=VMEM_LIMIT),
    )(x1, g2.reshape(1, d), w_query.astype(BF16), sk)


PEER_SLOTS = 4


def _peer_apply_body(ids_ref, x_ref, g2_ref, gates_ref, uv_hbm, o_ref, buf, coef_ref, sems, *, tb):
    nsel = PEER_HEADS * PEER_TOPK
    nchunk = x_ref.shape[1]
    dm = nchunk * LANES

    ngrp = nsel // SUBLANES

    def issue(tok, slot, k0, k1):
        for kk in range(k0, k1):
            pltpu.make_async_copy(uv_hbm.at[ids_ref[tok, kk]], buf.at[slot, kk], sems.at[slot]).start(
                priority=kk % 2)

    def wait_all(slot):
        pltpu.make_async_copy(uv_hbm.at[pl.ds(0, nsel)], buf.at[slot], sems.at[slot]).wait()

    eye = lax.broadcasted_iota(I32, (nsel, nsel), 0) == lax.broadcasted_iota(I32, (nsel, nsel), 1)
    sub = lax.broadcasted_iota(I32, (SUBLANES, LANES), 0)
    masks = {k: (sub & k) == 0 for k in (4, 2, 1)}
    g2 = g2_ref[...]

    def merge(x, y, k):
        if k == 4:
            return jnp.where(masks[k], x, y) + pltpu.roll(jnp.where(masks[k], y, x), k, axis=0)
        return jnp.where(masks[k], x + pltpu.roll(x, SUBLANES - k, axis=0), y + pltpu.roll(y, k, axis=0))

    order = (0, 4, 2, 6, 1, 5, 3, 7)

    def u_of(word):
        return lax.bitcast_convert_type(word << 16, F32)

    def v_of(word):
        return lax.bitcast_convert_type(word & jnp.uint32(0xFFFF0000), F32)

    def step(t, slot, nxt, nxt_slot):
        def prefetch(k0, k1):
            if nxt is not None:
                issue(nxt, nxt_slot, k0, k1)

        xt = x_ref[t]
        ssq = jnp.sum(jnp.sum(xt * xt, axis=1, keepdims=True), axis=0, keepdims=True)
        h8 = xt * lax.rsqrt(ssq * (1.0 / dm) + EPS) * g2
        grow = gates_ref[t]
        gcol = jnp.sum(jnp.where(eye, jnp.broadcast_to(grow, (nsel, nsel)), 0.0), axis=1, keepdims=True)

        wait_all(slot)
        per_grp = nsel // (2 * ngrp)
        groups = []
        for grp in range(ngrp):
            prefetch(grp * per_grp, (grp + 1) * per_grp)
            p = [u_of(buf[slot, grp * SUBLANES + order.index(j)]) * h8 for j in range(SUBLANES)]
            q4 = [merge(p[2 * i], p[2 * i + 1], 4) for i in range(4)]
            q2 = [merge(q4[2 * i], q4[2 * i + 1], 2) for i in range(2)]
            groups.append(merge(q2[0], q2[1], 1))
        colsum = jnp.concatenate(groups, axis=0)
        act = jnp.sum(colsum, axis=1, keepdims=True)
        gelu = 0.5 * act * (1.0 + lax.erf(act * (2.0 ** -0.5)))
        coef_ref[...] = jnp.broadcast_to(gcol * gelu, (nsel, LANES))
        acc = jnp.zeros((nchunk, LANES), F32)
        for kk in range(nsel):
            if kk % 2 == 0:
                prefetch(nsel // 2 + kk // 2, nsel // 2 + kk // 2 + 1)
            acc = acc + coef_ref[kk:kk + 1, :] * v_of(buf[slot, kk])
        o_ref[t] = xt + acc

    ahead = PEER_SLOTS - 1
    for s in range(ahead):
        issue(s, s, 0, nsel)

    def group_body(g, carry):
        t0 = g * PEER_SLOTS
        for s in range(PEER_SLOTS):
            step(t0 + s, s, t0 + s + ahead, (s + ahead) % PEER_SLOTS)
        return carry

    lax.fori_loop(0, tb // PEER_SLOTS - 1, group_body, 0)
    t0 = tb - PEER_SLOTS
    step(t0, 0, t0 + ahead, ahead % PEER_SLOTS)
    for s in range(1, PEER_SLOTS):
        step(t0 + s, s, None, None)


def _peer_apply(x1, g2, ids, gates, peer_u, peer_v, tb):
    n, d = x1.shape
    nsel = PEER_HEADS * PEER_TOPK
    ne = peer_u.shape[0]
    nchunk = d // LANES

    def half_words(tab):
        return lax.bitcast_convert_type(tab.astype(BF16), jnp.uint16).astype(jnp.uint32)

    uv = ((half_words(peer_v) << 16) | half_words(peer_u)).reshape(ne, nchunk, LANES)
    out = pl.pallas_call(
        functools.partial(_peer_apply_body, tb=tb),
        grid=(n // tb,),
        in_specs=[
            pl.BlockSpec((tb, nsel), lambda i: (i, 0), memory_space=pltpu.SMEM),
            pl.BlockSpec((tb, nchunk, LANES), lambda i: (i, 0, 0)),
            pl.BlockSpec((nchunk, LANES), lambda i: (0, 0)),
            pl.BlockSpec((tb, 1, nsel), lambda i: (i, 0, 0)),
            pl.BlockSpec(memory_space=pl.ANY),
        ],
        out_specs=pl.BlockSpec((tb, nchunk, LANES), lambda i: (i, 0, 0)),
        out_shape=jax.ShapeDtypeStruct((n, nchunk, LANES), F32),
        scratch_shapes=[
            pltpu.VMEM((PEER_SLOTS, nsel, nchunk, LANES), jnp.uint32),
            pltpu.VMEM((nsel, LANES), F32),
            pltpu.SemaphoreType.DMA((PEER_SLOTS,)),
        ],
        compiler_params=pltpu.CompilerParams(dimension_semantics=("arbitrary",), vmem_limit_bytes=VMEM_LIMIT),
    )(ids, x1.reshape(n, nchunk, LANES), g2.reshape(nchunk, LANES), gates.reshape(n, 1, nsel), uv)
    return out.reshape(n, d)


def _block_sizes(t):
    return dict(
        tm_proj=256,
        tq=min(256, t),
        tk=min(512, t),
        tb_dn=min(256, t),
        tm_merge=512,
        tm_route=LANES,
        tb_peer=128,
    )


def kernel(x, norm1_gain, w_in, q_norm_gain, k_norm_gain, dn_conv_w, dn_a_log, dn_dt_bias, dn_out_norm_gain,
           w_att_branch, w_dn_branch, w_o, norm2_gain, peer_w_query, peer_sub_keys, peer_u, peer_v):
    b, t, d = x.shape
    n = b * t
    bs = _block_sizes(t)
    xf = x.reshape(n, d)
    for layer in range(w_in.shape[0]):
        (q, k, v, iq, ik, iw, dqkv, dz, dab, dabt, gab) = _in_proj(
            xf, norm1_gain[layer], w_in[layer], q_norm_gain[layer], k_norm_gain[layer], bs["tm_proj"])
        y_att = _dsa(q, k, v, iq, ik, iw, b, t, bs["tq"], bs["tk"])
        y_dn = _deltanet(dqkv, dz, dab, dabt, dn_conv_w[layer], dn_a_log[layer], dn_dt_bias[layer],
                         dn_out_norm_gain[layer], b, t, bs["tb_dn"])
        x1 = _merge(xf, y_att, y_dn, gab, w_att_branch[layer], w_dn_branch[layer], w_o[layer], bs["tm_merge"])
        ids_t, gates_t = _peer_route(x1, norm2_gain[layer], peer_w_query[layer], peer_sub_keys[layer],
                                     bs["tm_route"])
        xf = _peer_apply(x1, norm2_gain[layer], ids_t.T, gates_t.T, peer_u[layer], peer_v[layer], bs["tb_peer"])
    return xf.reshape(b, t, d)
```

```python
import functools

import jax
import jax.numpy as jnp
import numpy as np
from jax import lax
from jax.experimental import pallas as pl
from jax.experimental.pallas import tpu as pltpu

F32 = jnp.float32
BF16 = jnp.bfloat16
I32 = jnp.int32

ATT_HEADS = 8
ATT_KV_HEADS = 2
ATT_HEAD_DIM = 64
IDX_HEADS = 8
IDX_HEAD_DIM = 64
TOPK_MAX = 256
DN_HEADS = 4
DN_HEAD_DIM = 128
DN_CONV = 4
DN_CHUNK = 64
PEER_HEADS = 8
PEER_N_KEYS = 128
PEER_KEY_DIM = 256
PEER_TOPK = 16
EPS = 1e-6

ATT_Q_W = ATT_HEADS * ATT_HEAD_DIM
ATT_KV_W = ATT_KV_HEADS * ATT_HEAD_DIM
IDX_Q_W = IDX_HEADS * IDX_HEAD_DIM
DN_W = DN_HEADS * DN_HEAD_DIM

LANES = 128
SUBLANES = 8
VMEM_LIMIT = 56 * 1024 * 1024

NEG_INF_KEY = int(np.int32(np.uint32(0xFF800000) ^ np.uint32(0x7FFFFFFF)))
INT_MAX = int(np.iinfo(np.int32).max)

NT_DIMS = (((1,), (1,)), ((), ()))
TN_DIMS = (((0,), (0,)), ((), ()))


def _sigmoid(x):
    return 1.0 / (1.0 + jnp.exp(-x))


def _softplus(x):
    return jnp.maximum(x, 0.0) + jnp.log(1.0 + jnp.exp(-jnp.abs(x)))


def _dot(a, b):
    return jnp.dot(a.astype(BF16), b.astype(BF16), preferred_element_type=F32)


def _dot_nt(a, b):
    return lax.dot_general(a.astype(BF16), b.astype(BF16), NT_DIMS, preferred_element_type=F32)


def _split2(x):
    hi = x.astype(BF16)
    lo = (x - hi.astype(F32)).astype(BF16)
    return hi, lo


def _dot_hi(a, b):
    a1, a2 = _split2(a)
    b1, b2 = _split2(b)
    out = jnp.dot(a1, b1, preferred_element_type=F32)
    out = out + jnp.dot(a1, b2, preferred_element_type=F32)
    out = out + jnp.dot(a2, b1, preferred_element_type=F32)
    return out


def _dot_exact_rhs(a, b_exact):
    a1 = a.astype(BF16)
    r1 = a - a1.astype(F32)
    a2 = r1.astype(BF16)
    a3 = (r1 - a2.astype(F32)).astype(BF16)
    out = jnp.dot(a1, b_exact, preferred_element_type=F32)
    out = out + jnp.dot(a2, b_exact, preferred_element_type=F32)
    out = out + jnp.dot(a3, b_exact, preferred_element_type=F32)
    return out


def _dot_exact_lhs(a_exact, b):
    b1 = b.astype(BF16)
    r1 = b - b1.astype(F32)
    b2 = r1.astype(BF16)
    b3 = (r1 - b2.astype(F32)).astype(BF16)
    out = jnp.dot(a_exact, b1, preferred_element_type=F32)
    out = out + jnp.dot(a_exact, b2, preferred_element_type=F32)
    out = out + jnp.dot(a_exact, b3, preferred_element_type=F32)
    return out


C_AQ = 0
C_AK = C_AQ + ATT_Q_W
C_AV = C_AK + ATT_KV_W
C_IQ = C_AV + ATT_KV_W
C_SM = C_IQ + IDX_Q_W
C_DQKV = C_SM + LANES
C_DZ = C_DQKV + 3 * DN_W
C_GAB = C_DZ + DN_W
SM_IW = IDX_HEAD_DIM
SM_DAB = SM_IW + IDX_HEADS


def _in_proj_body(x_ref, g1_ref, w_ref, wabt_ref, qg_ref, kg_ref,
                  q_ref, k_ref, v_ref, iq_ref, ik_ref, iw_ref, dqkv_ref, dz_ref, dab_ref, dabt_ref, gab_ref):
    x = x_ref[...]
    ms = jnp.mean(x * x, axis=-1, keepdims=True)
    h = (x * lax.rsqrt(ms + EPS) * g1_ref[...]).astype(BF16)

    def proj(c0, c1):
        return jnp.dot(h, w_ref[:, c0:c1], preferred_element_type=F32)

    def head_norm(blk, gain):
        return blk * lax.rsqrt(jnp.mean(blk * blk, axis=-1, keepdims=True) + EPS) * gain

    aq = proj(C_AQ, C_AK)
    qg = qg_ref[...] * (ATT_HEAD_DIM ** -0.5)
    for hd in range(ATT_HEADS):
        sl = slice(hd * ATT_HEAD_DIM, (hd + 1) * ATT_HEAD_DIM)
        q_ref[:, sl] = head_norm(aq[:, sl], qg).astype(q_ref.dtype)
    ak = proj(C_AK, C_AV)
    kg = kg_ref[...]
    for hd in range(ATT_KV_HEADS):
        sl = slice(hd * ATT_HEAD_DIM, (hd + 1) * ATT_HEAD_DIM)
        k_ref[:, sl] = head_norm(ak[:, sl], kg).astype(k_ref.dtype)
    v_ref[...] = proj(C_AV, C_IQ).astype(v_ref.dtype)
    iq_ref[...] = proj(C_IQ, C_SM).astype(iq_ref.dtype)
    sm = proj(C_SM, C_DQKV)
    ik_ref[...] = sm[:, :IDX_HEAD_DIM].astype(ik_ref.dtype)
    iw_ref[...] = sm[:, SM_IW:SM_IW + IDX_HEADS]
    dab_ref[...] = sm[:, SM_DAB:SM_DAB + 2 * DN_HEADS]
    dabt_ref[...] = lax.dot_general(wabt_ref[...], h, NT_DIMS, preferred_element_type=F32)
    dqkv_ref[...] = proj(C_DQKV, C_DZ)
    dz_ref[...] = proj(C_DZ, C_GAB)
    gab_ref[...] = proj(C_GAB, C_GAB + 2 * x.shape[1])


def _in_proj(xf, g1, w_in, q_gain, k_gain, tm):
    n, d = xf.shape
    cuts = np.cumsum([ATT_Q_W, ATT_KV_W, ATT_KV_W, IDX_Q_W, IDX_HEAD_DIM, IDX_HEADS,
                      DN_W, DN_W, DN_W, DN_W, DN_HEADS, DN_HEADS, d, d])[:-1].tolist()
    aq, ak, av, iq, ik, iw, dq, dk, dv, dz, da, db, ga, gb = jnp.split(w_in, cuts, axis=-1)
    pad = jnp.zeros((d, LANES - IDX_HEAD_DIM - IDX_HEADS - 2 * DN_HEADS), w_in.dtype)
    w_all = jnp.concatenate([aq, ak, av, iq, ik, iw, da, db, pad, dq, dk, dv, dz, ga, gb], axis=-1).astype(BF16)
    wabt = jnp.concatenate([da, db], axis=-1).T.astype(BF16)
    wtot = w_all.shape[1]
    row = lambda i: (i, 0)
    fixed = lambda i: (0, 0)
    out_shapes = (
        jax.ShapeDtypeStruct((n, ATT_Q_W), BF16),
        jax.ShapeDtypeStruct((n, ATT_KV_W), BF16),
        jax.ShapeDtypeStruct((n, ATT_KV_W), BF16),
        jax.ShapeDtypeStruct((n, IDX_Q_W), BF16),
        jax.ShapeDtypeStruct((n, IDX_HEAD_DIM), BF16),
        jax.ShapeDtypeStruct((n, IDX_HEADS), F32),
        jax.ShapeDtypeStruct((n, 3 * DN_W), F32),
        jax.ShapeDtypeStruct((n, DN_W), F32),
        jax.ShapeDtypeStruct((n, 2 * DN_HEADS), F32),
        jax.ShapeDtypeStruct((2 * DN_HEADS, n), F32),
        jax.ShapeDtypeStruct((n, 2 * d), F32),
    )
    out_specs = tuple(
        pl.BlockSpec((2 * DN_HEADS, tm), lambda i: (0, i)) if s.shape[0] != n
        else pl.BlockSpec((tm, s.shape[1]), row)
        for s in out_shapes)
    return pl.pallas_call(
        _in_proj_body,
        grid=(n // tm,),
        in_specs=[
            pl.BlockSpec((tm, d), row),
            pl.BlockSpec((1, d), fixed),
            pl.BlockSpec((d, wtot), fixed),
            pl.BlockSpec((2 * DN_HEADS, d), fixed),
            pl.BlockSpec((1, ATT_HEAD_DIM), fixed),
            pl.BlockSpec((1, ATT_HEAD_DIM), fixed),
        ],
        out_specs=out_specs,
        out_shape=out_shapes,
        compiler_params=pltpu.CompilerParams(dimension_semantics=("parallel",), vmem_limit_bytes=VMEM_LIMIT),
    )(xf, g1.reshape(1, d), w_all, wabt, q_gain.reshape(1, -1), k_gain.reshape(1, -1))


def _dsa_body(q_ref, iq_ref, iw_ref, k_ref, v_ref, ik_ref, o_ref,
              key_ref, m_ref, l_ref, acc_ref, *, tq, tk, ksel):
    i = pl.program_id(1)
    q0 = i * tq
    n_kb = (q0 + tq + tk - 1) // tk
    qpos = q0 + lax.broadcasted_iota(I32, (tq, 1), 0)
    iw = iw_ref[...]
    iq = iq_ref[...]

    def score_body(kb, carry):
        k0 = pl.multiple_of(kb * tk, tk)
        ikb = ik_ref[pl.ds(k0, tk), :]
        acc = jnp.zeros((tq, tk), F32)
        for hd in range(IDX_HEADS):
            sl = slice(hd * IDX_HEAD_DIM, (hd + 1) * IDX_HEAD_DIM)
            dots = lax.dot_general(iq[:, sl], ikb, NT_DIMS, preferred_element_type=F32)
            acc = acc + iw[:, hd:hd + 1] * jnp.maximum(dots, 0.0)
        bits = lax.bitcast_convert_type(acc, I32)
        keys = jnp.where(bits >= 0, bits, bits ^ INT_MAX)
        kpos = k0 + lax.broadcasted_iota(I32, (1, tk), 1)
        key_ref[kb] = jnp.where(kpos <= qpos, keys, NEG_INF_KEY)
        return carry

    lax.fori_loop(0, n_kb, score_body, 0)

    def count_ge(thr):
        def body(kb, c):
            ind = jnp.where(key_ref[kb] >= thr, 1.0, 0.0)
            part = ind[:, 0:LANES]
            for j in range(1, tk // LANES):
                part = part + ind[:, j * LANES:(j + 1) * LANES]
            return c + part
        c = lax.fori_loop(0, n_kb, body, jnp.zeros((tq, LANES), F32))
        return jnp.sum(c, axis=1, keepdims=True)

    def bisect_body(_, st):
        lo, hi, c_lo, c_hi = st
        mid = (lo >> 1) + (hi >> 1) + (lo & hi & 1)
        c = count_ge(mid)
        ge = c >= ksel
        return (jnp.where(ge, mid, lo), jnp.where(ge, hi, mid),
                jnp.where(ge, c, c_lo), jnp.where(ge, c_hi, c))

    lo0 = jnp.full((tq, 1), NEG_INF_KEY + 1, I32)
    hi0 = jnp.full((tq, 1), INT_MAX, I32)
    zero = jnp.zeros((tq, 1), F32)
    thr, _, c_thr, c_above = lax.fori_loop(0, 32, bisect_body, (lo0, hi0, zero, zero))

    need = ksel - c_above
    has_excess = jnp.max(jnp.where(c_thr > ksel, 1.0, 0.0)) > 0.0

    @pl.when(has_excess)
    def _():
        upper = jnp.where(lax.broadcasted_iota(I32, (tk, tk), 0) <= lax.broadcasted_iota(I32, (tk, tk), 1),
                          1.0, 0.0).astype(BF16)

        def body(kb, seen):
            keys = key_ref[kb]
            tie = keys == thr
            tie_f = jnp.where(tie, 1.0, 0.0)
            prefix = jnp.dot(tie_f.astype(BF16), upper, preferred_element_type=F32) + seen
            drop = jnp.logical_and(tie, prefix > need)
            key_ref[kb] = jnp.where(drop, NEG_INF_KEY, keys)
            return seen + jnp.sum(tie_f, axis=1, keepdims=True)

        lax.fori_loop(0, n_kb, body, jnp.zeros((tq, 1), F32))

    m_ref[...] = jnp.full(m_ref.shape, -1e30, F32)
    l_ref[...] = jnp.zeros(l_ref.shape, F32)
    acc_ref[...] = jnp.zeros(acc_ref.shape, F32)
    q = q_ref[...]
    grp = ATT_HEADS // ATT_KV_HEADS

    def att_body(kb, carry):
        k0 = pl.multiple_of(kb * tk, tk)
        kblk = k_ref[pl.ds(k0, tk), :]
        vblk = v_ref[pl.ds(k0, tk), :]
        sel = key_ref[kb] >= thr
        for hd in range(ATT_HEADS):
            g = hd // grp
            qs = slice(hd * ATT_HEAD_DIM, (hd + 1) * ATT_HEAD_DIM)
            ks = slice(g * ATT_HEAD_DIM, (g + 1) * ATT_HEAD_DIM)
            s = lax.dot_general(q[:, qs], kblk[:, ks], NT_DIMS, preferred_element_type=F32)
            s = jnp.where(sel, s, -1e30)
            m_old = m_ref[hd]
            m_new = jnp.maximum(m_old, jnp.max(s, axis=1, keepdims=True))
            p = jnp.exp(s - m_new)
            alpha = jnp.exp(m_old - m_new)
            l_ref[hd] = alpha * l_ref[hd] + jnp.sum(p, axis=1, keepdims=True)
            acc_ref[hd] = alpha * acc_ref[hd] + jnp.dot(p.astype(BF16), vblk[:, ks], preferred_element_type=F32)
            m_ref[hd] = m_new
        return carry

    lax.fori_loop(0, n_kb, att_body, 0)
    for hd in range(ATT_HEADS):
        qs = slice(hd * ATT_HEAD_DIM, (hd + 1) * ATT_HEAD_DIM)
        o_ref[:, qs] = (acc_ref[hd] / l_ref[hd]).astype(o_ref.dtype)


def _dsa(q, k, v, iq, ik, iw, b, t, tq, tk):
    n = b * t
    nq = t // tq
    ksel = min(TOPK_MAX, t // 4)
    qrow = lambda bi, i: (bi * nq + i, 0)
    brow = lambda bi, i: (bi, 0)
    return pl.pallas_call(
        functools.partial(_dsa_body, tq=tq, tk=tk, ksel=ksel),
        grid=(b, nq),
        in_specs=[
            pl.BlockSpec((tq, ATT_Q_W), qrow),
            pl.BlockSpec((tq, IDX_Q_W), qrow),
            pl.BlockSpec((tq, IDX_HEADS), qrow),
            pl.BlockSpec((t, ATT_KV_W), brow),
            pl.BlockSpec((t, ATT_KV_W), brow),
            pl.BlockSpec((t, IDX_HEAD_DIM), brow),
        ],
        out_specs=pl.BlockSpec((tq, ATT_Q_W), qrow),
        out_shape=jax.ShapeDtypeStruct((n, ATT_Q_W), BF16),
        scratch_shapes=[
            pltpu.VMEM((t // tk, tq, tk), I32),
            pltpu.VMEM((ATT_HEADS, tq, 1), F32),
            pltpu.VMEM((ATT_HEADS, tq, 1), F32),
            pltpu.VMEM((ATT_HEADS, tq, ATT_HEAD_DIM), F32),
        ],
        compiler_params=pltpu.CompilerParams(dimension_semantics=("parallel", "arbitrary"),
                                             vmem_limit_bytes=VMEM_LIMIT),
    )(q, iq, iw, k, v, ik)


def _deltanet_body(x_ref, dz_ref, dab_ref, dabt_ref, cw_ref, alog_r_ref, bias_r_ref, alog_c_ref, bias_c_ref,
                   og_ref, y_ref, carry_ref, state_ref, *, tb):
    c = DN_CHUNK
    d = DN_HEAD_DIM
    nh = DN_HEADS

    @pl.when(pl.program_id(1) == 0)
    def _():
        carry_ref[...] = jnp.zeros(carry_ref.shape, F32)
        state_ref[...] = jnp.zeros(state_ref.shape, F32)

    xb = x_ref[...]
    xx = jnp.concatenate([carry_ref[...], xb], axis=0)
    cw = cw_ref[...]
    off = SUBLANES - (DN_CONV - 1)
    conv = cw[0:1] * xx[off:off + tb]
    for j in range(1, DN_CONV):
        conv = conv + cw[j:j + 1] * xx[off + j:off + j + tb]
    carry_ref[...] = xb[tb - SUBLANES:tb]
    qkv = conv * _sigmoid(conv)

    dab = dab_ref[...]
    g_col = -jnp.exp(alog_r_ref[...]) * _softplus(dab[:, 0:nh] + bias_r_ref[...])
    beta_col = _sigmoid(dab[:, nh:2 * nh])
    g_row = -jnp.exp(alog_c_ref[...]) * _softplus(dabt_ref[0:nh, :] + bias_c_ref[...])

    ri = lax.broadcasted_iota(I32, (tb, tb), 0)
    ci = lax.broadcasted_iota(I32, (tb, tb), 1)
    same_chunk = (ri // c) == (ci // c)
    lower_blk = jnp.where(jnp.logical_and(same_chunk, ri >= ci), 1.0, 0.0).astype(BF16)
    upper_blk = jnp.where(jnp.logical_and(same_chunk, ri <= ci), 1.0, 0.0).astype(BF16)
    gc_col = _dot_exact_lhs(lower_blk, g_col)
    gc_row = _dot_exact_rhs(g_row, upper_blk)

    r64 = lax.broadcasted_iota(I32, (c, c), 0)
    c64 = lax.broadcasted_iota(I32, (c, c), 1)
    tri = r64 >= c64
    strict = r64 > c64
    eye = jnp.where(r64 == c64, 1.0, 0.0)
    og = og_ref[...]

    inst = [(ch, hd) for ch in range(tb // c) for hd in range(nh)]
    rows = lambda ch: slice(ch * c, (ch + 1) * c)
    qs, ks, vs, bcols, gcols, decays, kbetas = [], [], [], [], [], [], []
    for ch, hd in inst:
        rs = rows(ch)
        qc = qkv[rs, hd * d:(hd + 1) * d]
        kc = qkv[rs, nh * d + hd * d:nh * d + (hd + 1) * d]
        qs.append(qc * lax.rsqrt(jnp.sum(qc * qc, axis=-1, keepdims=True) + EPS) * (d ** -0.5))
        ks.append(kc * lax.rsqrt(jnp.sum(kc * kc, axis=-1, keepdims=True) + EPS))
        vs.append(qkv[rs, 2 * nh * d + hd * d:2 * nh * d + (hd + 1) * d])
        bcols.append(beta_col[rs, hd:hd + 1])
        gcols.append(gc_col[rs, hd:hd + 1])
        decays.append(jnp.exp(jnp.where(tri, gcols[-1] - gc_row[hd:hd + 1, rs], -jnp.inf)))
        kbetas.append(ks[-1] * bcols[-1])
    kk = [_dot_nt(kbetas[i], ks[i]) for i in range(len(inst))]
    qk = [_dot_nt(qs[i], ks[i]) for i in range(len(inst))]
    a_mats = [jnp.where(strict, kk[i] * decays[i], 0.0) for i in range(len(inst))]
    intras = [jnp.where(tri, qk[i] * decays[i], 0.0) for i in range(len(inst))]
    xs = [eye - a for a in a_mats]
    ps = a_mats
    for _ in range(5):
        ps = [_dot_hi(p, p) for p in ps]
        xs = [x + _dot_hi(x, p) for x, p in zip(xs, ps)]
    egs = [jnp.exp(g) for g in gcols]
    sols = [_dot(xs[i], jnp.concatenate([vs[i] * bcols[i], kbetas[i] * egs[i]], axis=1)) for i in range(len(inst))]
    qes = [qs[i] * egs[i] for i in range(len(inst))]
    glasts = [g[c - 1:c] for g in gcols]
    kdecs = [ks[i] * jnp.exp(glasts[i] - gcols[i]) for i in range(len(inst))]

    states = [state_ref[hd] for hd in range(nh)]
    for ch in range(tb // c):
        ii = [ch * nh + hd for hd in range(nh)]
        both = [_dot(jnp.concatenate([sols[i][:, d:], qes[i]], axis=0), states[hd]) for hd, i in enumerate(ii)]
        v_new = [sols[i][:, :d] - both[hd][:c] for hd, i in enumerate(ii)]
        outs = [both[hd][c:] + _dot(intras[i], v_new[hd]) for hd, i in enumerate(ii)]
        states = [states[hd] * jnp.exp(glasts[i]) + lax.dot_general(
            kdecs[i].astype(BF16), v_new[hd].astype(BF16), TN_DIMS, preferred_element_type=F32)
            for hd, i in enumerate(ii)]
        for hd in range(nh):
            ls = slice(hd * d, (hd + 1) * d)
            o = outs[hd]
            on = o * lax.rsqrt(jnp.mean(o * o, axis=-1, keepdims=True) + EPS) * og
            z = dz_ref[rows(ch), ls]
            y_ref[rows(ch), ls] = (on * (z * _sigmoid(z))).astype(y_ref.dtype)
    for hd in range(nh):
        state_ref[hd] = states[hd]


def _deltanet(dqkv, dz, dab, dabt, conv_w, a_log, dt_bias, out_gain, b, t, tb):
    n = b * t
    nb = t // tb
    row = lambda bi, j: (bi * nb + j, 0)
    col = lambda bi, j: (0, bi * nb + j)
    fixed = lambda bi, j: (0, 0)
    nh = DN_HEADS
    return pl.pallas_call(
        functools.partial(_deltanet_body, tb=tb),
        grid=(b, nb),
        in_specs=[
            pl.BlockSpec((tb, 3 * DN_W), row),
            pl.BlockSpec((tb, DN_W), row),
            pl.BlockSpec((tb, 2 * nh), row),
            pl.BlockSpec((2 * nh, tb), col),
            pl.BlockSpec((DN_CONV, 3 * DN_W), fixed),
            pl.BlockSpec((1, nh), fixed),
            pl.BlockSpec((1, nh), fixed),
            pl.BlockSpec((nh, 1), fixed),
            pl.BlockSpec((nh, 1), fixed),
            pl.BlockSpec((1, DN_HEAD_DIM), fixed),
        ],
        out_specs=pl.BlockSpec((tb, DN_W), row),
        out_shape=jax.ShapeDtypeStruct((n, DN_W), BF16),
        scratch_shapes=[
            pltpu.VMEM((SUBLANES, 3 * DN_W), F32),
            pltpu.VMEM((nh, DN_HEAD_DIM, DN_HEAD_DIM), F32),
        ],
        compiler_params=pltpu.CompilerParams(dimension_semantics=("parallel", "arbitrary"),
                                             vmem_limit_bytes=VMEM_LIMIT),
    )(dqkv, dz, dab, dabt, conv_w, a_log.reshape(1, nh), dt_bias.reshape(1, nh),
      a_log.reshape(nh, 1), dt_bias.reshape(nh, 1), out_gain.reshape(1, -1))


def _merge_body(x_ref, ya_ref, yd_ref, gab_ref, wa_ref, wb_ref, wo_ref, o_ref):
    dm = x_ref.shape[1]
    gab = gab_ref[...]
    a = jnp.dot(ya_ref[...], wa_ref[...], preferred_element_type=F32)
    bb = jnp.dot(yd_ref[...], wb_ref[...], preferred_element_type=F32)
    merged = _sigmoid(gab[:, :dm]) * a + _sigmoid(gab[:, dm:]) * bb
    o_ref[...] = x_ref[...] + jnp.dot(merged.astype(BF16), wo_ref[...], preferred_element_type=F32)


def _merge(xf, y_att, y_dn, gab, w_a, w_b, w_o, tm):
    n, d = xf.shape
    row = lambda i: (i, 0)
    fixed = lambda i: (0, 0)
    return pl.pallas_call(
        _merge_body,
        grid=(n // tm,),
        in_specs=[
            pl.BlockSpec((tm, d), row),
            pl.BlockSpec((tm, ATT_Q_W), row),
            pl.BlockSpec((tm, DN_W), row),
            pl.BlockSpec((tm, 2 * d), row),
            pl.BlockSpec((ATT_Q_W, d), fixed),
            pl.BlockSpec((DN_W, d), fixed),
            pl.BlockSpec((d, d), fixed),
        ],
        out_specs=pl.BlockSpec((tm, d), row),
        out_shape=jax.ShapeDtypeStruct((n, d), F32),
        compiler_params=pltpu.CompilerParams(dimension_semantics=("parallel",), vmem_limit_bytes=VMEM_LIMIT),
    )(xf, y_att, y_dn, gab, w_a.astype(BF16), w_b.astype(BF16), w_o.astype(BF16))


def _top16_rows(s, payload=None):
    rows = lax.broadcasted_iota(I32, s.shape, 0)
    big = s.shape[0]
    vals, pays = [], []
    for _ in range(PEER_TOPK):
        m = jnp.max(s, axis=0, keepdims=True)
        am = jnp.min(jnp.where(s == m, rows, big), axis=0, keepdims=True)
        hit = rows == am
        vals.append(m)
        if payload is None:
            pays.append(am)
        else:
            pays.append(jnp.sum(jnp.where(hit, payload, 0), axis=0, keepdims=True))
        s = jnp.where(hit, -jnp.inf, s)
    return jnp.concatenate(vals, axis=0), jnp.concatenate(pays, axis=0)


def _pair_candidates(s0, i0, s1, i1):
    k = PEER_TOPK
    rows8 = lax.broadcasted_iota(I32, (SUBLANES, s0.shape[1]), 0)
    cs = [s0[0:1] + s1]
    ce = [i0[0:1] * PEER_N_KEYS + i1]
    for i in range(1, SUBLANES):
        valid = rows8 < k // (i + 1)
        cs.append(jnp.where(valid, s0[i:i + 1] + s1[0:SUBLANES], -jnp.inf))
        ce.append(i0[i:i + 1] * PEER_N_KEYS + i1[0:SUBLANES])
    cs.append(s0[SUBLANES:k] + s1[0:1])
    ce.append(i0[SUBLANES:k] * PEER_N_KEYS + i1[0:1])
    return jnp.concatenate(cs, axis=0), jnp.concatenate(ce, axis=0)


def _peer_route_body(x_ref, g2_ref, wq_ref, sk_ref, ids_ref, gates_ref, *, tm):
    x = x_ref[...]
    ms = jnp.mean(x * x, axis=-1, keepdims=True)
    h = (x * lax.rsqrt(ms + EPS) * g2_ref[...]).astype(BF16)
    q = jnp.dot(h, wq_ref[...], preferred_element_type=F32).astype(BF16)
    half = PEER_KEY_DIM // 2
    for hd in range(PEER_HEADS):
        tops = []
        for p in range(2):
            c0 = hd * PEER_KEY_DIM + p * half
            st = lax.dot_general(sk_ref[2 * hd + p], q[:, c0:c0 + half], NT_DIMS,
                                 preferred_element_type=F32)
            tops.append(_top16_rows(st))
        (s0, i0), (s1, i1) = tops
        cand_s, cand_e = _pair_candidates(s0, i0, s1, i1)
        best, expert = _top16_rows(cand_s, cand_e)
        e = jnp.exp(best - best[0:1])
        gate = e / jnp.sum(e, axis=0, keepdims=True)
        rs = slice(hd * PEER_TOPK, (hd + 1) * PEER_TOPK)
        ids_ref[rs, :] = expert
        gates_ref[rs, :] = gate


def _peer_route(x1, g2, w_query, sub_keys, tm):
    n, d = x1.shape
    nsel = PEER_HEADS * PEER_TOPK
    half = PEER_KEY_DIM // 2
    sk = sub_keys.reshape(PEER_HEADS * 2, PEER_N_KEYS, half).astype(BF16)
    return pl.pallas_call(
        functools.partial(_peer_route_body, tm=tm),
        grid=(n // tm,),
        in_specs=[
            pl.BlockSpec((tm, d), lambda i: (i, 0)),
            pl.BlockSpec((1, d), lambda i: (0, 0)),
            pl.BlockSpec((d, PEER_HEADS * PEER_KEY_DIM), lambda i: (0, 0)),
            pl.BlockSpec((PEER_HEADS * 2, PEER_N_KEYS, half), lambda i: (0, 0, 0)),
        ],
        out_specs=(pl.BlockSpec((nsel, tm), lambda i: (0, i)),
                   pl.BlockSpec((nsel, tm), lambda i: (0, i))),
        out_shape=(jax.ShapeDtypeStruct((nsel, n), I32), jax.ShapeDtypeStruct((nsel, n), F32)),
        compiler_params=pltpu.CompilerParams(dimension_semantics=("parallel",), vmem_limit_bytes=VMEM_LIMIT),
    )(x1, g2.reshape(1, d), w_query.astype(BF16), sk)


PEER_SLOTS = 4


def _peer_apply_body(ids_ref, x_ref, g2_ref, gates_ref, uv_hbm, o_ref, buf, coef_ref, sems, *, tb):
    nsel = PEER_HEADS * PEER_TOPK
    nchunk = x_ref.shape[1]
    dm = nchunk * LANES

    ngrp = nsel // SUBLANES

    def issue(tok, slot, k0, k1):
        for kk in range(k0, k1):
            pltpu.make_async_copy(uv_hbm.at[ids_ref[tok, kk]], buf.at[slot, kk], sems.at[slot]).start(
                priority=kk % 2)

    def wait_all(slot):
        pltpu.make_async_copy(uv_hbm.at[pl.ds(0, nsel)], buf.at[slot], sems.at[slot]).wait()

    eye = lax.broadcasted_iota(I32, (nsel, nsel), 0) == lax.broadcasted_iota(I32, (nsel, nsel), 1)
    sub = lax.broadcasted_iota(I32, (SUBLANES, LANES), 0)
    masks = {k: (sub & k) == 0 for k in (4, 2, 1)}
    g2 = g2_ref[...]

    def merge(x, y, k):
        if k == 4:
            return jnp.where(masks[k], x, y) + pltpu.roll(jnp.where(masks[k], y, x), k, axis=0)
        return jnp.where(masks[k], x + pltpu.roll(x, SUBLANES - k, axis=0), y + pltpu.roll(y, k, axis=0))

    order = (0, 4, 2, 6, 1, 5, 3, 7)

    def u_of(word):
        return lax.bitcast_convert_type(word << 16, F32)

    def v_of(word):
        return lax.bitcast_convert_type(word & jnp.uint32(0xFFFF0000), F32)

    def step(t, slot, nxt, nxt_slot):
        def prefetch(k0, k1):
            if nxt is not None:
                issue(nxt, nxt_slot, k0, k1)

        xt = x_ref[t]
        ssq = jnp.sum(jnp.sum(xt * xt, axis=1, keepdims=True), axis=0, keepdims=True)
        h8 = xt * lax.rsqrt(ssq * (1.0 / dm) + EPS) * g2
        grow = gates_ref[t]
        gcol = jnp.sum(jnp.where(eye, jnp.broadcast_to(grow, (nsel, nsel)), 0.0), axis=1, keepdims=True)

        wait_all(slot)
        per_grp = nsel // (2 * ngrp)
        groups = []
        for grp in range(ngrp):
            prefetch(grp * per_grp, (grp + 1) * per_grp)
            p = [u_of(buf[slot, grp * SUBLANES + order.index(j)]) * h8 for j in range(SUBLANES)]
            q4 = [merge(p[2 * i], p[2 * i + 1], 4) for i in range(4)]
            q2 = [merge(q4[2 * i], q4[2 * i + 1], 2) for i in range(2)]
            groups.append(merge(q2[0], q2[1], 1))
        colsum = jnp.concatenate(groups, axis=0)
        act = jnp.sum(colsum, axis=1, keepdims=True)
        gelu = 0.5 * act * (1.0 + lax.erf(act * (2.0 ** -0.5)))
        coef_ref[...] = jnp.broadcast_to(gcol * gelu, (nsel, LANES))
        acc = jnp.zeros((nchunk, LANES), F32)
        for kk in range(nsel):
            if kk % 2 == 0:
                prefetch(nsel // 2 + kk // 2, nsel // 2 + kk // 2 + 1)
            acc = acc + coef_ref[kk:kk + 1, :] * v_of(buf[slot, kk])
        o_ref[t] = xt + acc

    ahead = PEER_SLOTS - 1
    for s in range(ahead):
        issue(s, s, 0, nsel)

    def group_body(g, carry):
        t0 = g * PEER_SLOTS
        for s in range(PEER_SLOTS):
            step(t0 + s, s, t0 + s + ahead, (s + ahead) % PEER_SLOTS)
        return carry

    lax.fori_loop(0, tb // PEER_SLOTS - 1, group_body, 0)
    t0 = tb - PEER_SLOTS
    step(t0, 0, t0 + ahead, ahead % PEER_SLOTS)
    for s in range(1, PEER_SLOTS):
        step(t0 + s, s, None, None)


def _peer_apply(x1, g2, ids, gates, peer_u, peer_v, tb):
    n, d = x1.shape
    nsel = PEER_HEADS * PEER_TOPK
    ne = peer_u.shape[0]
    nchunk = d // LANES

    def half_words(tab):
        return lax.bitcast_convert_type(tab.astype(BF16), jnp.uint16).astype(jnp.uint32)

    uv = ((half_words(peer_v) << 16) | half_words(peer_u)).reshape(ne, nchunk, LANES)
    out = pl.pallas_call(
        functools.partial(_peer_apply_body, tb=tb),
        grid=(n // tb,),
        in_specs=[
            pl.BlockSpec((tb, nsel), lambda i: (i, 0), memory_space=pltpu.SMEM),
            pl.BlockSpec((tb, nchunk, LANES), lambda i: (i, 0, 0)),
            pl.BlockSpec((nchunk, LANES), lambda i: (0, 0)),
            pl.BlockSpec((tb, 1, nsel), lambda i: (i, 0, 0)),
            pl.BlockSpec(memory_space=pl.ANY),
        ],
        out_specs=pl.BlockSpec((tb, nchunk, LANES), lambda i: (i, 0, 0)),
        out_shape=jax.ShapeDtypeStruct((n, nchunk, LANES), F32),
        scratch_shapes=[
            pltpu.VMEM((PEER_SLOTS, nsel, nchunk, LANES), jnp.uint32),
            pltpu.VMEM((nsel, LANES), F32),
            pltpu.SemaphoreType.DMA((PEER_SLOTS,)),
        ],
        compiler_params=pltpu.CompilerParams(dimension_semantics=("arbitrary",), vmem_limit_bytes=VMEM_LIMIT),
    )(ids, x1.reshape(n, nchunk, LANES), g2.reshape(nchunk, LANES), gates.reshape(n, 1, nsel), uv)
    return out.reshape(n, d)


def _block_sizes(t):
    return dict(
        tm_proj=256,
        tq=min(256, t),
        tk=min(512, t),
        tb_dn=min(256, t),
        tm_merge=512,
        tm_route=LANES,
        tb_peer=128,
    )


def kernel(x, norm1_gain, w_in, q_norm_gain, k_norm_gain, dn_conv_w, dn_a_log, dn_dt_bias, dn_out_norm_gain,
           w_att_branch, w_dn_branch, w_o, norm2_gain, peer_w_query, peer_sub_keys, peer_u, peer_v):
    b, t, d = x.shape
    n = b * t
    bs = _block_sizes(t)
    xf = x.reshape(n, d)
    for layer in range(w_in.shape[0]):
        (q, k, v, iq, ik, iw, dqkv, dz, dab, dabt, gab) = _in_proj(
            xf, norm1_gain[layer], w_in[layer], q_norm_gain[layer], k_norm_gain[layer], bs["tm_proj"])
        y_att = _dsa(q, k, v, iq, ik, iw, b, t, bs["tq"], bs["tk"])
        y_dn = _deltanet(dqkv, dz, dab, dabt, dn_conv_w[layer], dn_a_log[layer], dn_dt_bias[layer],
                         dn_out_norm_gain[layer], b, t, bs["tb_dn"])
        x1 = _merge(xf, y_att, y_dn, gab, w_att_branch[layer], w_dn_branch[layer], w_o[layer], bs["tm_merge"])
        ids_t, gates_t = _peer_route(x1, norm2_gain[layer], peer_w_query[layer], peer_sub_keys[layer],
                                     bs["tm_route"])
        xf = _peer_apply(x1, norm2_gain[layer], ids_t.T, gates_t.T, peer_u[layer], peer_v[layer], bs["tb_peer"])
    return xf.reshape(b, t, d)
```

```python
import functools

import jax
import jax.numpy as jnp
import numpy as np
from jax import lax
from jax.experimental import pallas as pl
from jax.experimental.pallas import tpu as pltpu

F32 = jnp.float32
BF16 = jnp.bfloat16
I32 = jnp.int32

ATT_HEADS = 8
ATT_KV_HEADS = 2
ATT_HEAD_DIM = 64
IDX_HEADS = 8
IDX_HEAD_DIM = 64
TOPK_MAX = 256
DN_HEADS = 4
DN_HEAD_DIM = 128
DN_CONV = 4
DN_CHUNK = 64
PEER_HEADS = 8
PEER_N_KEYS = 128
PEER_KEY_DIM = 256
PEER_TOPK = 16
EPS = 1e-6

ATT_Q_W = ATT_HEADS * ATT_HEAD_DIM
ATT_KV_W = ATT_KV_HEADS * ATT_HEAD_DIM
IDX_Q_W = IDX_HEADS * IDX_HEAD_DIM
DN_W = DN_HEADS * DN_HEAD_DIM

LANES = 128
SUBLANES = 8
VMEM_LIMIT = 56 * 1024 * 1024

NEG_INF_KEY = int(np.int32(np.uint32(0xFF800000) ^ np.uint32(0x7FFFFFFF)))
INT_MAX = int(np.iinfo(np.int32).max)

NT_DIMS = (((1,), (1,)), ((), ()))
TN_DIMS = (((0,), (0,)), ((), ()))


def _sigmoid(x):
    return 1.0 / (1.0 + jnp.exp(-x))


def _softplus(x):
    return jnp.maximum(x, 0.0) + jnp.log(1.0 + jnp.exp(-jnp.abs(x)))


def _dot(a, b):
    return jnp.dot(a.astype(BF16), b.astype(BF16), preferred_element_type=F32)


def _dot_nt(a, b):
    return lax.dot_general(a.astype(BF16), b.astype(BF16), NT_DIMS, preferred_element_type=F32)


def _split2(x):
    hi = x.astype(BF16)
    lo = (x - hi.astype(F32)).astype(BF16)
    return hi, lo


def _dot_hi(a, b):
    a1, a2 = _split2(a)
    b1, b2 = _split2(b)
    out = jnp.dot(a1, b1, preferred_element_type=F32)
    out = out + jnp.dot(a1, b2, preferred_element_type=F32)
    out = out + jnp.dot(a2, b1, preferred_element_type=F32)
    return out


def _dot_exact_rhs(a, b_exact):
    a1 = a.astype(BF16)
    r1 = a - a1.astype(F32)
    a2 = r1.astype(BF16)
    a3 = (r1 - a2.astype(F32)).astype(BF16)
    out = jnp.dot(a1, b_exact, preferred_element_type=F32)
    out = out + jnp.dot(a2, b_exact, preferred_element_type=F32)
    out = out + jnp.dot(a3, b_exact, preferred_element_type=F32)
    return out


def _dot_exact_lhs(a_exact, b):
    b1 = b.astype(BF16)
    r1 = b - b1.astype(F32)
    b2 = r1.astype(BF16)
    b3 = (r1 - b2.astype(F32)).astype(BF16)
    out = jnp.dot(a_exact, b1, preferred_element_type=F32)
    out = out + jnp.dot(a_exact, b2, preferred_element_type=F32)
    out = out + jnp.dot(a_exact, b3, preferred_element_type=F32)
    return out


C_AK = 0
C_SM = C_AK + ATT_KV_W
C_DQKV = C_SM + LANES
C_DZ = C_DQKV + 3 * DN_W
C_GAB = C_DZ + DN_W
SM_DAB = IDX_HEAD_DIM
R_AQ = 0
R_IQ = R_AQ + ATT_Q_W
R_AV = R_IQ + IDX_Q_W
R_IW = R_AV + ATT_KV_W
R_DAB = R_IW + IDX_HEADS
R_END = R_DAB + 2 * DN_HEADS


def _in_proj_body(x_ref, g1_ref, w_ref, wt_ref, qg_ref, kg_ref,
                  qt_ref, iqt_ref, vt_ref, iwt_ref, dabt_ref, k_ref, ik_ref, dab_ref, dqkv_ref, dz_ref, gab_ref):
    x = x_ref[...]
    ms = jnp.mean(x * x, axis=-1, keepdims=True)
    h = (x * lax.rsqrt(ms + EPS) * g1_ref[...]).astype(BF16)

    rt = lax.dot_general(wt_ref[...], h, NT_DIMS, preferred_element_type=F32)
    qg = qg_ref[...] * (ATT_HEAD_DIM ** -0.5)
    for hd in range(ATT_HEADS):
        sl = slice(R_AQ + hd * ATT_HEAD_DIM, R_AQ + (hd + 1) * ATT_HEAD_DIM)
        blk = rt[sl]
        qt_ref[sl, :] = (blk * lax.rsqrt(jnp.mean(blk * blk, axis=0, keepdims=True) + EPS) * qg).astype(qt_ref.dtype)
    iqt_ref[...] = rt[R_IQ:R_AV].astype(iqt_ref.dtype)
    vt_ref[...] = rt[R_AV:R_IW].astype(vt_ref.dtype)
    iwt_ref[...] = rt[R_IW:R_DAB]
    dabt_ref[...] = rt[R_DAB:R_END]

    def proj(c0, c1):
        return jnp.dot(h, w_ref[:, c0:c1], preferred_element_type=F32)

    ak = proj(C_AK, C_SM)
    kg = kg_ref[...]
    for hd in range(ATT_KV_HEADS):
        sl = slice(hd * ATT_HEAD_DIM, (hd + 1) * ATT_HEAD_DIM)
        blk = ak[:, sl]
        k_ref[:, sl] = (blk * lax.rsqrt(jnp.mean(blk * blk, axis=-1, keepdims=True) + EPS) * kg).astype(k_ref.dtype)
    sm = proj(C_SM, C_DQKV)
    ik_ref[...] = sm[:, :IDX_HEAD_DIM].astype(ik_ref.dtype)
    dab_ref[...] = sm[:, SM_DAB:SM_DAB + 2 * DN_HEADS]
    dqkv_ref[...] = proj(C_DQKV, C_DZ)
    dz_ref[...] = proj(C_DZ, C_GAB)
    gab_ref[...] = proj(C_GAB, C_GAB + 2 * x.shape[1])


def _in_proj(xf, g1, w_in, q_gain, k_gain, tm):
    n, d = xf.shape
    cuts = np.cumsum([ATT_Q_W, ATT_KV_W, ATT_KV_W, IDX_Q_W, IDX_HEAD_DIM, IDX_HEADS,
                      DN_W, DN_W, DN_W, DN_W, DN_HEADS, DN_HEADS, d, d])[:-1].tolist()
    aq, ak, av, iq, ik, iw, dq, dk, dv, dz, da, db, ga, gb = jnp.split(w_in, cuts, axis=-1)
    pad = jnp.zeros((d, LANES - IDX_HEAD_DIM - 2 * DN_HEADS), w_in.dtype)
    w_all = jnp.concatenate([ak, ik, da, db, pad, dq, dk, dv, dz, ga, gb], axis=-1).astype(BF16)
    wt_all = jnp.concatenate([aq, iq, av, iw, da, db], axis=-1).T.astype(BF16)
    wtot = w_all.shape[1]
    row = lambda i: (i, 0)
    col = lambda i: (0, i)
    fixed = lambda i: (0, 0)
    out_shapes = (
        jax.ShapeDtypeStruct((ATT_Q_W, n), BF16),
        jax.ShapeDtypeStruct((IDX_Q_W, n), BF16),
        jax.ShapeDtypeStruct((ATT_KV_W, n), BF16),
        jax.ShapeDtypeStruct((IDX_HEADS, n), F32),
        jax.ShapeDtypeStruct((2 * DN_HEADS, n), F32),
        jax.ShapeDtypeStruct((n, ATT_KV_W), BF16),
        jax.ShapeDtypeStruct((n, IDX_HEAD_DIM), BF16),
        jax.ShapeDtypeStruct((n, 2 * DN_HEADS), F32),
        jax.ShapeDtypeStruct((n, 3 * DN_W), F32),
        jax.ShapeDtypeStruct((n, DN_W), F32),
        jax.ShapeDtypeStruct((n, 2 * d), F32),
    )
    out_specs = tuple(
        pl.BlockSpec((s.shape[0], tm), col) if s.shape[0] != n else pl.BlockSpec((tm, s.shape[1]), row)
        for s in out_shapes)
    return pl.pallas_call(
        _in_proj_body,
        grid=(n // tm,),
        in_specs=[
            pl.BlockSpec((tm, d), row),
            pl.BlockSpec((1, d), fixed),
            pl.BlockSpec((d, wtot), fixed),
            pl.BlockSpec((R_END, d), fixed),
            pl.BlockSpec((ATT_HEAD_DIM, 1), fixed),
            pl.BlockSpec((1, ATT_HEAD_DIM), fixed),
        ],
        out_specs=out_specs,
        out_shape=out_shapes,
        compiler_params=pltpu.CompilerParams(dimension_semantics=("parallel",), vmem_limit_bytes=VMEM_LIMIT),
    )(xf, g1.reshape(1, d), w_all, wt_all, q_gain.reshape(-1, 1), k_gain.reshape(1, -1))


FOLD_ROWS = 4 * SUBLANES
DSA_AHEAD = 2


def _dsa_body(qt_ref, iqt_ref, iwt_ref, k_ref, vt_ref, ik_ref, o_ref,
              key_ref, bias_ref, *acc_refs, tq, tk, ksel):
    i = pl.program_id(1)
    q0 = i * tq
    n_kb = (q0 + tq + tk - 1) // tk
    qpos = q0 + lax.broadcasted_iota(I32, (1, tq), 1)
    iwt = iwt_ref[...]

    def score_body(kb, carry):
        k0 = pl.multiple_of(kb * tk, tk)
        ikb = ik_ref[pl.ds(k0, tk), :]
        acc = jnp.zeros((tk, tq), F32)
        for hd in range(IDX_HEADS):
            sl = slice(hd * IDX_HEAD_DIM, (hd + 1) * IDX_HEAD_DIM)
            dots = jnp.dot(ikb, iqt_ref[sl, :], preferred_element_type=F32)
            acc = acc + iwt[hd:hd + 1, :] * jnp.maximum(dots, 0.0)
        bits = lax.bitcast_convert_type(acc, I32)
        keys = jnp.where(bits >= 0, bits, bits ^ INT_MAX)
        kpos = k0 + lax.broadcasted_iota(I32, (tk, 1), 0)
        key_ref[kb] = jnp.where(kpos <= qpos, keys, NEG_INF_KEY)
        return carry

    lax.fori_loop(0, n_kb, score_body, 0)

    def fold(x, op):
        return op(x.reshape(tk // FOLD_ROWS, FOLD_ROWS, tq), axis=0)

    def count_ge(thr):
        def body(kb, c):
            return c + fold(jnp.where(key_ref[kb] >= thr, 1.0, 0.0), jnp.sum)
        c = lax.fori_loop(0, n_kb, body, jnp.zeros((FOLD_ROWS, tq), F32))
        return jnp.sum(c, axis=0, keepdims=True)

    def bisect_cond(st):
        it, _, _, _, _, pending = st
        return jnp.logical_and(it < 32, jnp.max(pending) > 0.0)

    def bisect_body(st):
        it, lo, hi, c_lo, c_hi, pending = st
        mid = (lo >> 1) + (hi >> 1) + (lo & hi & 1)
        c = count_ge(mid)
        live = pending > 0.0
        up = jnp.logical_and(live, c >= ksel)
        down = jnp.logical_and(live, c < ksel)
        c_lo = jnp.where(up, c, c_lo)
        return (it + 1, jnp.where(up, mid, lo), jnp.where(down, mid, hi), c_lo, jnp.where(down, c, c_hi),
                jnp.where(c_lo == ksel, 0.0, pending))

    lo0 = jnp.full((1, tq), NEG_INF_KEY + 1, I32)
    hi0 = jnp.full((1, tq), INT_MAX, I32)
    zero = jnp.zeros((1, tq), F32)
    pending0 = jnp.where(qpos + 1 > ksel, 1.0, 0.0)
    _, thr, _, c_thr, c_above, _ = lax.while_loop(bisect_cond, bisect_body, (0, lo0, hi0, zero, zero, pending0))

    need = ksel - c_above
    has_excess = jnp.max(jnp.where(c_thr > ksel, 1.0, 0.0)) > 0.0

    @pl.when(has_excess)
    def _():
        lower = jnp.where(lax.broadcasted_iota(I32, (tk, tk), 0) >= lax.broadcasted_iota(I32, (tk, tk), 1),
                          1.0, 0.0).astype(BF16)

        def body(kb, seen):
            keys = key_ref[kb]
            tie = keys == thr
            tie_f = jnp.where(tie, 1.0, 0.0)
            prefix = jnp.dot(lower, tie_f.astype(BF16), preferred_element_type=F32) + seen
            drop = jnp.logical_and(tie, prefix > need)
            key_ref[kb] = jnp.where(drop, NEG_INF_KEY, keys)
            return seen + jnp.sum(tie_f, axis=0, keepdims=True)

        lax.fori_loop(0, n_kb, body, jnp.zeros((1, tq), F32))

    for acc_ref in acc_refs:
        acc_ref[...] = jnp.zeros(acc_ref.shape, F32)
    grp = ATT_HEADS // ATT_KV_HEADS

    def att_body(kb, carry):
        m_all, l_all = carry
        k0 = pl.multiple_of(kb * tk, tk)
        kblk = k_ref[pl.ds(k0, tk), :]
        vtb = vt_ref[kb]
        bias_ref[...] = jnp.where(key_ref[kb] >= thr, 0.0, -1e30)
        m_rows, l_rows = [], []

        def logits(hd):
            g = hd // grp
            return jnp.dot(kblk[:, g * ATT_HEAD_DIM:(g + 1) * ATT_HEAD_DIM],
                           qt_ref[hd * ATT_HEAD_DIM:(hd + 1) * ATT_HEAD_DIM, :],
                           preferred_element_type=F32) + bias_ref[...]

        def accumulate(hd, alpha, p):
            g = hd // grp
            acc_ref = acc_refs[hd]
            acc_ref[...] = alpha * acc_ref[...] + jnp.dot(vtb[g * ATT_HEAD_DIM:(g + 1) * ATT_HEAD_DIM, :], p,
                                                          preferred_element_type=F32)

        queue = [logits(hd) for hd in range(DSA_AHEAD)]
        pending = None
        for hd in range(ATT_HEADS):
            s = queue.pop(0)
            if hd + DSA_AHEAD < ATT_HEADS:
                queue.append(logits(hd + DSA_AHEAD))
            m_old = m_all[hd:hd + 1, :]
            m_new = jnp.maximum(m_old, jnp.max(fold(s, jnp.max), axis=0, keepdims=True))
            p = jnp.exp(s - m_new)
            alpha = jnp.exp(m_old - m_new)
            l_rows.append(alpha * l_all[hd:hd + 1, :] + jnp.sum(fold(p, jnp.sum), axis=0, keepdims=True))
            m_rows.append(m_new)
            if pending is not None:
                accumulate(*pending)
            pending = (hd, alpha, p.astype(BF16))
        accumulate(*pending)
        return jnp.concatenate(m_rows, axis=0), jnp.concatenate(l_rows, axis=0)

    m0 = jnp.full((ATT_HEADS, tq), -1e30, F32)
    _, l_fin = lax.fori_loop(0, n_kb, att_body, (m0, jnp.zeros((ATT_HEADS, tq), F32)))
    for pair in range(ATT_HEADS // 2):
        rows = [acc_refs[hd][...] / l_fin[hd:hd + 1, :] for hd in (2 * pair, 2 * pair + 1)]
        o_ref[:, pair * LANES:(pair + 1) * LANES] = jnp.concatenate(rows, axis=0).T.astype(o_ref.dtype)


def _dsa(qt, iqt, iwt, k, vt, ik, b, t, tq, tk):
    n = b * t
    nq = t // tq
    nkb = t // tk
    ksel = min(TOPK_MAX, t // 4)
    qcol = lambda bi, i: (0, bi * nq + i)
    brow = lambda bi, i: (bi, 0)
    vtb = vt.reshape(ATT_KV_W, n // tk, tk).transpose(1, 0, 2)
    return pl.pallas_call(
        functools.partial(_dsa_body, tq=tq, tk=tk, ksel=ksel),
        grid=(b, nq),
        in_specs=[
            pl.BlockSpec((ATT_Q_W, tq), qcol),
            pl.BlockSpec((IDX_Q_W, tq), qcol),
            pl.BlockSpec((IDX_HEADS, tq), qcol),
            pl.BlockSpec((t, ATT_KV_W), brow),
            pl.BlockSpec((nkb, ATT_KV_W, tk), lambda bi, i: (bi, 0, 0)),
            pl.BlockSpec((t, IDX_HEAD_DIM), brow),
        ],
        out_specs=pl.BlockSpec((tq, ATT_Q_W), lambda bi, i: (bi * nq + i, 0)),
        out_shape=jax.ShapeDtypeStruct((n, ATT_Q_W), BF16),
        scratch_shapes=[
            pltpu.VMEM((nkb, tk, tq), I32),
            pltpu.VMEM((tk, tq), F32),
        ] + [pltpu.VMEM((ATT_HEAD_DIM, tq), F32) for _ in range(ATT_HEADS)],
        compiler_params=pltpu.CompilerParams(dimension_semantics=("parallel", "arbitrary"),
                                             vmem_limit_bytes=VMEM_LIMIT),
    )(qt, iqt, iwt, k, vtb, ik)


def _deltanet_body(x_ref, dz_ref, dab_ref, dabt_ref, cw_ref, alog_r_ref, bias_r_ref, alog_c_ref, bias_c_ref,
                   og_ref, y_ref, carry_ref, state_ref, *, tb):
    c = DN_CHUNK
    d = DN_HEAD_DIM
    nh = DN_HEADS

    @pl.when(pl.program_id(1) == 0)
    def _():
        carry_ref[...] = jnp.zeros(carry_ref.shape, F32)
        state_ref[...] = jnp.zeros(state_ref.shape, F32)

    xb = x_ref[...]
    xx = jnp.concatenate([carry_ref[...], xb], axis=0)
    cw = cw_ref[...]
    off = SUBLANES - (DN_CONV - 1)
    conv = cw[0:1] * xx[off:off + tb]
    for j in range(1, DN_CONV):
        conv = conv + cw[j:j + 1] * xx[off + j:off + j + tb]
    carry_ref[...] = xb[tb - SUBLANES:tb]
    qkv = conv * _sigmoid(conv)

    dab = dab_ref[...]
    g_col = -jnp.exp(alog_r_ref[...]) * _softplus(dab[:, 0:nh] + bias_r_ref[...])
    beta_col = _sigmoid(dab[:, nh:2 * nh])
    g_row = -jnp.exp(alog_c_ref[...]) * _softplus(dabt_ref[0:nh, :] + bias_c_ref[...])

    ri = lax.broadcasted_iota(I32, (tb, tb), 0)
    ci = lax.broadcasted_iota(I32, (tb, tb), 1)
    same_chunk = (ri // c) == (ci // c)
    lower_blk = jnp.where(jnp.logical_and(same_chunk, ri >= ci), 1.0, 0.0).astype(BF16)
    upper_blk = jnp.where(jnp.logical_and(same_chunk, ri <= ci), 1.0, 0.0).astype(BF16)
    gc_col = _dot_exact_lhs(lower_blk, g_col)
    gc_row = _dot_exact_rhs(g_row, upper_blk)

    r64 = lax.broadcasted_iota(I32, (c, c), 0)
    c64 = lax.broadcasted_iota(I32, (c, c), 1)
    tri = r64 >= c64
    strict = r64 > c64
    eye = jnp.where(r64 == c64, 1.0, 0.0)
    og = og_ref[...]

    inst = [(ch, hd) for ch in range(tb // c) for hd in range(nh)]
    rows = lambda ch: slice(ch * c, (ch + 1) * c)
    qs, ks, vs, bcols, gcols, decays, kbetas = [], [], [], [], [], [], []
    for ch, hd in inst:
        rs = rows(ch)
        qc = qkv[rs, hd * d:(hd + 1) * d]
        kc = qkv[rs, nh * d + hd * d:nh * d + (hd + 1) * d]
        qs.append(qc * lax.rsqrt(jnp.sum(qc * qc, axis=-1, keepdims=True) + EPS) * (d ** -0.5))
        ks.append(kc * lax.rsqrt(jnp.sum(kc * kc, axis=-1, keepdims=True) + EPS))
        vs.append(qkv[rs, 2 * nh * d + hd * d:2 * nh * d + (hd + 1) * d])
        bcols.append(beta_col[rs, hd:hd + 1])
        gcols.append(gc_col[rs, hd:hd + 1])
        decays.append(jnp.exp(jnp.where(tri, gcols[-1] - gc_row[hd:hd + 1, rs], -jnp.inf)))
        kbetas.append(ks[-1] * bcols[-1])
    kk = [_dot_nt(kbetas[i], ks[i]) for i in range(len(inst))]
    qk = [_dot_nt(qs[i], ks[i]) for i in range(len(inst))]
    a_mats = [jnp.where(strict, kk[i] * decays[i], 0.0) for i in range(len(inst))]
    intras = [jnp.where(tri, qk[i] * decays[i], 0.0) for i in range(len(inst))]
    xs = [eye - a for a in a_mats]
    ps = a_mats
    for _ in range(5):
        ps = [_dot_hi(p, p) for p in ps]
        xs = [x + _dot_hi(x, p) for x, p in zip(xs, ps)]
    egs = [jnp.exp(g) for g in gcols]
    sols = [_dot(xs[i], jnp.concatenate([vs[i] * bcols[i], kbetas[i] * egs[i]], axis=1)) for i in range(len(inst))]
    qes = [qs[i] * egs[i] for i in range(len(inst))]
    glasts = [g[c - 1:c] for g in gcols]
    kdecs = [ks[i] * jnp.exp(glasts[i] - gcols[i]) for i in range(len(inst))]

    states = [state_ref[hd] for hd in range(nh)]
    for ch in range(tb // c):
        ii = [ch * nh + hd for hd in range(nh)]
        both = [_dot(jnp.concatenate([sols[i][:, d:], qes[i]], axis=0), states[hd]) for hd, i in enumerate(ii)]
        v_new = [sols[i][:, :d] - both[hd][:c] for hd, i in enumerate(ii)]
        outs = [both[hd][c:] + _dot(intras[i], v_new[hd]) for hd, i in enumerate(ii)]
        states = [states[hd] * jnp.exp(glasts[i]) + lax.dot_general(
            kdecs[i].astype(BF16), v_new[hd].astype(BF16), TN_DIMS, preferred_element_type=F32)
            for hd, i in enumerate(ii)]
        for hd in range(nh):
            ls = slice(hd * d, (hd + 1) * d)
            o = outs[hd]
            on = o * lax.rsqrt(jnp.mean(o * o, axis=-1, keepdims=True) + EPS) * og
            z = dz_ref[rows(ch), ls]
            y_ref[rows(ch), ls] = (on * (z * _sigmoid(z))).astype(y_ref.dtype)
    for hd in range(nh):
        state_ref[hd] = states[hd]


def _deltanet(dqkv, dz, dab, dabt, conv_w, a_log, dt_bias, out_gain, b, t, tb):
    n = b * t
    nb = t // tb
    row = lambda bi, j: (bi * nb + j, 0)
    col = lambda bi, j: (0, bi * nb + j)
    fixed = lambda bi, j: (0, 0)
    nh = DN_HEADS
    return pl.pallas_call(
        functools.partial(_deltanet_body, tb=tb),
        grid=(b, nb),
        in_specs=[
            pl.BlockSpec((tb, 3 * DN_W), row),
            pl.BlockSpec((tb, DN_W), row),
            pl.BlockSpec((tb, 2 * nh), row),
            pl.BlockSpec((2 * nh, tb), col),
            pl.BlockSpec((DN_CONV, 3 * DN_W), fixed),
            pl.BlockSpec((1, nh), fixed),
            pl.BlockSpec((1, nh), fixed),
            pl.BlockSpec((nh, 1), fixed),
            pl.BlockSpec((nh, 1), fixed),
            pl.BlockSpec((1, DN_HEAD_DIM), fixed),
        ],
        out_specs=pl.BlockSpec((tb, DN_W), row),
        out_shape=jax.ShapeDtypeStruct((n, DN_W), BF16),
        scratch_shapes=[
            pltpu.VMEM((SUBLANES, 3 * DN_W), F32),
            pltpu.VMEM((nh, DN_HEAD_DIM, DN_HEAD_DIM), F32),
        ],
        compiler_params=pltpu.CompilerParams(dimension_semantics=("parallel", "arbitrary"),
                                             vmem_limit_bytes=VMEM_LIMIT),
    )(dqkv, dz, dab, dabt, conv_w, a_log.reshape(1, nh), dt_bias.reshape(1, nh),
      a_log.reshape(nh, 1), dt_bias.reshape(nh, 1), out_gain.reshape(1, -1))


def _merge_body(x_ref, ya_ref, yd_ref, gab_ref, wa_ref, wb_ref, wo_ref, o_ref):
    dm = x_ref.shape[1]
    gab = gab_ref[...]
    a = jnp.dot(ya_ref[...], wa_ref[...], preferred_element_type=F32)
    bb = jnp.dot(yd_ref[...], wb_ref[...], preferred_element_type=F32)
    merged = _sigmoid(gab[:, :dm]) * a + _sigmoid(gab[:, dm:]) * bb
    o_ref[...] = x_ref[...] + jnp.dot(merged.astype(BF16), wo_ref[...], preferred_element_type=F32)


def _merge(xf, y_att, y_dn, gab, w_a, w_b, w_o, tm):
    n, d = xf.shape
    row = lambda i: (i, 0)
    fixed = lambda i: (0, 0)
    return pl.pallas_call(
        _merge_body,
        grid=(n // tm,),
        in_specs=[
            pl.BlockSpec((tm, d), row),
            pl.BlockSpec((tm, ATT_Q_W), row),
            pl.BlockSpec((tm, DN_W), row),
            pl.BlockSpec((tm, 2 * d), row),
            pl.BlockSpec((ATT_Q_W, d), fixed),
            pl.BlockSpec((DN_W, d), fixed),
            pl.BlockSpec((d, d), fixed),
        ],
        out_specs=pl.BlockSpec((tm, d), row),
        out_shape=jax.ShapeDtypeStruct((n, d), F32),
        compiler_params=pltpu.CompilerParams(dimension_semantics=("parallel",), vmem_limit_bytes=VMEM_LIMIT),
    )(xf, y_att, y_dn, gab, w_a.astype(BF16), w_b.astype(BF16), w_o.astype(BF16))


def _top16_rows(s, payload=None):
    rows = lax.broadcasted_iota(I32, s.shape, 0)
    big = s.shape[0]
    vals, pays = [], []
    for _ in range(PEER_TOPK):
        m = jnp.max(s, axis=0, keepdims=True)
        am = jnp.min(jnp.where(s == m, rows, big), axis=0, keepdims=True)
        hit = rows == am
        vals.append(m)
        if payload is None:
            pays.append(am)
        else:
            pays.append(jnp.sum(jnp.where(hit, payload, 0), axis=0, keepdims=True))
        s = jnp.where(hit, -jnp.inf, s)
    return jnp.concatenate(vals, axis=0), jnp.concatenate(pays, axis=0)


def _pair_candidates(s0, i0, s1, i1):
    k = PEER_TOPK
    rows8 = lax.broadcasted_iota(I32, (SUBLANES, s0.shape[1]), 0)
    cs = [s0[0:1] + s1]
    ce = [i0[0:1] * PEER_N_KEYS + i1]
    for i in range(1, SUBLANES):
        valid = rows8 < k // (i + 1)
        cs.append(jnp.where(valid, s0[i:i + 1] + s1[0:SUBLANES], -jnp.inf))
        ce.append(i0[i:i + 1] * PEER_N_KEYS + i1[0:SUBLANES])
    cs.append(s0[SUBLANES:k] + s1[0:1])
    ce.append(i0[SUBLANES:k] * PEER_N_KEYS + i1[0:1])
    return jnp.concatenate(cs, axis=0), jnp.concatenate(ce, axis=0)


def _peer_route_body(x_ref, g2_ref, wq_ref, sk_ref, ids_ref, gates_ref, *, tm):
    x = x_ref[...]
    ms = jnp.mean(x * x, axis=-1, keepdims=True)
    h = (x * lax.rsqrt(ms + EPS) * g2_ref[...]).astype(BF16)
    q = jnp.dot(h, wq_ref[...], preferred_element_type=F32).astype(BF16)
    half = PEER_KEY_DIM // 2
    for hd in range(PEER_HEADS):
        tops = []
        for p in range(2):
            c0 = hd * PEER_KEY_DIM + p * half
            st = lax.dot_general(sk_ref[2 * hd + p], q[:, c0:c0 + half], NT_DIMS,
                                 preferred_element_type=F32)
            tops.append(_top16_rows(st))
        (s0, i0), (s1, i1) = tops
        cand_s, cand_e = _pair_candidates(s0, i0, s1, i1)
        best, expert = _top16_rows(cand_s, cand_e)
        e = jnp.exp(best - best[0:1])
        gate = e / jnp.sum(e, axis=0, keepdims=True)
        rs = slice(hd * PEER_TOPK, (hd + 1) * PEER_TOPK)
        ids_ref[rs, :] = expert
        gates_ref[rs, :] = gate


def _peer_route(x1, g2, w_query, sub_keys, tm):
    n, d = x1.shape
    nsel = PEER_HEADS * PEER_TOPK
    half = PEER_KEY_DIM // 2
    sk = sub_keys.reshape(PEER_HEADS * 2, PEER_N_KEYS, half).astype(BF16)
    return pl.pallas_call(
        functools.partial(_peer_route_body, tm=tm),
        grid=(n // tm,),
        in_specs=[
            pl.BlockSpec((tm, d), lambda i: (i, 0)),
            pl.BlockSpec((1, d), lambda i: (0, 0)),
            pl.BlockSpec((d, PEER_HEADS * PEER_KEY_DIM), lambda i: (0, 0)),
            pl.BlockSpec((PEER_HEADS * 2, PEER_N_KEYS, half), lambda i: (0, 0, 0)),
        ],
        out_specs=(pl.BlockSpec((nsel, tm), lambda i: (0, i)),
                   pl.BlockSpec((nsel, tm), lambda i: (0, i))),
        out_shape=(jax.ShapeDtypeStruct((nsel, n), I32), jax.ShapeDtypeStruct((nsel, n), F32)),
        compiler_params=pltpu.CompilerParams(dimension_semantics=("parallel",), vmem_limit_bytes=VMEM_LIMIT),
    )(x1, g2.reshape(1, d), w_query.astype(BF16), sk)


PEER_SLOTS = 4


def _peer_apply_body(ids_ref, x_ref, g2_ref, gates_ref, uv_hbm, o_ref, buf, coef_ref, sems, *, tb):
    nsel = PEER_HEADS * PEER_TOPK
    nchunk = x_ref.shape[1]
    dm = nchunk * LANES

    ngrp = nsel // SUBLANES

    def issue(tok, slot, k0, k1):
        for kk in range(k0, k1):
            pltpu.make_async_copy(uv_hbm.at[ids_ref[tok, kk]], buf.at[slot, kk], sems.at[slot]).start(
                priority=kk % 2)

    def wait_all(slot):
        pltpu.make_async_copy(uv_hbm.at[pl.ds(0, nsel)], buf.at[slot], sems.at[slot]).wait()

    eye = lax.broadcasted_iota(I32, (nsel, nsel), 0) == lax.broadcasted_iota(I32, (nsel, nsel), 1)
    sub = lax.broadcasted_iota(I32, (SUBLANES, LANES), 0)
    masks = {k: (sub & k) == 0 for k in (4, 2, 1)}
    g2 = g2_ref[...]

    def merge(x, y, k):
        if k == 4:
            return jnp.where(masks[k], x, y) + pltpu.roll(jnp.where(masks[k], y, x), k, axis=0)
        return jnp.where(masks[k], x + pltpu.roll(x, SUBLANES - k, axis=0), y + pltpu.roll(y, k, axis=0))

    order = (0, 4, 2, 6, 1, 5, 3, 7)

    def u_of(word):
        return lax.bitcast_convert_type(word << 16, F32)

    def v_of(word):
        return lax.bitcast_convert_type(word & jnp.uint32(0xFFFF0000), F32)

    def step(t, slot, nxt, nxt_slot):
        def prefetch(k0, k1):
            if nxt is not None:
                issue(nxt, nxt_slot, k0, k1)

        xt = x_ref[t]
        ssq = jnp.sum(jnp.sum(xt * xt, axis=1, keepdims=True), axis=0, keepdims=True)
        h8 = xt * lax.rsqrt(ssq * (1.0 / dm) + EPS) * g2
        grow = gates_ref[t]
        gcol = jnp.sum(jnp.where(eye, jnp.broadcast_to(grow, (nsel, nsel)), 0.0), axis=1, keepdims=True)

        wait_all(slot)
        per_grp = nsel // (2 * ngrp)
        groups = []
        for grp in range(ngrp):
            prefetch(grp * per_grp, (grp + 1) * per_grp)
            p = [u_of(buf[slot, grp * SUBLANES + order.index(j)]) * h8 for j in range(SUBLANES)]
            q4 = [merge(p[2 * i], p[2 * i + 1], 4) for i in range(4)]
            q2 = [merge(q4[2 * i], q4[2 * i + 1], 2) for i in range(2)]
            groups.append(merge(q2[0], q2[1], 1))
        colsum = jnp.concatenate(groups, axis=0)
        act = jnp.sum(colsum, axis=1, keepdims=True)
        gelu = 0.5 * act * (1.0 + lax.erf(act * (2.0 ** -0.5)))
        coef_ref[...] = jnp.broadcast_to(gcol * gelu, (nsel, LANES))
        acc = jnp.zeros((nchunk, LANES), F32)
        for kk in range(nsel):
            if kk % 2 == 0:
                prefetch(nsel // 2 + kk // 2, nsel // 2 + kk // 2 + 1)
            acc = acc + coef_ref[kk:kk + 1, :] * v_of(buf[slot, kk])
        o_ref[t] = xt + acc

    ahead = PEER_SLOTS - 1
    for s in range(ahead):
        issue(s, s, 0, nsel)

    def group_body(g, carry):
        t0 = g * PEER_SLOTS
        for s in range(PEER_SLOTS):
            step(t0 + s, s, t0 + s + ahead, (s + ahead) % PEER_SLOTS)
        return carry

    lax.fori_loop(0, tb // PEER_SLOTS - 1, group_body, 0)
    t0 = tb - PEER_SLOTS
    step(t0, 0, t0 + ahead, ahead % PEER_SLOTS)
    for s in range(1, PEER_SLOTS):
        step(t0 + s, s, None, None)


def _peer_apply(x1, g2, ids, gates, peer_u, peer_v, tb):
    n, d = x1.shape
    nsel = PEER_HEADS * PEER_TOPK
    ne = peer_u.shape[0]
    nchunk = d // LANES

    def half_words(tab):
        return lax.bitcast_convert_type(tab.astype(BF16), jnp.uint16).astype(jnp.uint32)

    uv = ((half_words(peer_v) << 16) | half_words(peer_u)).reshape(ne, nchunk, LANES)
    out = pl.pallas_call(
        functools.partial(_peer_apply_body, tb=tb),
        grid=(n // tb,),
        in_specs=[
            pl.BlockSpec((tb, nsel), lambda i: (i, 0), memory_space=pltpu.SMEM),
            pl.BlockSpec((tb, nchunk, LANES), lambda i: (i, 0, 0)),
            pl.BlockSpec((nchunk, LANES), lambda i: (0, 0)),
            pl.BlockSpec((tb, 1, nsel), lambda i: (i, 0, 0)),
            pl.BlockSpec(memory_space=pl.ANY),
        ],
        out_specs=pl.BlockSpec((tb, nchunk, LANES), lambda i: (i, 0, 0)),
        out_shape=jax.ShapeDtypeStruct((n, nchunk, LANES), F32),
        scratch_shapes=[
            pltpu.VMEM((PEER_SLOTS, nsel, nchunk, LANES), jnp.uint32),
            pltpu.VMEM((nsel, LANES), F32),
            pltpu.SemaphoreType.DMA((PEER_SLOTS,)),
        ],
        compiler_params=pltpu.CompilerParams(dimension_semantics=("arbitrary",), vmem_limit_bytes=VMEM_LIMIT),
    )(ids, x1.reshape(n, nchunk, LANES), g2.reshape(nchunk, LANES), gates.reshape(n, 1, nsel), uv)
    return out.reshape(n, d)


def _block_sizes(t):
    return dict(
        tm_proj=256,
        tq=min(256, t),
        tk=min(512, t),
        tb_dn=min(256, t),
        tm_merge=512,
        tm_route=LANES,
        tb_peer=128,
    )


def kernel(x, norm1_gain, w_in, q_norm_gain, k_norm_gain, dn_conv_w, dn_a_log, dn_dt_bias, dn_out_norm_gain,
           w_att_branch, w_dn_branch, w_o, norm2_gain, peer_w_query, peer_sub_keys, peer_u, peer_v):
    b, t, d = x.shape
    n = b * t
    bs = _block_sizes(t)
    xf = x.reshape(n, d)
    for layer in range(w_in.shape[0]):
        (qt, iqt, vt, iwt, dabt, k, ik, dab, dqkv, dz, gab) = _in_proj(
            xf, norm1_gain[layer], w_in[layer], q_norm_gain[layer], k_norm_gain[layer], bs["tm_proj"])
        y_att = _dsa(qt, iqt, iwt, k, vt, ik, b, t, bs["tq"], bs["tk"])
        y_dn = _deltanet(dqkv, dz, dab, dabt, dn_conv_w[layer], dn_a_log[layer], dn_dt_bias[layer],
                         dn_out_norm_gain[layer], b, t, bs["tb_dn"])
        x1 = _merge(xf, y_att, y_dn, gab, w_att_branch[layer], w_dn_branch[layer], w_o[layer], bs["tm_merge"])
        ids_t, gates_t = _peer_route(x1, norm2_gain[layer], peer_w_query[layer], peer_sub_keys[layer],
                                     bs["tm_route"])
        xf = _peer_apply(x1, norm2_gain[layer], ids_t.T, gates_t.T, peer_u[layer], peer_v[layer], bs["tb_peer"])
    return xf.reshape(b, t, d)
```

```python
import functools

import jax
import jax.numpy as jnp
import numpy as np
from jax import lax
from jax.experimental import pallas as pl
from jax.experimental.pallas import tpu as pltpu

F32 = jnp.float32
BF16 = jnp.bfloat16
I32 = jnp.int32

ATT_HEADS = 8
ATT_KV_HEADS = 2
ATT_HEAD_DIM = 64
IDX_HEADS = 8
IDX_HEAD_DIM = 64
TOPK_MAX = 256
DN_HEADS = 4
DN_HEAD_DIM = 128
DN_CONV = 4
DN_CHUNK = 64
PEER_HEADS = 8
PEER_N_KEYS = 128
PEER_KEY_DIM = 256
PEER_TOPK = 16
EPS = 1e-6

ATT_Q_W = ATT_HEADS * ATT_HEAD_DIM
ATT_KV_W = ATT_KV_HEADS * ATT_HEAD_DIM
IDX_Q_W = IDX_HEADS * IDX_HEAD_DIM
DN_W = DN_HEADS * DN_HEAD_DIM

LANES = 128
SUBLANES = 8
VMEM_LIMIT = 56 * 1024 * 1024

NEG_INF_KEY = int(np.int32(np.uint32(0xFF800000) ^ np.uint32(0x7FFFFFFF)))
INT_MAX = int(np.iinfo(np.int32).max)

NT_DIMS = (((1,), (1,)), ((), ()))
TN_DIMS = (((0,), (0,)), ((), ()))


def _sigmoid(x):
    return 1.0 / (1.0 + jnp.exp(-x))


def _softplus(x):
    return jnp.maximum(x, 0.0) + jnp.log(1.0 + jnp.exp(-jnp.abs(x)))


def _dot(a, b):
    return jnp.dot(a.astype(BF16), b.astype(BF16), preferred_element_type=F32)


def _dot_nt(a, b):
    return lax.dot_general(a.astype(BF16), b.astype(BF16), NT_DIMS, preferred_element_type=F32)


def _split2(x):
    hi = x.astype(BF16)
    lo = (x - hi.astype(F32)).astype(BF16)
    return hi, lo


def _dot_hi(a, b):
    a1, a2 = _split2(a)
    b1, b2 = _split2(b)
    out = jnp.dot(a1, b1, preferred_element_type=F32)
    out = out + jnp.dot(a1, b2, preferred_element_type=F32)
    out = out + jnp.dot(a2, b1, preferred_element_type=F32)
    return out


def _dot_exact_rhs(a, b_exact):
    a1 = a.astype(BF16)
    r1 = a - a1.astype(F32)
    a2 = r1.astype(BF16)
    a3 = (r1 - a2.astype(F32)).astype(BF16)
    out = jnp.dot(a1, b_exact, preferred_element_type=F32)
    out = out + jnp.dot(a2, b_exact, preferred_element_type=F32)
    out = out + jnp.dot(a3, b_exact, preferred_element_type=F32)
    return out


def _dot_exact_lhs(a_exact, b):
    b1 = b.astype(BF16)
    r1 = b - b1.astype(F32)
    b2 = r1.astype(BF16)
    b3 = (r1 - b2.astype(F32)).astype(BF16)
    out = jnp.dot(a_exact, b1, preferred_element_type=F32)
    out = out + jnp.dot(a_exact, b2, preferred_element_type=F32)
    out = out + jnp.dot(a_exact, b3, preferred_element_type=F32)
    return out


C_AK = 0
C_SM = C_AK + ATT_KV_W
C_DQKV = C_SM + LANES
C_DZ = C_DQKV + 3 * DN_W
C_GAB = C_DZ + DN_W
SM_DAB = IDX_HEAD_DIM
R_AQ = 0
R_IQ = R_AQ + ATT_Q_W
R_AV = R_IQ + IDX_Q_W
R_IW = R_AV + ATT_KV_W
R_DAB = R_IW + IDX_HEADS
R_END = R_DAB + 2 * DN_HEADS


def _in_proj_body(x_ref, g1_ref, w_ref, wt_ref, qg_ref, kg_ref,
                  qt_ref, iqt_ref, vt_ref, iwt_ref, dabt_ref, k_ref, ik_ref, dab_ref, dqkv_ref, dz_ref, gab_ref):
    x = x_ref[...]
    ms = jnp.mean(x * x, axis=-1, keepdims=True)
    h = (x * lax.rsqrt(ms + EPS) * g1_ref[...]).astype(BF16)

    rt = lax.dot_general(wt_ref[...], h, NT_DIMS, preferred_element_type=F32)
    qg = qg_ref[...] * (ATT_HEAD_DIM ** -0.5)
    for hd in range(ATT_HEADS):
        sl = slice(R_AQ + hd * ATT_HEAD_DIM, R_AQ + (hd + 1) * ATT_HEAD_DIM)
        blk = rt[sl]
        qt_ref[sl, :] = (blk * lax.rsqrt(jnp.mean(blk * blk, axis=0, keepdims=True) + EPS) * qg).astype(qt_ref.dtype)
    iqt_ref[...] = rt[R_IQ:R_AV].astype(iqt_ref.dtype)
    vt_ref[...] = rt[R_AV:R_IW].astype(vt_ref.dtype)
    iwt_ref[...] = rt[R_IW:R_DAB]
    dabt_ref[...] = rt[R_DAB:R_END]

    def proj(c0, c1):
        return jnp.dot(h, w_ref[:, c0:c1], preferred_element_type=F32)

    ak = proj(C_AK, C_SM)
    kg = kg_ref[...]
    for hd in range(ATT_KV_HEADS):
        sl = slice(hd * ATT_HEAD_DIM, (hd + 1) * ATT_HEAD_DIM)
        blk = ak[:, sl]
        k_ref[:, sl] = (blk * lax.rsqrt(jnp.mean(blk * blk, axis=-1, keepdims=True) + EPS) * kg).astype(k_ref.dtype)
    sm = proj(C_SM, C_DQKV)
    ik_ref[...] = sm[:, :IDX_HEAD_DIM].astype(ik_ref.dtype)
    dab_ref[...] = sm[:, SM_DAB:SM_DAB + 2 * DN_HEADS]
    dqkv_ref[...] = proj(C_DQKV, C_DZ)
    dz_ref[...] = proj(C_DZ, C_GAB)
    gab_ref[...] = proj(C_GAB, C_GAB + 2 * x.shape[1])


def _in_proj(xf, g1, w_in, q_gain, k_gain, tm):
    n, d = xf.shape
    cuts = np.cumsum([ATT_Q_W, ATT_KV_W, ATT_KV_W, IDX_Q_W, IDX_HEAD_DIM, IDX_HEADS,
                      DN_W, DN_W, DN_W, DN_W, DN_HEADS, DN_HEADS, d, d])[:-1].tolist()
    aq, ak, av, iq, ik, iw, dq, dk, dv, dz, da, db, ga, gb = jnp.split(w_in, cuts, axis=-1)
    pad = jnp.zeros((d, LANES - IDX_HEAD_DIM - 2 * DN_HEADS), w_in.dtype)
    w_all = jnp.concatenate([ak, ik, da, db, pad, dq, dk, dv, dz, ga, gb], axis=-1).astype(BF16)
    wt_all = jnp.concatenate([aq, iq, av, iw, da, db], axis=-1).T.astype(BF16)
    wtot = w_all.shape[1]
    row = lambda i: (i, 0)
    col = lambda i: (0, i)
    fixed = lambda i: (0, 0)
    out_shapes = (
        jax.ShapeDtypeStruct((ATT_Q_W, n), BF16),
        jax.ShapeDtypeStruct((IDX_Q_W, n), BF16),
        jax.ShapeDtypeStruct((ATT_KV_W, n), BF16),
        jax.ShapeDtypeStruct((IDX_HEADS, n), F32),
        jax.ShapeDtypeStruct((2 * DN_HEADS, n), F32),
        jax.ShapeDtypeStruct((n, ATT_KV_W), BF16),
        jax.ShapeDtypeStruct((n, IDX_HEAD_DIM), BF16),
        jax.ShapeDtypeStruct((n, 2 * DN_HEADS), F32),
        jax.ShapeDtypeStruct((n, 3 * DN_W), F32),
        jax.ShapeDtypeStruct((n, DN_W), F32),
        jax.ShapeDtypeStruct((n, 2 * d), F32),
    )
    out_specs = tuple(
        pl.BlockSpec((s.shape[0], tm), col) if s.shape[0] != n else pl.BlockSpec((tm, s.shape[1]), row)
        for s in out_shapes)
    return pl.pallas_call(
        _in_proj_body,
        grid=(n // tm,),
        in_specs=[
            pl.BlockSpec((tm, d), row),
            pl.BlockSpec((1, d), fixed),
            pl.BlockSpec((d, wtot), fixed),
            pl.BlockSpec((R_END, d), fixed),
            pl.BlockSpec((ATT_HEAD_DIM, 1), fixed),
            pl.BlockSpec((1, ATT_HEAD_DIM), fixed),
        ],
        out_specs=out_specs,
        out_shape=out_shapes,
        compiler_params=pltpu.CompilerParams(dimension_semantics=("parallel",), vmem_limit_bytes=VMEM_LIMIT),
    )(xf, g1.reshape(1, d), w_all, wt_all, q_gain.reshape(-1, 1), k_gain.reshape(1, -1))


FOLD_ROWS = 4 * SUBLANES
DSA_AHEAD = 2


def _dsa_body(qt_ref, iqt_ref, iwt_ref, k_ref, vt_ref, ik_ref, o_ref,
              key_ref, bias_ref, *acc_refs, tq, tk, ksel):
    i = pl.program_id(1)
    q0 = i * tq
    n_kb = (q0 + tq + tk - 1) // tk
    qpos = q0 + lax.broadcasted_iota(I32, (1, tq), 1)
    iwt = iwt_ref[...]

    def score_body(kb, carry):
        k0 = pl.multiple_of(kb * tk, tk)
        ikb = ik_ref[pl.ds(k0, tk), :]
        acc = jnp.zeros((tk, tq), F32)
        for hd in range(IDX_HEADS):
            sl = slice(hd * IDX_HEAD_DIM, (hd + 1) * IDX_HEAD_DIM)
            dots = jnp.dot(ikb, iqt_ref[sl, :], preferred_element_type=F32)
            acc = acc + iwt[hd:hd + 1, :] * jnp.maximum(dots, 0.0)
        bits = lax.bitcast_convert_type(acc, I32)
        keys = jnp.where(bits >= 0, bits, bits ^ INT_MAX)
        kpos = k0 + lax.broadcasted_iota(I32, (tk, 1), 0)
        key_ref[kb] = jnp.where(kpos <= qpos, keys, NEG_INF_KEY)
        return carry

    lax.fori_loop(0, n_kb, score_body, 0)

    def fold(x, op):
        return op(x.reshape(tk // FOLD_ROWS, FOLD_ROWS, tq), axis=0)

    def count_ge(thr):
        def body(kb, c):
            return c + fold(jnp.where(key_ref[kb] >= thr, 1.0, 0.0), jnp.sum)
        c = lax.fori_loop(0, n_kb, body, jnp.zeros((FOLD_ROWS, tq), F32))
        return jnp.sum(c, axis=0, keepdims=True)

    def bisect_cond(st):
        it, _, _, _, _, pending = st
        return jnp.logical_and(it < 32, jnp.max(pending) > 0.0)

    def bisect_body(st):
        it, lo, hi, c_lo, c_hi, pending = st
        mid = (lo >> 1) + (hi >> 1) + (lo & hi & 1)
        c = count_ge(mid)
        live = pending > 0.0
        up = jnp.logical_and(live, c >= ksel)
        down = jnp.logical_and(live, c < ksel)
        c_lo = jnp.where(up, c, c_lo)
        return (it + 1, jnp.where(up, mid, lo), jnp.where(down, mid, hi), c_lo, jnp.where(down, c, c_hi),
                jnp.where(c_lo == ksel, 0.0, pending))

    lo0 = jnp.full((1, tq), NEG_INF_KEY + 1, I32)
    hi0 = jnp.full((1, tq), INT_MAX, I32)
    zero = jnp.zeros((1, tq), F32)
    pending0 = jnp.where(qpos + 1 > ksel, 1.0, 0.0)
    _, thr, _, c_thr, c_above, _ = lax.while_loop(bisect_cond, bisect_body, (0, lo0, hi0, zero, zero, pending0))

    need = ksel - c_above
    has_excess = jnp.max(jnp.where(c_thr > ksel, 1.0, 0.0)) > 0.0

    @pl.when(has_excess)
    def _():
        lower = jnp.where(lax.broadcasted_iota(I32, (tk, tk), 0) >= lax.broadcasted_iota(I32, (tk, tk), 1),
                          1.0, 0.0).astype(BF16)

        def body(kb, seen):
            keys = key_ref[kb]
            tie = keys == thr
            tie_f = jnp.where(tie, 1.0, 0.0)
            prefix = jnp.dot(lower, tie_f.astype(BF16), preferred_element_type=F32) + seen
            drop = jnp.logical_and(tie, prefix > need)
            key_ref[kb] = jnp.where(drop, NEG_INF_KEY, keys)
            return seen + jnp.sum(tie_f, axis=0, keepdims=True)

        lax.fori_loop(0, n_kb, body, jnp.zeros((1, tq), F32))

    for acc_ref in acc_refs:
        acc_ref[...] = jnp.zeros(acc_ref.shape, F32)
    grp = ATT_HEADS // ATT_KV_HEADS

    def att_body(kb, carry):
        m_all, l_all = carry
        k0 = pl.multiple_of(kb * tk, tk)
        kblk = k_ref[pl.ds(k0, tk), :]
        vtb = vt_ref[kb]
        bias_ref[...] = jnp.where(key_ref[kb] >= thr, 0.0, -1e30)
        m_rows, l_rows = [], []

        def logits(hd):
            g = hd // grp
            return jnp.dot(kblk[:, g * ATT_HEAD_DIM:(g + 1) * ATT_HEAD_DIM],
                           qt_ref[hd * ATT_HEAD_DIM:(hd + 1) * ATT_HEAD_DIM, :],
                           preferred_element_type=F32) + bias_ref[...]

        def accumulate(hd, alpha, p):
            g = hd // grp
            acc_ref = acc_refs[hd]
            acc_ref[...] = alpha * acc_ref[...] + jnp.dot(vtb[g * ATT_HEAD_DIM:(g + 1) * ATT_HEAD_DIM, :], p,
                                                          preferred_element_type=F32)

        queue = [logits(hd) for hd in range(DSA_AHEAD)]
        pending = None
        for hd in range(ATT_HEADS):
            s = queue.pop(0)
            if hd + DSA_AHEAD < ATT_HEADS:
                queue.append(logits(hd + DSA_AHEAD))
            m_old = m_all[hd:hd + 1, :]
            m_new = jnp.maximum(m_old, jnp.max(fold(s, jnp.max), axis=0, keepdims=True))
            p = jnp.exp(s - m_new)
            alpha = jnp.exp(m_old - m_new)
            l_rows.append(alpha * l_all[hd:hd + 1, :] + jnp.sum(fold(p, jnp.sum), axis=0, keepdims=True))
            m_rows.append(m_new)
            if pending is not None:
                accumulate(*pending)
            pending = (hd, alpha, p.astype(BF16))
        accumulate(*pending)
        return jnp.concatenate(m_rows, axis=0), jnp.concatenate(l_rows, axis=0)

    m0 = jnp.full((ATT_HEADS, tq), -1e30, F32)
    _, l_fin = lax.fori_loop(0, n_kb, att_body, (m0, jnp.zeros((ATT_HEADS, tq), F32)))
    for pair in range(ATT_HEADS // 2):
        rows = [acc_refs[hd][...] / l_fin[hd:hd + 1, :] for hd in (2 * pair, 2 * pair + 1)]
        o_ref[:, pair * LANES:(pair + 1) * LANES] = jnp.concatenate(rows, axis=0).T.astype(o_ref.dtype)


def _dsa(qt, iqt, iwt, k, vt, ik, b, t, tq, tk):
    n = b * t
    nq = t // tq
    nkb = t // tk
    ksel = min(TOPK_MAX, t // 4)
    qcol = lambda bi, i: (0, bi * nq + i)
    brow = lambda bi, i: (bi, 0)
    vtb = vt.reshape(ATT_KV_W, n // tk, tk).transpose(1, 0, 2)
    return pl.pallas_call(
        functools.partial(_dsa_body, tq=tq, tk=tk, ksel=ksel),
        grid=(b, nq),
        in_specs=[
            pl.BlockSpec((ATT_Q_W, tq), qcol),
            pl.BlockSpec((IDX_Q_W, tq), qcol),
            pl.BlockSpec((IDX_HEADS, tq), qcol),
            pl.BlockSpec((t, ATT_KV_W), brow),
            pl.BlockSpec((nkb, ATT_KV_W, tk), lambda bi, i: (bi, 0, 0)),
            pl.BlockSpec((t, IDX_HEAD_DIM), brow),
        ],
        out_specs=pl.BlockSpec((tq, ATT_Q_W), lambda bi, i: (bi * nq + i, 0)),
        out_shape=jax.ShapeDtypeStruct((n, ATT_Q_W), BF16),
        scratch_shapes=[
            pltpu.VMEM((nkb, tk, tq), I32),
            pltpu.VMEM((tk, tq), F32),
        ] + [pltpu.VMEM((ATT_HEAD_DIM, tq), F32) for _ in range(ATT_HEADS)],
        compiler_params=pltpu.CompilerParams(dimension_semantics=("parallel", "arbitrary"),
                                             vmem_limit_bytes=VMEM_LIMIT),
    )(qt, iqt, iwt, k, vtb, ik)


def _deltanet_body(x_ref, dz_ref, dab_ref, dabt_ref, cw_ref, alog_r_ref, bias_r_ref, alog_c_ref, bias_c_ref,
                   og_ref, y_ref, carry_ref, state_ref, *, tb):
    c = DN_CHUNK
    d = DN_HEAD_DIM
    nh = DN_HEADS

    @pl.when(pl.program_id(1) == 0)
    def _():
        carry_ref[...] = jnp.zeros(carry_ref.shape, F32)
        state_ref[...] = jnp.zeros(state_ref.shape, F32)

    xb = x_ref[...]
    xx = jnp.concatenate([carry_ref[...], xb], axis=0)
    cw = cw_ref[...]
    off = SUBLANES - (DN_CONV - 1)
    conv = cw[0:1] * xx[off:off + tb]
    for j in range(1, DN_CONV):
        conv = conv + cw[j:j + 1] * xx[off + j:off + j + tb]
    carry_ref[...] = xb[tb - SUBLANES:tb]
    qkv = conv * _sigmoid(conv)

    dab = dab_ref[...]
    g_col = -jnp.exp(alog_r_ref[...]) * _softplus(dab[:, 0:nh] + bias_r_ref[...])
    beta_col = _sigmoid(dab[:, nh:2 * nh])
    g_row = -jnp.exp(alog_c_ref[...]) * _softplus(dabt_ref[0:nh, :] + bias_c_ref[...])

    ri = lax.broadcasted_iota(I32, (tb, tb), 0)
    ci = lax.broadcasted_iota(I32, (tb, tb), 1)
    same_chunk = (ri // c) == (ci // c)
    lower_blk = jnp.where(jnp.logical_and(same_chunk, ri >= ci), 1.0, 0.0).astype(BF16)
    upper_blk = jnp.where(jnp.logical_and(same_chunk, ri <= ci), 1.0, 0.0).astype(BF16)
    gc_col = _dot_exact_lhs(lower_blk, g_col)
    gc_row = _dot_exact_rhs(g_row, upper_blk)

    r64 = lax.broadcasted_iota(I32, (c, c), 0)
    c64 = lax.broadcasted_iota(I32, (c, c), 1)
    tri = r64 >= c64
    strict = r64 > c64
    eye = jnp.where(r64 == c64, 1.0, 0.0)
    og = og_ref[...]

    inst = [(ch, hd) for ch in range(tb // c) for hd in range(nh)]
    rows = lambda ch: slice(ch * c, (ch + 1) * c)
    qs, ks, vs, bcols, gcols, decays, kbetas = [], [], [], [], [], [], []
    for ch, hd in inst:
        rs = rows(ch)
        qc = qkv[rs, hd * d:(hd + 1) * d]
        kc = qkv[rs, nh * d + hd * d:nh * d + (hd + 1) * d]
        qs.append(qc * lax.rsqrt(jnp.sum(qc * qc, axis=-1, keepdims=True) + EPS) * (d ** -0.5))
        ks.append(kc * lax.rsqrt(jnp.sum(kc * kc, axis=-1, keepdims=True) + EPS))
        vs.append(qkv[rs, 2 * nh * d + hd * d:2 * nh * d + (hd + 1) * d])
        bcols.append(beta_col[rs, hd:hd + 1])
        gcols.append(gc_col[rs, hd:hd + 1])
        decays.append(jnp.exp(jnp.where(tri, gcols[-1] - gc_row[hd:hd + 1, rs], -jnp.inf)))
        kbetas.append(ks[-1] * bcols[-1])
    kk = [_dot_nt(kbetas[i], ks[i]) for i in range(len(inst))]
    qk = [_dot_nt(qs[i], ks[i]) for i in range(len(inst))]
    a_mats = [jnp.where(strict, kk[i] * decays[i], 0.0) for i in range(len(inst))]
    intras = [jnp.where(tri, qk[i] * decays[i], 0.0) for i in range(len(inst))]
    xs = [eye - a for a in a_mats]
    ps = a_mats
    for _ in range(5):
        ps = [_dot_hi(p, p) for p in ps]
        xs = [x + _dot_hi(x, p) for x, p in zip(xs, ps)]
    egs = [jnp.exp(g) for g in gcols]
    sols = [_dot(xs[i], jnp.concatenate([vs[i] * bcols[i], kbetas[i] * egs[i]], axis=1)) for i in range(len(inst))]
    qes = [qs[i] * egs[i] for i in range(len(inst))]
    glasts = [g[c - 1:c] for g in gcols]
    kdecs = [ks[i] * jnp.exp(glasts[i] - gcols[i]) for i in range(len(inst))]

    states = [state_ref[hd] for hd in range(nh)]
    for ch in range(tb // c):
        ii = [ch * nh + hd for hd in range(nh)]
        both = [_dot(jnp.concatenate([sols[i][:, d:], qes[i]], axis=0), states[hd]) for hd, i in enumerate(ii)]
        v_new = [sols[i][:, :d] - both[hd][:c] for hd, i in enumerate(ii)]
        outs = [both[hd][c:] + _dot(intras[i], v_new[hd]) for hd, i in enumerate(ii)]
        states = [states[hd] * jnp.exp(glasts[i]) + lax.dot_general(
            kdecs[i].astype(BF16), v_new[hd].astype(BF16), TN_DIMS, preferred_element_type=F32)
            for hd, i in enumerate(ii)]
        for hd in range(nh):
            ls = slice(hd * d, (hd + 1) * d)
            o = outs[hd]
            on = o * lax.rsqrt(jnp.mean(o * o, axis=-1, keepdims=True) + EPS) * og
            z = dz_ref[rows(ch), ls]
            y_ref[rows(ch), ls] = (on * (z * _sigmoid(z))).astype(y_ref.dtype)
    for hd in range(nh):
        state_ref[hd] = states[hd]


def _deltanet(dqkv, dz, dab, dabt, conv_w, a_log, dt_bias, out_gain, b, t, tb):
    n = b * t
    nb = t // tb
    row = lambda bi, j: (bi * nb + j, 0)
    col = lambda bi, j: (0, bi * nb + j)
    fixed = lambda bi, j: (0, 0)
    nh = DN_HEADS
    return pl.pallas_call(
        functools.partial(_deltanet_body, tb=tb),
        grid=(b, nb),
        in_specs=[
            pl.BlockSpec((tb, 3 * DN_W), row),
            pl.BlockSpec((tb, DN_W), row),
            pl.BlockSpec((tb, 2 * nh), row),
            pl.BlockSpec((2 * nh, tb), col),
            pl.BlockSpec((DN_CONV, 3 * DN_W), fixed),
            pl.BlockSpec((1, nh), fixed),
            pl.BlockSpec((1, nh), fixed),
            pl.BlockSpec((nh, 1), fixed),
            pl.BlockSpec((nh, 1), fixed),
            pl.BlockSpec((1, DN_HEAD_DIM), fixed),
        ],
        out_specs=pl.BlockSpec((tb, DN_W), row),
        out_shape=jax.ShapeDtypeStruct((n, DN_W), BF16),
        scratch_shapes=[
            pltpu.VMEM((SUBLANES, 3 * DN_W), F32),
            pltpu.VMEM((nh, DN_HEAD_DIM, DN_HEAD_DIM), F32),
        ],
        compiler_params=pltpu.CompilerParams(dimension_semantics=("parallel", "arbitrary"),
                                             vmem_limit_bytes=VMEM_LIMIT),
    )(dqkv, dz, dab, dabt, conv_w, a_log.reshape(1, nh), dt_bias.reshape(1, nh),
      a_log.reshape(nh, 1), dt_bias.reshape(nh, 1), out_gain.reshape(1, -1))


def _merge_body(x_ref, ya_ref, yd_ref, gab_ref, wa_ref, wb_ref, wo_ref, o_ref):
    dm = x_ref.shape[1]
    gab = gab_ref[...]
    a = jnp.dot(ya_ref[...], wa_ref[...], preferred_element_type=F32)
    bb = jnp.dot(yd_ref[...], wb_ref[...], preferred_element_type=F32)
    merged = _sigmoid(gab[:, :dm]) * a + _sigmoid(gab[:, dm:]) * bb
    o_ref[...] = x_ref[...] + jnp.dot(merged.astype(BF16), wo_ref[...], preferred_element_type=F32)


def _merge(xf, y_att, y_dn, gab, w_a, w_b, w_o, tm):
    n, d = xf.shape
    row = lambda i: (i, 0)
    fixed = lambda i: (0, 0)
    return pl.pallas_call(
        _merge_body,
        grid=(n // tm,),
        in_specs=[
            pl.BlockSpec((tm, d), row),
            pl.BlockSpec((tm, ATT_Q_W), row),
            pl.BlockSpec((tm, DN_W), row),
            pl.BlockSpec((tm, 2 * d), row),
            pl.BlockSpec((ATT_Q_W, d), fixed),
            pl.BlockSpec((DN_W, d), fixed),
            pl.BlockSpec((d, d), fixed),
        ],
        out_specs=pl.BlockSpec((tm, d), row),
        out_shape=jax.ShapeDtypeStruct((n, d), F32),
        compiler_params=pltpu.CompilerParams(dimension_semantics=("parallel",), vmem_limit_bytes=VMEM_LIMIT),
    )(xf, y_att, y_dn, gab, w_a.astype(BF16), w_b.astype(BF16), w_o.astype(BF16))


def _top16_rows(s, payload=None):
    rows = lax.broadcasted_iota(I32, s.shape, 0)
    big = s.shape[0]
    vals, pays = [], []
    for _ in range(PEER_TOPK):
        m = jnp.max(s, axis=0, keepdims=True)
        am = jnp.min(jnp.where(s == m, rows, big), axis=0, keepdims=True)
        hit = rows == am
        vals.append(m)
        if payload is None:
            pays.append(am)
        else:
            pays.append(jnp.sum(jnp.where(hit, payload, 0), axis=0, keepdims=True))
        s = jnp.where(hit, -jnp.inf, s)
    return jnp.concatenate(vals, axis=0), jnp.concatenate(pays, axis=0)


def _pair_candidates(s0, i0, s1, i1):
    k = PEER_TOPK
    rows8 = lax.broadcasted_iota(I32, (SUBLANES, s0.shape[1]), 0)
    cs = [s0[0:1] + s1]
    ce = [i0[0:1] * PEER_N_KEYS + i1]
    for i in range(1, SUBLANES):
        valid = rows8 < k // (i + 1)
        cs.append(jnp.where(valid, s0[i:i + 1] + s1[0:SUBLANES], -jnp.inf))
        ce.append(i0[i:i + 1] * PEER_N_KEYS + i1[0:SUBLANES])
    cs.append(s0[SUBLANES:k] + s1[0:1])
    ce.append(i0[SUBLANES:k] * PEER_N_KEYS + i1[0:1])
    return jnp.concatenate(cs, axis=0), jnp.concatenate(ce, axis=0)


def _peer_route_body(x_ref, g2_ref, wq_ref, sk_ref, ids_ref, gates_ref, *, tm):
    x = x_ref[...]
    ms = jnp.mean(x * x, axis=-1, keepdims=True)
    h = (x * lax.rsqrt(ms + EPS) * g2_ref[...]).astype(BF16)
    q = jnp.dot(h, wq_ref[...], preferred_element_type=F32).astype(BF16)
    half = PEER_KEY_DIM // 2
    for hd in range(PEER_HEADS):
        tops = []
        for p in range(2):
            c0 = hd * PEER_KEY_DIM + p * half
            st = lax.dot_general(sk_ref[2 * hd + p], q[:, c0:c0 + half], NT_DIMS,
                                 preferred_element_type=F32)
            tops.append(_top16_rows(st))
        (s0, i0), (s1, i1) = tops
        cand_s, cand_e = _pair_candidates(s0, i0, s1, i1)
        best, expert = _top16_rows(cand_s, cand_e)
        e = jnp.exp(best - best[0:1])
        gate = e / jnp.sum(e, axis=0, keepdims=True)
        rs = slice(hd * PEER_TOPK, (hd + 1) * PEER_TOPK)
        ids_ref[rs, :] = expert
        gates_ref[rs, :] = gate


def _peer_route(x1, g2, w_query, sub_keys, tm):
    n, d = x1.shape
    nsel = PEER_HEADS * PEER_TOPK
    half = PEER_KEY_DIM // 2
    sk = sub_keys.reshape(PEER_HEADS * 2, PEER_N_KEYS, half).astype(BF16)
    return pl.pallas_call(
        functools.partial(_peer_route_body, tm=tm),
        grid=(n // tm,),
        in_specs=[
            pl.BlockSpec((tm, d), lambda i: (i, 0)),
            pl.BlockSpec((1, d), lambda i: (0, 0)),
            pl.BlockSpec((d, PEER_HEADS * PEER_KEY_DIM), lambda i: (0, 0)),
            pl.BlockSpec((PEER_HEADS * 2, PEER_N_KEYS, half), lambda i: (0, 0, 0)),
        ],
        out_specs=(pl.BlockSpec((nsel, tm), lambda i: (0, i)),
                   pl.BlockSpec((nsel, tm), lambda i: (0, i))),
        out_shape=(jax.ShapeDtypeStruct((nsel, n), I32), jax.ShapeDtypeStruct((nsel, n), F32)),
        compiler_params=pltpu.CompilerParams(dimension_semantics=("parallel",), vmem_limit_bytes=VMEM_LIMIT),
    )(x1, g2.reshape(1, d), w_query.astype(BF16), sk)


PEER_SLOTS = 8


def _peer_apply_body(ids_ref, x_ref, g2_ref, gates_ref, uv_hbm, o_ref, buf, coef_ref, sems, *, tb):
    nsel = PEER_HEADS * PEER_TOPK
    nchunk = x_ref.shape[1]
    dm = nchunk * LANES

    ngrp = nsel // SUBLANES

    def issue(tok, slot, k0, k1):
        for kk in range(k0, k1):
            pltpu.make_async_copy(uv_hbm.at[ids_ref[tok, kk]], buf.at[slot, kk], sems.at[slot]).start(
                priority=kk % 2)

    def wait_all(slot):
        pltpu.make_async_copy(uv_hbm.at[pl.ds(0, nsel)], buf.at[slot], sems.at[slot]).wait()

    eye = lax.broadcasted_iota(I32, (nsel, nsel), 0) == lax.broadcasted_iota(I32, (nsel, nsel), 1)
    sub = lax.broadcasted_iota(I32, (SUBLANES, LANES), 0)
    masks = {k: (sub & k) == 0 for k in (4, 2, 1)}
    g2 = g2_ref[...]

    def merge(x, y, k):
        if k == 4:
            return jnp.where(masks[k], x, y) + pltpu.roll(jnp.where(masks[k], y, x), k, axis=0)
        return jnp.where(masks[k], x + pltpu.roll(x, SUBLANES - k, axis=0), y + pltpu.roll(y, k, axis=0))

    order = (0, 4, 2, 6, 1, 5, 3, 7)

    def u_of(word):
        return lax.bitcast_convert_type(word << 16, F32)

    def v_of(word):
        return lax.bitcast_convert_type(word & jnp.uint32(0xFFFF0000), F32)

    def step(t, slot, nxt, nxt_slot):
        def prefetch(k0, k1):
            if nxt is not None:
                issue(nxt, nxt_slot, k0, k1)

        xt = x_ref[t]
        ssq = jnp.sum(jnp.sum(xt * xt, axis=1, keepdims=True), axis=0, keepdims=True)
        h8 = xt * lax.rsqrt(ssq * (1.0 / dm) + EPS) * g2
        grow = gates_ref[t]
        gcol = jnp.sum(jnp.where(eye, jnp.broadcast_to(grow, (nsel, nsel)), 0.0), axis=1, keepdims=True)

        wait_all(slot)
        per_grp = nsel // (2 * ngrp)
        groups = []
        for grp in range(ngrp):
            prefetch(grp * per_grp, (grp + 1) * per_grp)
            p = [u_of(buf[slot, grp * SUBLANES + order.index(j)]) * h8 for j in range(SUBLANES)]
            q4 = [merge(p[2 * i], p[2 * i + 1], 4) for i in range(4)]
            q2 = [merge(q4[2 * i], q4[2 * i + 1], 2) for i in range(2)]
            groups.append(merge(q2[0], q2[1], 1))
        colsum = jnp.concatenate(groups, axis=0)
        act = jnp.sum(colsum, axis=1, keepdims=True)
        gelu = 0.5 * act * (1.0 + lax.erf(act * (2.0 ** -0.5)))
        coef_ref[...] = jnp.broadcast_to(gcol * gelu, (nsel, LANES))
        acc = jnp.zeros((nchunk, LANES), F32)
        for kk in range(nsel):
            if kk % 2 == 0:
                prefetch(nsel // 2 + kk // 2, nsel // 2 + kk // 2 + 1)
            acc = acc + coef_ref[kk:kk + 1, :] * v_of(buf[slot, kk])
        o_ref[t] = xt + acc

    ahead = PEER_SLOTS - 1
    for s in range(ahead):
        issue(s, s, 0, nsel)

    def group_body(g, carry):
        t0 = g * PEER_SLOTS
        for s in range(PEER_SLOTS):
            step(t0 + s, s, t0 + s + ahead, (s + ahead) % PEER_SLOTS)
        return carry

    lax.fori_loop(0, tb // PEER_SLOTS - 1, group_body, 0)
    t0 = tb - PEER_SLOTS
    step(t0, 0, t0 + ahead, ahead % PEER_SLOTS)
    for s in range(1, PEER_SLOTS):
        step(t0 + s, s, None, None)


def _peer_apply(x1, g2, ids, gates, peer_u, peer_v, tb):
    n, d = x1.shape
    nsel = PEER_HEADS * PEER_TOPK
    ne = peer_u.shape[0]
    nchunk = d // LANES

    def half_words(tab):
        return lax.bitcast_convert_type(tab.astype(BF16), jnp.uint16).astype(jnp.uint32)

    uv = ((half_words(peer_v) << 16) | half_words(peer_u)).reshape(ne, nchunk, LANES)
    out = pl.pallas_call(
        functools.partial(_peer_apply_body, tb=tb),
        grid=(n // tb,),
        in_specs=[
            pl.BlockSpec((tb, nsel), lambda i: (i, 0), memory_space=pltpu.SMEM),
            pl.BlockSpec((tb, nchunk, LANES), lambda i: (i, 0, 0)),
            pl.BlockSpec((nchunk, LANES), lambda i: (0, 0)),
            pl.BlockSpec((tb, 1, nsel), lambda i: (i, 0, 0)),
            pl.BlockSpec(memory_space=pl.ANY),
        ],
        out_specs=pl.BlockSpec((tb, nchunk, LANES), lambda i: (i, 0, 0)),
        out_shape=jax.ShapeDtypeStruct((n, nchunk, LANES), F32),
        scratch_shapes=[
            pltpu.VMEM((PEER_SLOTS, nsel, nchunk, LANES), jnp.uint32),
            pltpu.VMEM((nsel, LANES), F32),
            pltpu.SemaphoreType.DMA((PEER_SLOTS,)),
        ],
        compiler_params=pltpu.CompilerParams(dimension_semantics=("arbitrary",), vmem_limit_bytes=VMEM_LIMIT),
    )(ids, x1.reshape(n, nchunk, LANES), g2.reshape(nchunk, LANES), gates.reshape(n, 1, nsel), uv)
    return out.reshape(n, d)


def _block_sizes(t):
    return dict(
        tm_proj=256,
        tq=min(256, t),
        tk=min(512, t),
        tb_dn=min(256, t),
        tm_merge=512,
        tm_route=LANES,
        tb_peer=128,
    )


def kernel(x, norm1_gain, w_in, q_norm_gain, k_norm_gain, dn_conv_w, dn_a_log, dn_dt_bias, dn_out_norm_gain,
           w_att_branch, w_dn_branch, w_o, norm2_gain, peer_w_query, peer_sub_keys, peer_u, peer_v):
    b, t, d = x.shape
    n = b * t
    bs = _block_sizes(t)
    xf = x.reshape(n, d)
    for layer in range(w_in.shape[0]):
        (qt, iqt, vt, iwt, dabt, k, ik, dab, dqkv, dz, gab) = _in_proj(
            xf, norm1_gain[layer], w_in[layer], q_norm_gain[layer], k_norm_gain[layer], bs["tm_proj"])
        y_att = _dsa(qt, iqt, iwt, k, vt, ik, b, t, bs["tq"], bs["tk"])
        y_dn = _deltanet(dqkv, dz, dab, dabt, dn_conv_w[layer], dn_a_log[layer], dn_dt_bias[layer],
                         dn_out_norm_gain[layer], b, t, bs["tb_dn"])
        x1 = _merge(xf, y_att, y_dn, gab, w_att_branch[layer], w_dn_branch[layer], w_o[layer], bs["tm_merge"])
        ids_t, gates_t = _peer_route(x1, norm2_gain[layer], peer_w_query[layer], peer_sub_keys[layer],
                                     bs["tm_route"])
        xf = _peer_apply(x1, norm2_gain[layer], ids_t.T, gates_t.T, peer_u[layer], peer_v[layer], bs["tb_peer"])
    return xf.reshape(b, t, d)
```

```python
import functools

import jax
import jax.numpy as jnp
import numpy as np
from jax import lax
from jax.experimental import pallas as pl
from jax.experimental.pallas import tpu as pltpu

F32 = jnp.float32
BF16 = jnp.bfloat16
I32 = jnp.int32

ATT_HEADS = 8
ATT_KV_HEADS = 2
ATT_HEAD_DIM = 64
IDX_HEADS = 8
IDX_HEAD_DIM = 64
TOPK_MAX = 256
DN_HEADS = 4
DN_HEAD_DIM = 128
DN_CONV = 4
DN_CHUNK = 64
PEER_HEADS = 8
PEER_N_KEYS = 128
PEER_KEY_DIM = 256
PEER_TOPK = 16
EPS = 1e-6

ATT_Q_W = ATT_HEADS * ATT_HEAD_DIM
ATT_KV_W = ATT_KV_HEADS * ATT_HEAD_DIM
IDX_Q_W = IDX_HEADS * IDX_HEAD_DIM
DN_W = DN_HEADS * DN_HEAD_DIM

LANES = 128
SUBLANES = 8
VMEM_LIMIT = 56 * 1024 * 1024

NEG_INF_KEY = int(np.int32(np.uint32(0xFF800000) ^ np.uint32(0x7FFFFFFF)))
INT_MAX = int(np.iinfo(np.int32).max)

NT_DIMS = (((1,), (1,)), ((), ()))
TN_DIMS = (((0,), (0,)), ((), ()))


def _sigmoid(x):
    return 1.0 / (1.0 + jnp.exp(-x))


def _softplus(x):
    return jnp.maximum(x, 0.0) + jnp.log(1.0 + jnp.exp(-jnp.abs(x)))


def _dot(a, b):
    return jnp.dot(a.astype(BF16), b.astype(BF16), preferred_element_type=F32)


def _dot_nt(a, b):
    return lax.dot_general(a.astype(BF16), b.astype(BF16), NT_DIMS, preferred_element_type=F32)


def _split2(x):
    hi = x.astype(BF16)
    lo = (x - hi.astype(F32)).astype(BF16)
    return hi, lo


def _dot_hi(a, b):
    a1, a2 = _split2(a)
    b1, b2 = _split2(b)
    out = jnp.dot(a1, b1, preferred_element_type=F32)
    out = out + jnp.dot(a1, b2, preferred_element_type=F32)
    out = out + jnp.dot(a2, b1, preferred_element_type=F32)
    return out


def _dot_exact_rhs(a, b_exact):
    a1 = a.astype(BF16)
    r1 = a - a1.astype(F32)
    a2 = r1.astype(BF16)
    a3 = (r1 - a2.astype(F32)).astype(BF16)
    out = jnp.dot(a1, b_exact, preferred_element_type=F32)
    out = out + jnp.dot(a2, b_exact, preferred_element_type=F32)
    out = out + jnp.dot(a3, b_exact, preferred_element_type=F32)
    return out


def _dot_exact_lhs(a_exact, b):
    b1 = b.astype(BF16)
    r1 = b - b1.astype(F32)
    b2 = r1.astype(BF16)
    b3 = (r1 - b2.astype(F32)).astype(BF16)
    out = jnp.dot(a_exact, b1, preferred_element_type=F32)
    out = out + jnp.dot(a_exact, b2, preferred_element_type=F32)
    out = out + jnp.dot(a_exact, b3, preferred_element_type=F32)
    return out


C_AK = 0
C_SM = C_AK + ATT_KV_W
C_DQKV = C_SM + LANES
C_DZ = C_DQKV + 3 * DN_W
C_GAB = C_DZ + DN_W
SM_DAB = IDX_HEAD_DIM
R_AQ = 0
R_IQ = R_AQ + ATT_Q_W
R_AV = R_IQ + IDX_Q_W
R_IW = R_AV + ATT_KV_W
R_DAB = R_IW + IDX_HEADS
R_END = R_DAB + 2 * DN_HEADS


def _in_proj_body(x_ref, g1_ref, w_ref, wt_ref, qg_ref, kg_ref,
                  qt_ref, iqt_ref, vt_ref, iwt_ref, dabt_ref, k_ref, ik_ref, dab_ref, dqkv_ref, dz_ref, gab_ref):
    x = x_ref[...]
    ms = jnp.mean(x * x, axis=-1, keepdims=True)
    h = (x * lax.rsqrt(ms + EPS) * g1_ref[...]).astype(BF16)

    rt = lax.dot_general(wt_ref[...], h, NT_DIMS, preferred_element_type=F32)
    qg = qg_ref[...] * (ATT_HEAD_DIM ** -0.5)
    for hd in range(ATT_HEADS):
        sl = slice(R_AQ + hd * ATT_HEAD_DIM, R_AQ + (hd + 1) * ATT_HEAD_DIM)
        blk = rt[sl]
        qt_ref[sl, :] = (blk * lax.rsqrt(jnp.mean(blk * blk, axis=0, keepdims=True) + EPS) * qg).astype(qt_ref.dtype)
    iqt_ref[...] = rt[R_IQ:R_AV].astype(iqt_ref.dtype)
    vt_ref[...] = rt[R_AV:R_IW].astype(vt_ref.dtype)
    iwt_ref[...] = rt[R_IW:R_DAB]
    dabt_ref[...] = rt[R_DAB:R_END]

    def proj(c0, c1):
        return jnp.dot(h, w_ref[:, c0:c1], preferred_element_type=F32)

    ak = proj(C_AK, C_SM)
    kg = kg_ref[...]
    for hd in range(ATT_KV_HEADS):
        sl = slice(hd * ATT_HEAD_DIM, (hd + 1) * ATT_HEAD_DIM)
        blk = ak[:, sl]
        k_ref[:, sl] = (blk * lax.rsqrt(jnp.mean(blk * blk, axis=-1, keepdims=True) + EPS) * kg).astype(k_ref.dtype)
    sm = proj(C_SM, C_DQKV)
    ik_ref[...] = sm[:, :IDX_HEAD_DIM].astype(ik_ref.dtype)
    dab_ref[...] = sm[:, SM_DAB:SM_DAB + 2 * DN_HEADS]
    dqkv_ref[...] = proj(C_DQKV, C_DZ)
    dz_ref[...] = proj(C_DZ, C_GAB)
    gab_ref[...] = proj(C_GAB, C_GAB + 2 * x.shape[1])


def _in_proj(xf, g1, w_in, q_gain, k_gain, tm):
    n, d = xf.shape
    cuts = np.cumsum([ATT_Q_W, ATT_KV_W, ATT_KV_W, IDX_Q_W, IDX_HEAD_DIM, IDX_HEADS,
                      DN_W, DN_W, DN_W, DN_W, DN_HEADS, DN_HEADS, d, d])[:-1].tolist()
    aq, ak, av, iq, ik, iw, dq, dk, dv, dz, da, db, ga, gb = jnp.split(w_in, cuts, axis=-1)
    pad = jnp.zeros((d, LANES - IDX_HEAD_DIM - 2 * DN_HEADS), w_in.dtype)
    w_all = jnp.concatenate([ak, ik, da, db, pad, dq, dk, dv, dz, ga, gb], axis=-1).astype(BF16)
    wt_all = jnp.concatenate([aq, iq, av, iw, da, db], axis=-1).T.astype(BF16)
    wtot = w_all.shape[1]
    row = lambda i: (i, 0)
    col = lambda i: (0, i)
    fixed = lambda i: (0, 0)
    out_shapes = (
        jax.ShapeDtypeStruct((ATT_Q_W, n), BF16),
        jax.ShapeDtypeStruct((IDX_Q_W, n), BF16),
        jax.ShapeDtypeStruct((ATT_KV_W, n), BF16),
        jax.ShapeDtypeStruct((IDX_HEADS, n), F32),
        jax.ShapeDtypeStruct((2 * DN_HEADS, n), F32),
        jax.ShapeDtypeStruct((n, ATT_KV_W), BF16),
        jax.ShapeDtypeStruct((n, IDX_HEAD_DIM), BF16),
        jax.ShapeDtypeStruct((n, 2 * DN_HEADS), F32),
        jax.ShapeDtypeStruct((n, 3 * DN_W), F32),
        jax.ShapeDtypeStruct((n, DN_W), F32),
        jax.ShapeDtypeStruct((n, 2 * d), F32),
    )
    out_specs = tuple(
        pl.BlockSpec((s.shape[0], tm), col) if s.shape[0] != n else pl.BlockSpec((tm, s.shape[1]), row)
        for s in out_shapes)
    return pl.pallas_call(
        _in_proj_body,
        grid=(n // tm,),
        in_specs=[
            pl.BlockSpec((tm, d), row),
            pl.BlockSpec((1, d), fixed),
            pl.BlockSpec((d, wtot), fixed),
            pl.BlockSpec((R_END, d), fixed),
            pl.BlockSpec((ATT_HEAD_DIM, 1), fixed),
            pl.BlockSpec((1, ATT_HEAD_DIM), fixed),
        ],
        out_specs=out_specs,
        out_shape=out_shapes,
        compiler_params=pltpu.CompilerParams(dimension_semantics=("parallel",), vmem_limit_bytes=VMEM_LIMIT),
    )(xf, g1.reshape(1, d), w_all, wt_all, q_gain.reshape(-1, 1), k_gain.reshape(1, -1))


FOLD_ROWS = 4 * SUBLANES
DSA_AHEAD = 2


def _dsa_body(qt_ref, iqt_ref, iwt_ref, k_ref, vt_ref, ik_ref, o_ref,
              key_ref, bias_ref, *acc_refs, tq, tk, ksel):
    i = pl.program_id(1)
    q0 = i * tq
    n_kb = (q0 + tq + tk - 1) // tk
    qpos = q0 + lax.broadcasted_iota(I32, (1, tq), 1)
    iwt = iwt_ref[...]

    def score_body(kb, carry):
        k0 = pl.multiple_of(kb * tk, tk)
        ikb = ik_ref[pl.ds(k0, tk), :]
        acc = jnp.zeros((tk, tq), F32)
        for hd in range(IDX_HEADS):
            sl = slice(hd * IDX_HEAD_DIM, (hd + 1) * IDX_HEAD_DIM)
            dots = jnp.dot(ikb, iqt_ref[sl, :], preferred_element_type=F32)
            acc = acc + iwt[hd:hd + 1, :] * jnp.maximum(dots, 0.0)
        bits = lax.bitcast_convert_type(acc, I32)
        keys = jnp.where(bits >= 0, bits, bits ^ INT_MAX)
        kpos = k0 + lax.broadcasted_iota(I32, (tk, 1), 0)
        key_ref[kb] = jnp.where(kpos <= qpos, keys, NEG_INF_KEY)
        return carry

    lax.fori_loop(0, n_kb, score_body, 0)

    def fold(x, op):
        return op(x.reshape(tk // FOLD_ROWS, FOLD_ROWS, tq), axis=0)

    def count_ge(thr):
        def body(kb, c):
            return c + fold(jnp.where(key_ref[kb] >= thr, 1.0, 0.0), jnp.sum)
        c = lax.fori_loop(0, n_kb, body, jnp.zeros((FOLD_ROWS, tq), F32))
        return jnp.sum(c, axis=0, keepdims=True)

    def bisect_cond(st):
        it, _, _, _, _, pending = st
        return jnp.logical_and(it < 32, jnp.max(pending) > 0.0)

    def bisect_body(st):
        it, lo, hi, c_lo, c_hi, pending = st
        mid = (lo >> 1) + (hi >> 1) + (lo & hi & 1)
        c = count_ge(mid)
        live = pending > 0.0
        up = jnp.logical_and(live, c >= ksel)
        down = jnp.logical_and(live, c < ksel)
        c_lo = jnp.where(up, c, c_lo)
        return (it + 1, jnp.where(up, mid, lo), jnp.where(down, mid, hi), c_lo, jnp.where(down, c, c_hi),
                jnp.where(c_lo == ksel, 0.0, pending))

    lo0 = jnp.full((1, tq), NEG_INF_KEY + 1, I32)
    hi0 = jnp.full((1, tq), INT_MAX, I32)
    zero = jnp.zeros((1, tq), F32)
    pending0 = jnp.where(qpos + 1 > ksel, 1.0, 0.0)
    _, thr, _, c_thr, c_above, _ = lax.while_loop(bisect_cond, bisect_body, (0, lo0, hi0, zero, zero, pending0))

    need = ksel - c_above
    has_excess = jnp.max(jnp.where(c_thr > ksel, 1.0, 0.0)) > 0.0

    @pl.when(has_excess)
    def _():
        lower = jnp.where(lax.broadcasted_iota(I32, (tk, tk), 0) >= lax.broadcasted_iota(I32, (tk, tk), 1),
                          1.0, 0.0).astype(BF16)

        def body(kb, seen):
            keys = key_ref[kb]
            tie = keys == thr
            tie_f = jnp.where(tie, 1.0, 0.0)
            prefix = jnp.dot(lower, tie_f.astype(BF16), preferred_element_type=F32) + seen
            drop = jnp.logical_and(tie, prefix > need)
            key_ref[kb] = jnp.where(drop, NEG_INF_KEY, keys)
            return seen + jnp.sum(tie_f, axis=0, keepdims=True)

        lax.fori_loop(0, n_kb, body, jnp.zeros((1, tq), F32))

    for acc_ref in acc_refs:
        acc_ref[...] = jnp.zeros(acc_ref.shape, F32)
    grp = ATT_HEADS // ATT_KV_HEADS

    def att_body(kb, carry):
        m_all, l_all = carry
        k0 = pl.multiple_of(kb * tk, tk)
        kblk = k_ref[pl.ds(k0, tk), :]
        vtb = vt_ref[kb]
        bias_ref[...] = jnp.where(key_ref[kb] >= thr, 0.0, -1e30)
        m_rows, l_rows = [], []

        def logits(hd):
            g = hd // grp
            return jnp.dot(kblk[:, g * ATT_HEAD_DIM:(g + 1) * ATT_HEAD_DIM],
                           qt_ref[hd * ATT_HEAD_DIM:(hd + 1) * ATT_HEAD_DIM, :],
                           preferred_element_type=F32) + bias_ref[...]

        def accumulate(hd, alpha, p):
            g = hd // grp
            acc_ref = acc_refs[hd]
            acc_ref[...] = alpha * acc_ref[...] + jnp.dot(vtb[g * ATT_HEAD_DIM:(g + 1) * ATT_HEAD_DIM, :], p,
                                                          preferred_element_type=F32)

        queue = [logits(hd) for hd in range(DSA_AHEAD)]
        pending = None
        for hd in range(ATT_HEADS):
            s = queue.pop(0)
            if hd + DSA_AHEAD < ATT_HEADS:
                queue.append(logits(hd + DSA_AHEAD))
            m_old = m_all[hd:hd + 1, :]
            m_new = jnp.maximum(m_old, jnp.max(fold(s, jnp.max), axis=0, keepdims=True))
            p = jnp.exp(s - m_new)
            alpha = jnp.exp(m_old - m_new)
            l_rows.append(alpha * l_all[hd:hd + 1, :] + jnp.sum(fold(p, jnp.sum), axis=0, keepdims=True))
            m_rows.append(m_new)
            if pending is not None:
                accumulate(*pending)
            pending = (hd, alpha, p.astype(BF16))
        accumulate(*pending)
        return jnp.concatenate(m_rows, axis=0), jnp.concatenate(l_rows, axis=0)

    m0 = jnp.full((ATT_HEADS, tq), -1e30, F32)
    _, l_fin = lax.fori_loop(0, n_kb, att_body, (m0, jnp.zeros((ATT_HEADS, tq), F32)))
    for pair in range(ATT_HEADS // 2):
        rows = [acc_refs[hd][...] / l_fin[hd:hd + 1, :] for hd in (2 * pair, 2 * pair + 1)]
        o_ref[:, pair * LANES:(pair + 1) * LANES] = jnp.concatenate(rows, axis=0).T.astype(o_ref.dtype)


def _dsa(qt, iqt, iwt, k, vt, ik, b, t, tq, tk):
    n = b * t
    nq = t // tq
    nkb = t // tk
    ksel = min(TOPK_MAX, t // 4)
    qcol = lambda bi, i: (0, bi * nq + i)
    brow = lambda bi, i: (bi, 0)
    vtb = vt.reshape(ATT_KV_W, n // tk, tk).transpose(1, 0, 2)
    return pl.pallas_call(
        functools.partial(_dsa_body, tq=tq, tk=tk, ksel=ksel),
        grid=(b, nq),
        in_specs=[
            pl.BlockSpec((ATT_Q_W, tq), qcol),
            pl.BlockSpec((IDX_Q_W, tq), qcol),
            pl.BlockSpec((IDX_HEADS, tq), qcol),
            pl.BlockSpec((t, ATT_KV_W), brow),
            pl.BlockSpec((nkb, ATT_KV_W, tk), lambda bi, i: (bi, 0, 0)),
            pl.BlockSpec((t, IDX_HEAD_DIM), brow),
        ],
        out_specs=pl.BlockSpec((tq, ATT_Q_W), lambda bi, i: (bi * nq + i, 0)),
        out_shape=jax.ShapeDtypeStruct((n, ATT_Q_W), BF16),
        scratch_shapes=[
            pltpu.VMEM((nkb, tk, tq), I32),
            pltpu.VMEM((tk, tq), F32),
        ] + [pltpu.VMEM((ATT_HEAD_DIM, tq), F32) for _ in range(ATT_HEADS)],
        compiler_params=pltpu.CompilerParams(dimension_semantics=("parallel", "arbitrary"),
                                             vmem_limit_bytes=VMEM_LIMIT),
    )(qt, iqt, iwt, k, vtb, ik)


def _deltanet_body(x_ref, dz_ref, dab_ref, dabt_ref, cw_ref, alog_r_ref, bias_r_ref, alog_c_ref, bias_c_ref,
                   og_ref, y_ref, carry_ref, state_ref, *, tb):
    c = DN_CHUNK
    d = DN_HEAD_DIM
    nh = DN_HEADS

    @pl.when(pl.program_id(1) == 0)
    def _():
        carry_ref[...] = jnp.zeros(carry_ref.shape, F32)
        state_ref[...] = jnp.zeros(state_ref.shape, F32)

    xb = x_ref[...]
    xx = jnp.concatenate([carry_ref[...], xb], axis=0)
    cw = cw_ref[...]
    off = SUBLANES - (DN_CONV - 1)
    conv = cw[0:1] * xx[off:off + tb]
    for j in range(1, DN_CONV):
        conv = conv + cw[j:j + 1] * xx[off + j:off + j + tb]
    carry_ref[...] = xb[tb - SUBLANES:tb]
    qkv = conv * _sigmoid(conv)

    dab = dab_ref[...]
    g_col = -jnp.exp(alog_r_ref[...]) * _softplus(dab[:, 0:nh] + bias_r_ref[...])
    beta_col = _sigmoid(dab[:, nh:2 * nh])
    g_row = -jnp.exp(alog_c_ref[...]) * _softplus(dabt_ref[0:nh, :] + bias_c_ref[...])

    ri = lax.broadcasted_iota(I32, (tb, tb), 0)
    ci = lax.broadcasted_iota(I32, (tb, tb), 1)
    same_chunk = (ri // c) == (ci // c)
    lower_blk = jnp.where(jnp.logical_and(same_chunk, ri >= ci), 1.0, 0.0).astype(BF16)
    upper_blk = jnp.where(jnp.logical_and(same_chunk, ri <= ci), 1.0, 0.0).astype(BF16)
    gc_col = _dot_exact_lhs(lower_blk, g_col)
    gc_row = _dot_exact_rhs(g_row, upper_blk)

    r64 = lax.broadcasted_iota(I32, (c, c), 0)
    c64 = lax.broadcasted_iota(I32, (c, c), 1)
    tri = r64 >= c64
    strict = r64 > c64
    eye = jnp.where(r64 == c64, 1.0, 0.0)
    og = og_ref[...]

    inst = [(ch, hd) for ch in range(tb // c) for hd in range(nh)]
    rows = lambda ch: slice(ch * c, (ch + 1) * c)
    qs, ks, vs, bcols, gcols, decays, kbetas = [], [], [], [], [], [], []
    for ch, hd in inst:
        rs = rows(ch)
        qc = qkv[rs, hd * d:(hd + 1) * d]
        kc = qkv[rs, nh * d + hd * d:nh * d + (hd + 1) * d]
        qs.append(qc * lax.rsqrt(jnp.sum(qc * qc, axis=-1, keepdims=True) + EPS) * (d ** -0.5))
        ks.append(kc * lax.rsqrt(jnp.sum(kc * kc, axis=-1, keepdims=True) + EPS))
        vs.append(qkv[rs, 2 * nh * d + hd * d:2 * nh * d + (hd + 1) * d])
        bcols.append(beta_col[rs, hd:hd + 1])
        gcols.append(gc_col[rs, hd:hd + 1])
        decays.append(jnp.exp(jnp.where(tri, gcols[-1] - gc_row[hd:hd + 1, rs], -jnp.inf)))
        kbetas.append(ks[-1] * bcols[-1])
    kk = [_dot_nt(kbetas[i], ks[i]) for i in range(len(inst))]
    qk = [_dot_nt(qs[i], ks[i]) for i in range(len(inst))]
    a_mats = [jnp.where(strict, kk[i] * decays[i], 0.0) for i in range(len(inst))]
    intras = [jnp.where(tri, qk[i] * decays[i], 0.0) for i in range(len(inst))]
    xs = [eye - a for a in a_mats]
    ps = a_mats
    for _ in range(5):
        ps = [_dot_hi(p, p) for p in ps]
        xs = [x + _dot_hi(x, p) for x, p in zip(xs, ps)]
    egs = [jnp.exp(g) for g in gcols]
    sols = [_dot(xs[i], jnp.concatenate([vs[i] * bcols[i], kbetas[i] * egs[i]], axis=1)) for i in range(len(inst))]
    qes = [qs[i] * egs[i] for i in range(len(inst))]
    glasts = [g[c - 1:c] for g in gcols]
    kdecs = [ks[i] * jnp.exp(glasts[i] - gcols[i]) for i in range(len(inst))]

    states = [state_ref[hd] for hd in range(nh)]
    for ch in range(tb // c):
        ii = [ch * nh + hd for hd in range(nh)]
        both = [_dot(jnp.concatenate([sols[i][:, d:], qes[i]], axis=0), states[hd]) for hd, i in enumerate(ii)]
        v_new = [sols[i][:, :d] - both[hd][:c] for hd, i in enumerate(ii)]
        outs = [both[hd][c:] + _dot(intras[i], v_new[hd]) for hd, i in enumerate(ii)]
        states = [states[hd] * jnp.exp(glasts[i]) + lax.dot_general(
            kdecs[i].astype(BF16), v_new[hd].astype(BF16), TN_DIMS, preferred_element_type=F32)
            for hd, i in enumerate(ii)]
        for hd in range(nh):
            ls = slice(hd * d, (hd + 1) * d)
            o = outs[hd]
            on = o * lax.rsqrt(jnp.mean(o * o, axis=-1, keepdims=True) + EPS) * og
            z = dz_ref[rows(ch), ls]
            y_ref[rows(ch), ls] = (on * (z * _sigmoid(z))).astype(y_ref.dtype)
    for hd in range(nh):
        state_ref[hd] = states[hd]


def _deltanet(dqkv, dz, dab, dabt, conv_w, a_log, dt_bias, out_gain, b, t, tb):
    n = b * t
    nb = t // tb
    row = lambda bi, j: (bi * nb + j, 0)
    col = lambda bi, j: (0, bi * nb + j)
    fixed = lambda bi, j: (0, 0)
    nh = DN_HEADS
    return pl.pallas_call(
        functools.partial(_deltanet_body, tb=tb),
        grid=(b, nb),
        in_specs=[
            pl.BlockSpec((tb, 3 * DN_W), row),
            pl.BlockSpec((tb, DN_W), row),
            pl.BlockSpec((tb, 2 * nh), row),
            pl.BlockSpec((2 * nh, tb), col),
            pl.BlockSpec((DN_CONV, 3 * DN_W), fixed),
            pl.BlockSpec((1, nh), fixed),
            pl.BlockSpec((1, nh), fixed),
            pl.BlockSpec((nh, 1), fixed),
            pl.BlockSpec((nh, 1), fixed),
            pl.BlockSpec((1, DN_HEAD_DIM), fixed),
        ],
        out_specs=pl.BlockSpec((tb, DN_W), row),
        out_shape=jax.ShapeDtypeStruct((n, DN_W), BF16),
        scratch_shapes=[
            pltpu.VMEM((SUBLANES, 3 * DN_W), F32),
            pltpu.VMEM((nh, DN_HEAD_DIM, DN_HEAD_DIM), F32),
        ],
        compiler_params=pltpu.CompilerParams(dimension_semantics=("parallel", "arbitrary"),
                                             vmem_limit_bytes=VMEM_LIMIT),
    )(dqkv, dz, dab, dabt, conv_w, a_log.reshape(1, nh), dt_bias.reshape(1, nh),
      a_log.reshape(nh, 1), dt_bias.reshape(nh, 1), out_gain.reshape(1, -1))


def _merge_body(x_ref, ya_ref, yd_ref, gab_ref, wa_ref, wb_ref, wo_ref, o_ref):
    dm = x_ref.shape[1]
    gab = gab_ref[...]
    a = jnp.dot(ya_ref[...], wa_ref[...], preferred_element_type=F32)
    bb = jnp.dot(yd_ref[...], wb_ref[...], preferred_element_type=F32)
    merged = _sigmoid(gab[:, :dm]) * a + _sigmoid(gab[:, dm:]) * bb
    o_ref[...] = x_ref[...] + jnp.dot(merged.astype(BF16), wo_ref[...], preferred_element_type=F32)


def _merge(xf, y_att, y_dn, gab, w_a, w_b, w_o, tm):
    n, d = xf.shape
    row = lambda i: (i, 0)
    fixed = lambda i: (0, 0)
    return pl.pallas_call(
        _merge_body,
        grid=(n // tm,),
        in_specs=[
            pl.BlockSpec((tm, d), row),
            pl.BlockSpec((tm, ATT_Q_W), row),
            pl.BlockSpec((tm, DN_W), row),
            pl.BlockSpec((tm, 2 * d), row),
            pl.BlockSpec((ATT_Q_W, d), fixed),
            pl.BlockSpec((DN_W, d), fixed),
            pl.BlockSpec((d, d), fixed),
        ],
        out_specs=pl.BlockSpec((tm, d), row),
        out_shape=jax.ShapeDtypeStruct((n, d), F32),
        compiler_params=pltpu.CompilerParams(dimension_semantics=("parallel",), vmem_limit_bytes=VMEM_LIMIT),
    )(xf, y_att, y_dn, gab, w_a.astype(BF16), w_b.astype(BF16), w_o.astype(BF16))


def _top16_rows(s, payload=None):
    rows = lax.broadcasted_iota(I32, s.shape, 0)
    big = s.shape[0]
    vals, pays = [], []
    for _ in range(PEER_TOPK):
        m = jnp.max(s, axis=0, keepdims=True)
        am = jnp.min(jnp.where(s == m, rows, big), axis=0, keepdims=True)
        hit = rows == am
        vals.append(m)
        if payload is None:
            pays.append(am)
        else:
            pays.append(jnp.sum(jnp.where(hit, payload, 0), axis=0, keepdims=True))
        s = jnp.where(hit, -jnp.inf, s)
    return jnp.concatenate(vals, axis=0), jnp.concatenate(pays, axis=0)


def _pair_candidates(s0, i0, s1, i1):
    k = PEER_TOPK
    rows8 = lax.broadcasted_iota(I32, (SUBLANES, s0.shape[1]), 0)
    cs = [s0[0:1] + s1]
    ce = [i0[0:1] * PEER_N_KEYS + i1]
    for i in range(1, SUBLANES):
        valid = rows8 < k // (i + 1)
        cs.append(jnp.where(valid, s0[i:i + 1] + s1[0:SUBLANES], -jnp.inf))
        ce.append(i0[i:i + 1] * PEER_N_KEYS + i1[0:SUBLANES])
    cs.append(s0[SUBLANES:k] + s1[0:1])
    ce.append(i0[SUBLANES:k] * PEER_N_KEYS + i1[0:1])
    return jnp.concatenate(cs, axis=0), jnp.concatenate(ce, axis=0)


def _peer_route_body(x_ref, g2_ref, wq_ref, sk_ref, ids_ref, gates_ref, *, tm):
    x = x_ref[...]
    ms = jnp.mean(x * x, axis=-1, keepdims=True)
    h = (x * lax.rsqrt(ms + EPS) * g2_ref[...]).astype(BF16)
    q = jnp.dot(h, wq_ref[...], preferred_element_type=F32).astype(BF16)
    half = PEER_KEY_DIM // 2
    for hd in range(PEER_HEADS):
        tops = []
        for p in range(2):
            c0 = hd * PEER_KEY_DIM + p * half
            st = lax.dot_general(sk_ref[2 * hd + p], q[:, c0:c0 + half], NT_DIMS,
                                 preferred_element_type=F32)
            tops.append(_top16_rows(st))
        (s0, i0), (s1, i1) = tops
        cand_s, cand_e = _pair_candidates(s0, i0, s1, i1)
        best, expert = _top16_rows(cand_s, cand_e)
        e = jnp.exp(best - best[0:1])
        gate = e / jnp.sum(e, axis=0, keepdims=True)
        rs = slice(hd * PEER_TOPK, (hd + 1) * PEER_TOPK)
        ids_ref[rs, :] = expert
        gates_ref[rs, :] = gate


def _peer_route(x1, g2, w_query, sub_keys, tm):
    n, d = x1.shape
    nsel = PEER_HEADS * PEER_TOPK
    half = PEER_KEY_DIM // 2
    sk = sub_keys.reshape(PEER_HEADS * 2, PEER_N_KEYS, half).astype(BF16)
    return pl.pallas_call(
        functools.partial(_peer_route_body, tm=tm),
        grid=(n // tm,),
        in_specs=[
            pl.BlockSpec((tm, d), lambda i: (i, 0)),
            pl.BlockSpec((1, d), lambda i: (0, 0)),
            pl.BlockSpec((d, PEER_HEADS * PEER_KEY_DIM), lambda i: (0, 0)),
            pl.BlockSpec((PEER_HEADS * 2, PEER_N_KEYS, half), lambda i: (0, 0, 0)),
        ],
        out_specs=(pl.BlockSpec((nsel, tm), lambda i: (0, i)),
                   pl.BlockSpec((nsel, tm), lambda i: (0, i))),
        out_shape=(jax.ShapeDtypeStruct((nsel, n), I32), jax.ShapeDtypeStruct((nsel, n), F32)),
        compiler_params=pltpu.CompilerParams(dimension_semantics=("parallel",), vmem_limit_bytes=VMEM_LIMIT),
    )(x1, g2.reshape(1, d), w_query.astype(BF16), sk)


PEER_SLOTS = 8


def _peer_apply_body(ids_ref, x_ref, g2_ref, gates_ref, uv_hbm, o_ref, sems, *slot_refs, tb):
    bufs, coefs = slot_refs[:PEER_SLOTS], slot_refs[PEER_SLOTS:]
    nsel = PEER_HEADS * PEER_TOPK
    nchunk = x_ref.shape[1]
    dm = nchunk * LANES

    ngrp = nsel // SUBLANES

    def issue(tok, slot, k0, k1):
        for kk in range(k0, k1):
            pltpu.make_async_copy(uv_hbm.at[ids_ref[tok, kk]], bufs[slot].at[kk], sems.at[slot]).start(
                priority=kk % 2)

    def wait_all(slot):
        pltpu.make_async_copy(uv_hbm.at[pl.ds(0, nsel)], bufs[slot], sems.at[slot]).wait()

    eye = lax.broadcasted_iota(I32, (nsel, nsel), 0) == lax.broadcasted_iota(I32, (nsel, nsel), 1)
    sub = lax.broadcasted_iota(I32, (SUBLANES, LANES), 0)
    masks = {k: (sub & k) == 0 for k in (4, 2, 1)}
    g2 = g2_ref[...]

    def merge(x, y, k):
        if k == 4:
            return jnp.where(masks[k], x, y) + pltpu.roll(jnp.where(masks[k], y, x), k, axis=0)
        return jnp.where(masks[k], x + pltpu.roll(x, SUBLANES - k, axis=0), y + pltpu.roll(y, k, axis=0))

    order = (0, 4, 2, 6, 1, 5, 3, 7)

    def u_of(word):
        return lax.bitcast_convert_type(word << 16, F32)

    def v_of(word):
        return lax.bitcast_convert_type(word & jnp.uint32(0xFFFF0000), F32)

    def normed(tok):
        xt = x_ref[tok]
        ssq = jnp.sum(jnp.sum(xt * xt, axis=1, keepdims=True), axis=0, keepdims=True)
        return xt * lax.rsqrt(ssq * (1.0 / dm) + EPS) * g2

    def step(t, slot, nxt, nxt_slot, h8, t_after):
        def prefetch(k0, k1):
            if nxt is not None:
                issue(nxt, nxt_slot, k0, k1)

        wait_all(slot)
        grow = gates_ref[t]
        gcol = jnp.sum(jnp.where(eye, jnp.broadcast_to(grow, (nsel, nsel)), 0.0), axis=1, keepdims=True)
        per_grp = nsel // (2 * ngrp)
        groups = []
        for grp in range(ngrp):
            prefetch(grp * per_grp, (grp + 1) * per_grp)
            p = [u_of(bufs[slot][grp * SUBLANES + order.index(j)]) * h8 for j in range(SUBLANES)]
            q4 = [merge(p[2 * i], p[2 * i + 1], 4) for i in range(4)]
            q2 = [merge(q4[2 * i], q4[2 * i + 1], 2) for i in range(2)]
            groups.append(merge(q2[0], q2[1], 1))
        colsum = jnp.concatenate(groups, axis=0)
        act = jnp.sum(colsum, axis=1, keepdims=True)
        gelu = 0.5 * act * (1.0 + lax.erf(act * (2.0 ** -0.5)))
        coef_ref = coefs[slot]
        coef_ref[...] = jnp.broadcast_to(gcol * gelu, (nsel, LANES))
        h8_after = normed(t_after)
        acc = jnp.zeros((nchunk, LANES), F32)
        for kk in range(nsel):
            if kk % 2 == 0:
                prefetch(nsel // 2 + kk // 2, nsel // 2 + kk // 2 + 1)
            acc = acc + coef_ref[kk:kk + 1, :] * v_of(bufs[slot][kk])
        o_ref[t] = x_ref[t] + acc
        return h8_after

    ahead = PEER_SLOTS - 1
    for s in range(ahead):
        issue(s, s, 0, nsel)

    def group_body(g, h8):
        t0 = g * PEER_SLOTS
        for s in range(PEER_SLOTS):
            h8 = step(t0 + s, s, t0 + s + ahead, (s + ahead) % PEER_SLOTS, h8, t0 + s + 1)
        return h8

    h8 = lax.fori_loop(0, tb // PEER_SLOTS - 1, group_body, normed(0))
    t0 = tb - PEER_SLOTS
    h8 = step(t0, 0, t0 + ahead, ahead % PEER_SLOTS, h8, t0 + 1)
    for s in range(1, PEER_SLOTS):
        h8 = step(t0 + s, s, None, None, h8, min(t0 + s + 1, tb - 1))


def _peer_apply(x1, g2, ids, gates, peer_u, peer_v, tb):
    n, d = x1.shape
    nsel = PEER_HEADS * PEER_TOPK
    ne = peer_u.shape[0]
    nchunk = d // LANES

    def half_words(tab):
        return lax.bitcast_convert_type(tab.astype(BF16), jnp.uint16).astype(jnp.uint32)

    uv = ((half_words(peer_v) << 16) | half_words(peer_u)).reshape(ne, nchunk, LANES)
    out = pl.pallas_call(
        functools.partial(_peer_apply_body, tb=tb),
        grid=(n // tb,),
        in_specs=[
            pl.BlockSpec((tb, nsel), lambda i: (i, 0), memory_space=pltpu.SMEM),
            pl.BlockSpec((tb, nchunk, LANES), lambda i: (i, 0, 0)),
            pl.BlockSpec((nchunk, LANES), lambda i: (0, 0)),
            pl.BlockSpec((tb, 1, nsel), lambda i: (i, 0, 0)),
            pl.BlockSpec(memory_space=pl.ANY),
        ],
        out_specs=pl.BlockSpec((tb, nchunk, LANES), lambda i: (i, 0, 0)),
        out_shape=jax.ShapeDtypeStruct((n, nchunk, LANES), F32),
        scratch_shapes=[
            pltpu.SemaphoreType.DMA((PEER_SLOTS,)),
        ] + [pltpu.VMEM((nsel, nchunk, LANES), jnp.uint32) for _ in range(PEER_SLOTS)]
        + [pltpu.VMEM((nsel, LANES), F32) for _ in range(PEER_SLOTS)],
        compiler_params=pltpu.CompilerParams(dimension_semantics=("arbitrary",), vmem_limit_bytes=VMEM_LIMIT),
    )(ids, x1.reshape(n, nchunk, LANES), g2.reshape(nchunk, LANES), gates.reshape(n, 1, nsel), uv)
    return out.reshape(n, d)


def _block_sizes(t):
    return dict(
        tm_proj=256,
        tq=min(256, t),
        tk=min(512, t),
        tb_dn=min(256, t),
        tm_merge=512,
        tm_route=LANES,
        tb_peer=128,
    )


def kernel(x, norm1_gain, w_in, q_norm_gain, k_norm_gain, dn_conv_w, dn_a_log, dn_dt_bias, dn_out_norm_gain,
           w_att_branch, w_dn_branch, w_o, norm2_gain, peer_w_query, peer_sub_keys, peer_u, peer_v):
    b, t, d = x.shape
    n = b * t
    bs = _block_sizes(t)
    xf = x.reshape(n, d)
    for layer in range(w_in.shape[0]):
        (qt, iqt, vt, iwt, dabt, k, ik, dab, dqkv, dz, gab) = _in_proj(
            xf, norm1_gain[layer], w_in[layer], q_norm_gain[layer], k_norm_gain[layer], bs["tm_proj"])
        y_att = _dsa(qt, iqt, iwt, k, vt, ik, b, t, bs["tq"], bs["tk"])
        y_dn = _deltanet(dqkv, dz, dab, dabt, dn_conv_w[layer], dn_a_log[layer], dn_dt_bias[layer],
                         dn_out_norm_gain[layer], b, t, bs["tb_dn"])
        x1 = _merge(xf, y_att, y_dn, gab, w_att_branch[layer], w_dn_branch[layer], w_o[layer], bs["tm_merge"])
        ids_t, gates_t = _peer_route(x1, norm2_gain[layer], peer_w_query[layer], peer_sub_keys[layer],
                                     bs["tm_route"])
        xf = _peer_apply(x1, norm2_gain[layer], ids_t.T, gates_t.T, peer_u[layer], peer_v[layer], bs["tb_peer"])
    return xf.reshape(b, t, d)
```

```python
import functools

import jax
import jax.numpy as jnp
import numpy as np
from jax import lax
from jax.experimental import pallas as pl
from jax.experimental.pallas import tpu as pltpu

F32 = jnp.float32
BF16 = jnp.bfloat16
I32 = jnp.int32

ATT_HEADS = 8
ATT_KV_HEADS = 2
ATT_HEAD_DIM = 64
IDX_HEADS = 8
IDX_HEAD_DIM = 64
TOPK_MAX = 256
DN_HEADS = 4
DN_HEAD_DIM = 128
DN_CONV = 4
DN_CHUNK = 64
PEER_HEADS = 8
PEER_N_KEYS = 128
PEER_KEY_DIM = 256
PEER_TOPK = 16
EPS = 1e-6

ATT_Q_W = ATT_HEADS * ATT_HEAD_DIM
ATT_KV_W = ATT_KV_HEADS * ATT_HEAD_DIM
IDX_Q_W = IDX_HEADS * IDX_HEAD_DIM
DN_W = DN_HEADS * DN_HEAD_DIM

LANES = 128
SUBLANES = 8
VMEM_LIMIT = 56 * 1024 * 1024

NEG_INF_KEY = int(np.int32(np.uint32(0xFF800000) ^ np.uint32(0x7FFFFFFF)))
INT_MAX = int(np.iinfo(np.int32).max)

NT_DIMS = (((1,), (1,)), ((), ()))
TN_DIMS = (((0,), (0,)), ((), ()))


def _sigmoid(x):
    return 1.0 / (1.0 + jnp.exp(-x))


def _softplus(x):
    return jnp.maximum(x, 0.0) + jnp.log(1.0 + jnp.exp(-jnp.abs(x)))


def _dot(a, b):
    return jnp.dot(a.astype(BF16), b.astype(BF16), preferred_element_type=F32)


def _dot_nt(a, b):
    return lax.dot_general(a.astype(BF16), b.astype(BF16), NT_DIMS, preferred_element_type=F32)


def _split2(x):
    hi = x.astype(BF16)
    lo = (x - hi.astype(F32)).astype(BF16)
    return hi, lo


def _dot_hi(a, b):
    a1, a2 = _split2(a)
    b1, b2 = _split2(b)
    out = jnp.dot(a1, b1, preferred_element_type=F32)
    out = out + jnp.dot(a1, b2, preferred_element_type=F32)
    out = out + jnp.dot(a2, b1, preferred_element_type=F32)
    return out


def _dot_exact_rhs(a, b_exact):
    a1 = a.astype(BF16)
    r1 = a - a1.astype(F32)
    a2 = r1.astype(BF16)
    a3 = (r1 - a2.astype(F32)).astype(BF16)
    out = jnp.dot(a1, b_exact, preferred_element_type=F32)
    out = out + jnp.dot(a2, b_exact, preferred_element_type=F32)
    out = out + jnp.dot(a3, b_exact, preferred_element_type=F32)
    return out


def _dot_exact_lhs(a_exact, b):
    b1 = b.astype(BF16)
    r1 = b - b1.astype(F32)
    b2 = r1.astype(BF16)
    b3 = (r1 - b2.astype(F32)).astype(BF16)
    out = jnp.dot(a_exact, b1, preferred_element_type=F32)
    out = out + jnp.dot(a_exact, b2, preferred_element_type=F32)
    out = out + jnp.dot(a_exact, b3, preferred_element_type=F32)
    return out


C_AK = 0
C_SM = C_AK + ATT_KV_W
C_DQKV = C_SM + LANES
C_DZ = C_DQKV + 3 * DN_W
C_GAB = C_DZ + DN_W
SM_DAB = IDX_HEAD_DIM
R_AQ = 0
R_IQ = R_AQ + ATT_Q_W
R_AV = R_IQ + IDX_Q_W
R_IW = R_AV + ATT_KV_W
R_DAB = R_IW + IDX_HEADS
R_END = R_DAB + 2 * DN_HEADS


def _in_proj_body(x_ref, g1_ref, w_ref, wt_ref, qg_ref, kg_ref,
                  qt_ref, iqt_ref, vt_ref, iwt_ref, dabt_ref, k_ref, ik_ref, dab_ref, dqkv_ref, dz_ref, gab_ref):
    x = x_ref[...]
    ms = jnp.mean(x * x, axis=-1, keepdims=True)
    h = (x * lax.rsqrt(ms + EPS) * g1_ref[...]).astype(BF16)

    rt = lax.dot_general(wt_ref[...], h, NT_DIMS, preferred_element_type=F32)
    qg = qg_ref[...] * (ATT_HEAD_DIM ** -0.5)
    for hd in range(ATT_HEADS):
        sl = slice(R_AQ + hd * ATT_HEAD_DIM, R_AQ + (hd + 1) * ATT_HEAD_DIM)
        blk = rt[sl]
        qt_ref[sl, :] = (blk * lax.rsqrt(jnp.mean(blk * blk, axis=0, keepdims=True) + EPS) * qg).astype(qt_ref.dtype)
    iqt_ref[...] = rt[R_IQ:R_AV].astype(iqt_ref.dtype)
    vt_ref[...] = rt[R_AV:R_IW].astype(vt_ref.dtype)
    iwt_ref[...] = rt[R_IW:R_DAB]
    dabt_ref[...] = rt[R_DAB:R_END]

    def proj(c0, c1):
        return jnp.dot(h, w_ref[:, c0:c1], preferred_element_type=F32)

    ak = proj(C_AK, C_SM)
    kg = kg_ref[...]
    for hd in range(ATT_KV_HEADS):
        sl = slice(hd * ATT_HEAD_DIM, (hd + 1) * ATT_HEAD_DIM)
        blk = ak[:, sl]
        k_ref[:, sl] = (blk * lax.rsqrt(jnp.mean(blk * blk, axis=-1, keepdims=True) + EPS) * kg).astype(k_ref.dtype)
    sm = proj(C_SM, C_DQKV)
    ik_ref[...] = sm[:, :IDX_HEAD_DIM].astype(ik_ref.dtype)
    dab_ref[...] = sm[:, SM_DAB:SM_DAB + 2 * DN_HEADS]
    dqkv_ref[...] = proj(C_DQKV, C_DZ)
    dz_ref[...] = proj(C_DZ, C_GAB)
    gab_ref[...] = proj(C_GAB, C_GAB + 2 * x.shape[1])


def _in_proj(xf, g1, w_in, q_gain, k_gain, tm):
    n, d = xf.shape
    cuts = np.cumsum([ATT_Q_W, ATT_KV_W, ATT_KV_W, IDX_Q_W, IDX_HEAD_DIM, IDX_HEADS,
                      DN_W, DN_W, DN_W, DN_W, DN_HEADS, DN_HEADS, d, d])[:-1].tolist()
    aq, ak, av, iq, ik, iw, dq, dk, dv, dz, da, db, ga, gb = jnp.split(w_in, cuts, axis=-1)
    pad = jnp.zeros((d, LANES - IDX_HEAD_DIM - 2 * DN_HEADS), w_in.dtype)
    w_all = jnp.concatenate([ak, ik, da, db, pad, dq, dk, dv, dz, ga, gb], axis=-1).astype(BF16)
    wt_all = jnp.concatenate([aq, iq, av, iw, da, db], axis=-1).T.astype(BF16)
    wtot = w_all.shape[1]
    row = lambda i: (i, 0)
    col = lambda i: (0, i)
    fixed = lambda i: (0, 0)
    out_shapes = (
        jax.ShapeDtypeStruct((ATT_Q_W, n), BF16),
        jax.ShapeDtypeStruct((IDX_Q_W, n), BF16),
        jax.ShapeDtypeStruct((ATT_KV_W, n), BF16),
        jax.ShapeDtypeStruct((IDX_HEADS, n), F32),
        jax.ShapeDtypeStruct((2 * DN_HEADS, n), F32),
        jax.ShapeDtypeStruct((n, ATT_KV_W), BF16),
        jax.ShapeDtypeStruct((n, IDX_HEAD_DIM), BF16),
        jax.ShapeDtypeStruct((n, 2 * DN_HEADS), F32),
        jax.ShapeDtypeStruct((n, 3 * DN_W), F32),
        jax.ShapeDtypeStruct((n, DN_W), F32),
        jax.ShapeDtypeStruct((n, 2 * d), F32),
    )
    out_specs = tuple(
        pl.BlockSpec((s.shape[0], tm), col) if s.shape[0] != n else pl.BlockSpec((tm, s.shape[1]), row)
        for s in out_shapes)
    return pl.pallas_call(
        _in_proj_body,
        grid=(n // tm,),
        in_specs=[
            pl.BlockSpec((tm, d), row),
            pl.BlockSpec((1, d), fixed),
            pl.BlockSpec((d, wtot), fixed),
            pl.BlockSpec((R_END, d), fixed),
            pl.BlockSpec((ATT_HEAD_DIM, 1), fixed),
            pl.BlockSpec((1, ATT_HEAD_DIM), fixed),
        ],
        out_specs=out_specs,
        out_shape=out_shapes,
        compiler_params=pltpu.CompilerParams(dimension_semantics=("parallel",), vmem_limit_bytes=VMEM_LIMIT),
    )(xf, g1.reshape(1, d), w_all, wt_all, q_gain.reshape(-1, 1), k_gain.reshape(1, -1))


FOLD_ROWS = 4 * SUBLANES
DSA_AHEAD = 2


def _dsa_body(qt_ref, iqt_ref, iwt_ref, k_ref, vt_ref, ik_ref, o_ref,
              key_ref, bias_ref, *acc_refs, tq, tk, ksel):
    i = pl.program_id(1)
    q0 = i * tq
    n_kb = (q0 + tq + tk - 1) // tk
    qpos = q0 + lax.broadcasted_iota(I32, (1, tq), 1)
    iwt = iwt_ref[...]

    def score_body(kb, carry):
        k0 = pl.multiple_of(kb * tk, tk)
        ikb = ik_ref[pl.ds(k0, tk), :]
        acc = jnp.zeros((tk, tq), F32)
        for hd in range(IDX_HEADS):
            sl = slice(hd * IDX_HEAD_DIM, (hd + 1) * IDX_HEAD_DIM)
            dots = jnp.dot(ikb, iqt_ref[sl, :], preferred_element_type=F32)
            acc = acc + iwt[hd:hd + 1, :] * jnp.maximum(dots, 0.0)
        bits = lax.bitcast_convert_type(acc, I32)
        keys = jnp.where(bits >= 0, bits, bits ^ INT_MAX)
        kpos = k0 + lax.broadcasted_iota(I32, (tk, 1), 0)
        key_ref[kb] = jnp.where(kpos <= qpos, keys, NEG_INF_KEY)
        return carry

    lax.fori_loop(0, n_kb, score_body, 0)

    def fold(x, op):
        return op(x.reshape(tk // FOLD_ROWS, FOLD_ROWS, tq), axis=0)

    def count_ge(thr):
        def body(kb, c):
            return c + fold(jnp.where(key_ref[kb] >= thr, 1.0, 0.0), jnp.sum)
        c = lax.fori_loop(0, n_kb, body, jnp.zeros((FOLD_ROWS, tq), F32))
        return jnp.sum(c, axis=0, keepdims=True)

    def bisect_cond(st):
        it, _, _, _, _, pending = st
        return jnp.logical_and(it < 32, jnp.max(pending) > 0.0)

    def bisect_body(st):
        it, lo, hi, c_lo, c_hi, pending = st
        mid = (lo >> 1) + (hi >> 1) + (lo & hi & 1)
        c = count_ge(mid)
        live = pending > 0.0
        up = jnp.logical_and(live, c >= ksel)
        down = jnp.logical_and(live, c < ksel)
        c_lo = jnp.where(up, c, c_lo)
        return (it + 1, jnp.where(up, mid, lo), jnp.where(down, mid, hi), c_lo, jnp.where(down, c, c_hi),
                jnp.where(c_lo == ksel, 0.0, pending))

    lo0 = jnp.full((1, tq), NEG_INF_KEY + 1, I32)
    hi0 = jnp.full((1, tq), INT_MAX, I32)
    zero = jnp.zeros((1, tq), F32)
    pending0 = jnp.where(qpos + 1 > ksel, 1.0, 0.0)
    _, thr, _, c_thr, c_above, _ = lax.while_loop(bisect_cond, bisect_body, (0, lo0, hi0, zero, zero, pending0))

    need = ksel - c_above
    has_excess = jnp.max(jnp.where(c_thr > ksel, 1.0, 0.0)) > 0.0

    @pl.when(has_excess)
    def _():
        lower = jnp.where(lax.broadcasted_iota(I32, (tk, tk), 0) >= lax.broadcasted_iota(I32, (tk, tk), 1),
                          1.0, 0.0).astype(BF16)

        def body(kb, seen):
            keys = key_ref[kb]
            tie = keys == thr
            tie_f = jnp.where(tie, 1.0, 0.0)
            prefix = jnp.dot(lower, tie_f.astype(BF16), preferred_element_type=F32) + seen
            drop = jnp.logical_and(tie, prefix > need)
            key_ref[kb] = jnp.where(drop, NEG_INF_KEY, keys)
            return seen + jnp.sum(tie_f, axis=0, keepdims=True)

        lax.fori_loop(0, n_kb, body, jnp.zeros((1, tq), F32))

    for acc_ref in acc_refs:
        acc_ref[...] = jnp.zeros(acc_ref.shape, F32)
    grp = ATT_HEADS // ATT_KV_HEADS

    def att_body(kb, carry):
        m_all, l_all = carry
        k0 = pl.multiple_of(kb * tk, tk)
        kblk = k_ref[pl.ds(k0, tk), :]
        vtb = vt_ref[kb]
        bias_ref[...] = jnp.where(key_ref[kb] >= thr, 0.0, -1e30)
        m_rows, l_rows = [], []

        def logits(hd):
            g = hd // grp
            return jnp.dot(kblk[:, g * ATT_HEAD_DIM:(g + 1) * ATT_HEAD_DIM],
                           qt_ref[hd * ATT_HEAD_DIM:(hd + 1) * ATT_HEAD_DIM, :],
                           preferred_element_type=F32) + bias_ref[...]

        def accumulate(hd, alpha, p):
            g = hd // grp
            acc_ref = acc_refs[hd]
            acc_ref[...] = alpha * acc_ref[...] + jnp.dot(vtb[g * ATT_HEAD_DIM:(g + 1) * ATT_HEAD_DIM, :], p,
                                                          preferred_element_type=F32)

        queue = [logits(hd) for hd in range(DSA_AHEAD)]
        pending = None
        for hd in range(ATT_HEADS):
            s = queue.pop(0)
            if hd + DSA_AHEAD < ATT_HEADS:
                queue.append(logits(hd + DSA_AHEAD))
            m_old = m_all[hd:hd + 1, :]
            m_new = jnp.maximum(m_old, jnp.max(fold(s, jnp.max), axis=0, keepdims=True))
            p = jnp.exp(s - m_new)
            alpha = jnp.exp(m_old - m_new)
            l_rows.append(alpha * l_all[hd:hd + 1, :] + jnp.sum(fold(p, jnp.sum), axis=0, keepdims=True))
            m_rows.append(m_new)
            if pending is not None:
                accumulate(*pending)
            pending = (hd, alpha, p.astype(BF16))
        accumulate(*pending)
        return jnp.concatenate(m_rows, axis=0), jnp.concatenate(l_rows, axis=0)

    m0 = jnp.full((ATT_HEADS, tq), -1e30, F32)
    _, l_fin = lax.fori_loop(0, n_kb, att_body, (m0, jnp.zeros((ATT_HEADS, tq), F32)))
    for pair in range(ATT_HEADS // 2):
        rows = [acc_refs[hd][...] / l_fin[hd:hd + 1, :] for hd in (2 * pair, 2 * pair + 1)]
        o_ref[:, pair * LANES:(pair + 1) * LANES] = jnp.concatenate(rows, axis=0).T.astype(o_ref.dtype)


def _dsa(qt, iqt, iwt, k, vt, ik, b, t, tq, tk):
    n = b * t
    nq = t // tq
    nkb = t // tk
    ksel = min(TOPK_MAX, t // 4)
    qcol = lambda bi, i: (0, bi * nq + i)
    brow = lambda bi, i: (bi, 0)
    vtb = vt.reshape(ATT_KV_W, n // tk, tk).transpose(1, 0, 2)
    return pl.pallas_call(
        functools.partial(_dsa_body, tq=tq, tk=tk, ksel=ksel),
        grid=(b, nq),
        in_specs=[
            pl.BlockSpec((ATT_Q_W, tq), qcol),
            pl.BlockSpec((IDX_Q_W, tq), qcol),
            pl.BlockSpec((IDX_HEADS, tq), qcol),
            pl.BlockSpec((t, ATT_KV_W), brow),
            pl.BlockSpec((nkb, ATT_KV_W, tk), lambda bi, i: (bi, 0, 0)),
            pl.BlockSpec((t, IDX_HEAD_DIM), brow),
        ],
        out_specs=pl.BlockSpec((tq, ATT_Q_W), lambda bi, i: (bi * nq + i, 0)),
        out_shape=jax.ShapeDtypeStruct((n, ATT_Q_W), BF16),
        scratch_shapes=[
            pltpu.VMEM((nkb, tk, tq), I32),
            pltpu.VMEM((tk, tq), F32),
        ] + [pltpu.VMEM((ATT_HEAD_DIM, tq), F32) for _ in range(ATT_HEADS)],
        compiler_params=pltpu.CompilerParams(dimension_semantics=("parallel", "arbitrary"),
                                             vmem_limit_bytes=VMEM_LIMIT),
    )(qt, iqt, iwt, k, vtb, ik)


def _deltanet_body(x_ref, dz_ref, dab_ref, dabt_ref, cw_ref, alog_r_ref, bias_r_ref, alog_c_ref, bias_c_ref,
                   og_ref, y_ref, carry_ref, state_ref, *, tb):
    c = DN_CHUNK
    d = DN_HEAD_DIM
    nh = DN_HEADS

    @pl.when(pl.program_id(1) == 0)
    def _():
        carry_ref[...] = jnp.zeros(carry_ref.shape, F32)
        state_ref[...] = jnp.zeros(state_ref.shape, F32)

    xb = x_ref[...]
    xx = jnp.concatenate([carry_ref[...], xb], axis=0)
    cw = cw_ref[...]
    off = SUBLANES - (DN_CONV - 1)
    conv = cw[0:1] * xx[off:off + tb]
    for j in range(1, DN_CONV):
        conv = conv + cw[j:j + 1] * xx[off + j:off + j + tb]
    carry_ref[...] = xb[tb - SUBLANES:tb]
    qkv = conv * _sigmoid(conv)

    dab = dab_ref[...]
    g_col = -jnp.exp(alog_r_ref[...]) * _softplus(dab[:, 0:nh] + bias_r_ref[...])
    beta_col = _sigmoid(dab[:, nh:2 * nh])
    g_row = -jnp.exp(alog_c_ref[...]) * _softplus(dabt_ref[0:nh, :] + bias_c_ref[...])

    ri = lax.broadcasted_iota(I32, (tb, tb), 0)
    ci = lax.broadcasted_iota(I32, (tb, tb), 1)
    same_chunk = (ri // c) == (ci // c)
    lower_blk = jnp.where(jnp.logical_and(same_chunk, ri >= ci), 1.0, 0.0).astype(BF16)
    upper_blk = jnp.where(jnp.logical_and(same_chunk, ri <= ci), 1.0, 0.0).astype(BF16)
    gc_col = _dot_exact_lhs(lower_blk, g_col)
    gc_row = _dot_exact_rhs(g_row, upper_blk)

    r64 = lax.broadcasted_iota(I32, (c, c), 0)
    c64 = lax.broadcasted_iota(I32, (c, c), 1)
    tri = r64 >= c64
    strict = r64 > c64
    eye = jnp.where(r64 == c64, 1.0, 0.0)
    og = og_ref[...]

    inst = [(ch, hd) for ch in range(tb // c) for hd in range(nh)]
    rows = lambda ch: slice(ch * c, (ch + 1) * c)
    qs, ks, vs, bcols, gcols, decays, kbetas = [], [], [], [], [], [], []
    for ch, hd in inst:
        rs = rows(ch)
        qc = qkv[rs, hd * d:(hd + 1) * d]
        kc = qkv[rs, nh * d + hd * d:nh * d + (hd + 1) * d]
        qs.append(qc * lax.rsqrt(jnp.sum(qc * qc, axis=-1, keepdims=True) + EPS) * (d ** -0.5))
        ks.append(kc * lax.rsqrt(jnp.sum(kc * kc, axis=-1, keepdims=True) + EPS))
        vs.append(qkv[rs, 2 * nh * d + hd * d:2 * nh * d + (hd + 1) * d])
        bcols.append(beta_col[rs, hd:hd + 1])
        gcols.append(gc_col[rs, hd:hd + 1])
        decays.append(jnp.exp(jnp.where(tri, gcols[-1] - gc_row[hd:hd + 1, rs], -jnp.inf)))
        kbetas.append(ks[-1] * bcols[-1])
    kk = [_dot_nt(kbetas[i], ks[i]) for i in range(len(inst))]
    qk = [_dot_nt(qs[i], ks[i]) for i in range(len(inst))]
    a_mats = [jnp.where(strict, kk[i] * decays[i], 0.0) for i in range(len(inst))]
    intras = [jnp.where(tri, qk[i] * decays[i], 0.0) for i in range(len(inst))]
    xs = [eye - a for a in a_mats]
    ps = a_mats
    for _ in range(5):
        ps = [_dot_hi(p, p) for p in ps]
        xs = [x + _dot_hi(x, p) for x, p in zip(xs, ps)]
    egs = [jnp.exp(g) for g in gcols]
    sols = [_dot(xs[i], jnp.concatenate([vs[i] * bcols[i], kbetas[i] * egs[i]], axis=1)) for i in range(len(inst))]
    qes = [qs[i] * egs[i] for i in range(len(inst))]
    glasts = [g[c - 1:c] for g in gcols]
    kdecs = [ks[i] * jnp.exp(glasts[i] - gcols[i]) for i in range(len(inst))]

    states = [state_ref[hd] for hd in range(nh)]
    for ch in range(tb // c):
        ii = [ch * nh + hd for hd in range(nh)]
        both = [_dot(jnp.concatenate([sols[i][:, d:], qes[i]], axis=0), states[hd]) for hd, i in enumerate(ii)]
        v_new = [sols[i][:, :d] - both[hd][:c] for hd, i in enumerate(ii)]
        outs = [both[hd][c:] + _dot(intras[i], v_new[hd]) for hd, i in enumerate(ii)]
        states = [states[hd] * jnp.exp(glasts[i]) + lax.dot_general(
            kdecs[i].astype(BF16), v_new[hd].astype(BF16), TN_DIMS, preferred_element_type=F32)
            for hd, i in enumerate(ii)]
        for hd in range(nh):
            ls = slice(hd * d, (hd + 1) * d)
            o = outs[hd]
            on = o * lax.rsqrt(jnp.mean(o * o, axis=-1, keepdims=True) + EPS) * og
            z = dz_ref[rows(ch), ls]
            y_ref[rows(ch), ls] = (on * (z * _sigmoid(z))).astype(y_ref.dtype)
    for hd in range(nh):
        state_ref[hd] = states[hd]


def _deltanet(dqkv, dz, dab, dabt, conv_w, a_log, dt_bias, out_gain, b, t, tb):
    n = b * t
    nb = t // tb
    row = lambda bi, j: (bi * nb + j, 0)
    col = lambda bi, j: (0, bi * nb + j)
    fixed = lambda bi, j: (0, 0)
    nh = DN_HEADS
    return pl.pallas_call(
        functools.partial(_deltanet_body, tb=tb),
        grid=(b, nb),
        in_specs=[
            pl.BlockSpec((tb, 3 * DN_W), row),
            pl.BlockSpec((tb, DN_W), row),
            pl.BlockSpec((tb, 2 * nh), row),
            pl.BlockSpec((2 * nh, tb), col),
            pl.BlockSpec((DN_CONV, 3 * DN_W), fixed),
            pl.BlockSpec((1, nh), fixed),
            pl.BlockSpec((1, nh), fixed),
            pl.BlockSpec((nh, 1), fixed),
            pl.BlockSpec((nh, 1), fixed),
            pl.BlockSpec((1, DN_HEAD_DIM), fixed),
        ],
        out_specs=pl.BlockSpec((tb, DN_W), row),
        out_shape=jax.ShapeDtypeStruct((n, DN_W), BF16),
        scratch_shapes=[
            pltpu.VMEM((SUBLANES, 3 * DN_W), F32),
            pltpu.VMEM((nh, DN_HEAD_DIM, DN_HEAD_DIM), F32),
        ],
        compiler_params=pltpu.CompilerParams(dimension_semantics=("parallel", "arbitrary"),
                                             vmem_limit_bytes=VMEM_LIMIT),
    )(dqkv, dz, dab, dabt, conv_w, a_log.reshape(1, nh), dt_bias.reshape(1, nh),
      a_log.reshape(nh, 1), dt_bias.reshape(nh, 1), out_gain.reshape(1, -1))


def _merge_body(x_ref, ya_ref, yd_ref, gab_ref, wa_ref, wb_ref, wo_ref, o_ref):
    dm = x_ref.shape[1]
    gab = gab_ref[...]
    a = jnp.dot(ya_ref[...], wa_ref[...], preferred_element_type=F32)
    bb = jnp.dot(yd_ref[...], wb_ref[...], preferred_element_type=F32)
    merged = _sigmoid(gab[:, :dm]) * a + _sigmoid(gab[:, dm:]) * bb
    o_ref[...] = x_ref[...] + jnp.dot(merged.astype(BF16), wo_ref[...], preferred_element_type=F32)


def _merge(xf, y_att, y_dn, gab, w_a, w_b, w_o, tm):
    n, d = xf.shape
    row = lambda i: (i, 0)
    fixed = lambda i: (0, 0)
    return pl.pallas_call(
        _merge_body,
        grid=(n // tm,),
        in_specs=[
            pl.BlockSpec((tm, d), row),
            pl.BlockSpec((tm, ATT_Q_W), row),
            pl.BlockSpec((tm, DN_W), row),
            pl.BlockSpec((tm, 2 * d), row),
            pl.BlockSpec((ATT_Q_W, d), fixed),
            pl.BlockSpec((DN_W, d), fixed),
            pl.BlockSpec((d, d), fixed),
        ],
        out_specs=pl.BlockSpec((tm, d), row),
        out_shape=jax.ShapeDtypeStruct((n, d), F32),
        compiler_params=pltpu.CompilerParams(dimension_semantics=("parallel",), vmem_limit_bytes=VMEM_LIMIT),
    )(xf, y_att, y_dn, gab, w_a.astype(BF16), w_b.astype(BF16), w_o.astype(BF16))


def _top16_rows(s, payload=None):
    rows = lax.broadcasted_iota(I32, s.shape, 0)
    big = s.shape[0]
    vals, pays = [], []
    for _ in range(PEER_TOPK):
        m = jnp.max(s, axis=0, keepdims=True)
        am = jnp.min(jnp.where(s == m, rows, big), axis=0, keepdims=True)
        hit = rows == am
        vals.append(m)
        if payload is None:
            pays.append(am)
        else:
            pays.append(jnp.sum(jnp.where(hit, payload, 0), axis=0, keepdims=True))
        s = jnp.where(hit, -jnp.inf, s)
    return jnp.concatenate(vals, axis=0), jnp.concatenate(pays, axis=0)


def _pair_candidates(s0, i0, s1, i1):
    k = PEER_TOPK
    rows8 = lax.broadcasted_iota(I32, (SUBLANES, s0.shape[1]), 0)
    cs = [s0[0:1] + s1]
    ce = [i0[0:1] * PEER_N_KEYS + i1]
    for i in range(1, SUBLANES):
        valid = rows8 < k // (i + 1)
        cs.append(jnp.where(valid, s0[i:i + 1] + s1[0:SUBLANES], -jnp.inf))
        ce.append(i0[i:i + 1] * PEER_N_KEYS + i1[0:SUBLANES])
    cs.append(s0[SUBLANES:k] + s1[0:1])
    ce.append(i0[SUBLANES:k] * PEER_N_KEYS + i1[0:1])
    return jnp.concatenate(cs, axis=0), jnp.concatenate(ce, axis=0)


def _peer_route_body(x_ref, g2_ref, wq_ref, sk_ref, ids_ref, gates_ref, *, tm):
    x = x_ref[...]
    ms = jnp.mean(x * x, axis=-1, keepdims=True)
    h = (x * lax.rsqrt(ms + EPS) * g2_ref[...]).astype(BF16)
    q = jnp.dot(h, wq_ref[...], preferred_element_type=F32).astype(BF16)
    half = PEER_KEY_DIM // 2
    for hd in range(PEER_HEADS):
        tops = []
        for p in range(2):
            c0 = hd * PEER_KEY_DIM + p * half
            st = lax.dot_general(sk_ref[2 * hd + p], q[:, c0:c0 + half], NT_DIMS,
                                 preferred_element_type=F32)
            tops.append(_top16_rows(st))
        (s0, i0), (s1, i1) = tops
        cand_s, cand_e = _pair_candidates(s0, i0, s1, i1)
        best, expert = _top16_rows(cand_s, cand_e)
        e = jnp.exp(best - best[0:1])
        gate = e / jnp.sum(e, axis=0, keepdims=True)
        rs = slice(hd * PEER_TOPK, (hd + 1) * PEER_TOPK)
        ids_ref[rs, :] = expert
        gates_ref[rs, :] = gate


def _peer_route(x1, g2, w_query, sub_keys, tm):
    n, d = x1.shape
    nsel = PEER_HEADS * PEER_TOPK
    half = PEER_KEY_DIM // 2
    sk = sub_keys.reshape(PEER_HEADS * 2, PEER_N_KEYS, half).astype(BF16)
    return pl.pallas_call(
        functools.partial(_peer_route_body, tm=tm),
        grid=(n // tm,),
        in_specs=[
            pl.BlockSpec((tm, d), lambda i: (i, 0)),
            pl.BlockSpec((1, d), lambda i: (0, 0)),
            pl.BlockSpec((d, PEER_HEADS * PEER_KEY_DIM), lambda i: (0, 0)),
            pl.BlockSpec((PEER_HEADS * 2, PEER_N_KEYS, half), lambda i: (0, 0, 0)),
        ],
        out_specs=(pl.BlockSpec((nsel, tm), lambda i: (0, i)),
                   pl.BlockSpec((nsel, tm), lambda i: (0, i))),
        out_shape=(jax.ShapeDtypeStruct((nsel, n), I32), jax.ShapeDtypeStruct((nsel, n), F32)),
        compiler_params=pltpu.CompilerParams(dimension_semantics=("parallel",), vmem_limit_bytes=VMEM_LIMIT),
    )(x1, g2.reshape(1, d), w_query.astype(BF16), sk)


PEER_SLOTS = 16
PEER_WAIT_GROUP = 4


def _peer_apply_body(ids_ref, x_ref, g2_ref, gates_ref, uv_hbm, o_ref, sems, *slot_refs, tb):
    bufs, coefs = slot_refs[:PEER_SLOTS], slot_refs[PEER_SLOTS:]
    nsel = PEER_HEADS * PEER_TOPK
    nchunk = x_ref.shape[1]
    dm = nchunk * LANES

    ngrp = nsel // SUBLANES

    def issue(tok, slot, k0, k1):
        for kk in range(k0, k1):
            pltpu.make_async_copy(uv_hbm.at[ids_ref[tok, kk]], bufs[slot].at[kk], sems.at[slot]).start(
                priority=kk % 2)

    def wait_all(slot):
        pltpu.make_async_copy(uv_hbm.at[pl.ds(0, nsel)], bufs[slot], sems.at[slot]).wait()

    eye = lax.broadcasted_iota(I32, (nsel, nsel), 0) == lax.broadcasted_iota(I32, (nsel, nsel), 1)
    sub = lax.broadcasted_iota(I32, (SUBLANES, LANES), 0)
    masks = {k: (sub & k) == 0 for k in (4, 2, 1)}
    g2 = g2_ref[...]

    def merge(x, y, k):
        if k == 4:
            return jnp.where(masks[k], x, y) + pltpu.roll(jnp.where(masks[k], y, x), k, axis=0)
        return jnp.where(masks[k], x + pltpu.roll(x, SUBLANES - k, axis=0), y + pltpu.roll(y, k, axis=0))

    order = (0, 4, 2, 6, 1, 5, 3, 7)

    def u_of(word):
        return lax.bitcast_convert_type(word << 16, F32)

    def v_of(word):
        return lax.bitcast_convert_type(word & jnp.uint32(0xFFFF0000), F32)

    def normed(tok):
        xt = x_ref[tok]
        ssq = jnp.sum(jnp.sum(xt * xt, axis=1, keepdims=True), axis=0, keepdims=True)
        return xt * lax.rsqrt(ssq * (1.0 / dm) + EPS) * g2

    def step(t, slot, nxt, nxt_slot, h8, t_after):
        def prefetch(k0, k1):
            if nxt is not None:
                issue(nxt, nxt_slot, k0, k1)

        if slot % PEER_WAIT_GROUP == 0:
            for j in range(PEER_WAIT_GROUP):
                wait_all(slot + j)
        grow = gates_ref[t]
        gcol = jnp.sum(jnp.where(eye, jnp.broadcast_to(grow, (nsel, nsel)), 0.0), axis=1, keepdims=True)
        per_grp = nsel // (2 * ngrp)
        groups = []
        for grp in range(ngrp):
            prefetch(grp * per_grp, (grp + 1) * per_grp)
            p = [u_of(bufs[slot][grp * SUBLANES + order.index(j)]) * h8 for j in range(SUBLANES)]
            q4 = [merge(p[2 * i], p[2 * i + 1], 4) for i in range(4)]
            q2 = [merge(q4[2 * i], q4[2 * i + 1], 2) for i in range(2)]
            groups.append(merge(q2[0], q2[1], 1))
        colsum = jnp.concatenate(groups, axis=0)
        act = jnp.sum(colsum, axis=1, keepdims=True)
        gelu = 0.5 * act * (1.0 + lax.erf(act * (2.0 ** -0.5)))
        coef_ref = coefs[slot]
        coef_ref[...] = jnp.broadcast_to(gcol * gelu, (nsel, LANES))
        h8_after = normed(t_after)
        acc = jnp.zeros((nchunk, LANES), F32)
        for kk in range(nsel):
            if kk % 2 == 0:
                prefetch(nsel // 2 + kk // 2, nsel // 2 + kk // 2 + 1)
            acc = acc + coef_ref[kk:kk + 1, :] * v_of(bufs[slot][kk])
        o_ref[t] = x_ref[t] + acc
        return h8_after

    ahead = PEER_SLOTS - 1
    for s in range(ahead):
        issue(s, s, 0, nsel)

    def group_body(g, h8):
        t0 = g * PEER_SLOTS
        for s in range(PEER_SLOTS):
            h8 = step(t0 + s, s, t0 + s + ahead, (s + ahead) % PEER_SLOTS, h8, t0 + s + 1)
        return h8

    h8 = lax.fori_loop(0, tb // PEER_SLOTS - 1, group_body, normed(0))
    t0 = tb - PEER_SLOTS
    h8 = step(t0, 0, t0 + ahead, ahead % PEER_SLOTS, h8, t0 + 1)
    for s in range(1, PEER_SLOTS):
        h8 = step(t0 + s, s, None, None, h8, min(t0 + s + 1, tb - 1))


def _peer_apply(x1, g2, ids, gates, peer_u, peer_v, tb):
    n, d = x1.shape
    nsel = PEER_HEADS * PEER_TOPK
    ne = peer_u.shape[0]
    nchunk = d // LANES

    def half_words(tab):
        return lax.bitcast_convert_type(tab.astype(BF16), jnp.uint16).astype(jnp.uint32)

    uv = ((half_words(peer_v) << 16) | half_words(peer_u)).reshape(ne, nchunk, LANES)
    out = pl.pallas_call(
        functools.partial(_peer_apply_body, tb=tb),
        grid=(n // tb,),
        in_specs=[
            pl.BlockSpec((tb, nsel), lambda i: (i, 0), memory_space=pltpu.SMEM),
            pl.BlockSpec((tb, nchunk, LANES), lambda i: (i, 0, 0)),
            pl.BlockSpec((nchunk, LANES), lambda i: (0, 0)),
            pl.BlockSpec((tb, 1, nsel), lambda i: (i, 0, 0)),
            pl.BlockSpec(memory_space=pl.ANY),
        ],
        out_specs=pl.BlockSpec((tb, nchunk, LANES), lambda i: (i, 0, 0)),
        out_shape=jax.ShapeDtypeStruct((n, nchunk, LANES), F32),
        scratch_shapes=[
            pltpu.SemaphoreType.DMA((PEER_SLOTS,)),
        ] + [pltpu.VMEM((nsel, nchunk, LANES), jnp.uint32) for _ in range(PEER_SLOTS)]
        + [pltpu.VMEM((nsel, LANES), F32) for _ in range(PEER_SLOTS)],
        compiler_params=pltpu.CompilerParams(dimension_semantics=("arbitrary",), vmem_limit_bytes=VMEM_LIMIT),
    )(ids, x1.reshape(n, nchunk, LANES), g2.reshape(nchunk, LANES), gates.reshape(n, 1, nsel), uv)
    return out.reshape(n, d)


def _block_sizes(t):
    return dict(
        tm_proj=256,
        tq=min(256, t),
        tk=min(512, t),
        tb_dn=min(256, t),
        tm_merge=512,
        tm_route=LANES,
        tb_peer=512,
    )


def kernel(x, norm1_gain, w_in, q_norm_gain, k_norm_gain, dn_conv_w, dn_a_log, dn_dt_bias, dn_out_norm_gain,
           w_att_branch, w_dn_branch, w_o, norm2_gain, peer_w_query, peer_sub_keys, peer_u, peer_v):
    b, t, d = x.shape
    n = b * t
    bs = _block_sizes(t)
    xf = x.reshape(n, d)
    for layer in range(w_in.shape[0]):
        (qt, iqt, vt, iwt, dabt, k, ik, dab, dqkv, dz, gab) = _in_proj(
            xf, norm1_gain[layer], w_in[layer], q_norm_gain[layer], k_norm_gain[layer], bs["tm_proj"])
        y_att = _dsa(qt, iqt, iwt, k, vt, ik, b, t, bs["tq"], bs["tk"])
        y_dn = _deltanet(dqkv, dz, dab, dabt, dn_conv_w[layer], dn_a_log[layer], dn_dt_bias[layer],
                         dn_out_norm_gain[layer], b, t, bs["tb_dn"])
        x1 = _merge(xf, y_att, y_dn, gab, w_att_branch[layer], w_dn_branch[layer], w_o[layer], bs["tm_merge"])
        ids_t, gates_t = _peer_route(x1, norm2_gain[layer], peer_w_query[layer], peer_sub_keys[layer],
                                     bs["tm_route"])
        xf = _peer_apply(x1, norm2_gain[layer], ids_t.T, gates_t.T, peer_u[layer], peer_v[layer], bs["tb_peer"])
    return xf.reshape(b, t, d)
```

```python
import functools

import jax
import jax.numpy as jnp
import numpy as np
from jax import lax
from jax.experimental import pallas as pl
from jax.experimental.pallas import tpu as pltpu

F32 = jnp.float32
BF16 = jnp.bfloat16
I32 = jnp.int32

ATT_HEADS = 8
ATT_KV_HEADS = 2
ATT_HEAD_DIM = 64
IDX_HEADS = 8
IDX_HEAD_DIM = 64
TOPK_MAX = 256
DN_HEADS = 4
DN_HEAD_DIM = 128
DN_CONV = 4
DN_CHUNK = 64
PEER_HEADS = 8
PEER_N_KEYS = 128
PEER_KEY_DIM = 256
PEER_TOPK = 16
EPS = 1e-6

ATT_Q_W = ATT_HEADS * ATT_HEAD_DIM
ATT_KV_W = ATT_KV_HEADS * ATT_HEAD_DIM
IDX_Q_W = IDX_HEADS * IDX_HEAD_DIM
DN_W = DN_HEADS * DN_HEAD_DIM

LANES = 128
SUBLANES = 8
VMEM_LIMIT = 56 * 1024 * 1024

NEG_INF_KEY = int(np.int32(np.uint32(0xFF800000) ^ np.uint32(0x7FFFFFFF)))
INT_MAX = int(np.iinfo(np.int32).max)

NT_DIMS = (((1,), (1,)), ((), ()))
TN_DIMS = (((0,), (0,)), ((), ()))


def _sigmoid(x):
    return 1.0 / (1.0 + jnp.exp(-x))


def _softplus(x):
    return jnp.maximum(x, 0.0) + jnp.log(1.0 + jnp.exp(-jnp.abs(x)))


def _dot(a, b):
    return jnp.dot(a.astype(BF16), b.astype(BF16), preferred_element_type=F32)


def _dot_nt(a, b):
    return lax.dot_general(a.astype(BF16), b.astype(BF16), NT_DIMS, preferred_element_type=F32)


def _split2(x):
    hi = x.astype(BF16)
    lo = (x - hi.astype(F32)).astype(BF16)
    return hi, lo


def _dot_hi(a, b):
    a1, a2 = _split2(a)
    b1, b2 = _split2(b)
    out = jnp.dot(a1, b1, preferred_element_type=F32)
    out = out + jnp.dot(a1, b2, preferred_element_type=F32)
    out = out + jnp.dot(a2, b1, preferred_element_type=F32)
    return out


def _dot_exact_rhs(a, b_exact):
    a1 = a.astype(BF16)
    r1 = a - a1.astype(F32)
    a2 = r1.astype(BF16)
    a3 = (r1 - a2.astype(F32)).astype(BF16)
    out = jnp.dot(a1, b_exact, preferred_element_type=F32)
    out = out + jnp.dot(a2, b_exact, preferred_element_type=F32)
    out = out + jnp.dot(a3, b_exact, preferred_element_type=F32)
    return out


def _dot_exact_lhs(a_exact, b):
    b1 = b.astype(BF16)
    r1 = b - b1.astype(F32)
    b2 = r1.astype(BF16)
    b3 = (r1 - b2.astype(F32)).astype(BF16)
    out = jnp.dot(a_exact, b1, preferred_element_type=F32)
    out = out + jnp.dot(a_exact, b2, preferred_element_type=F32)
    out = out + jnp.dot(a_exact, b3, preferred_element_type=F32)
    return out


C_AK = 0
C_SM = C_AK + ATT_KV_W
C_DQKV = C_SM + LANES
C_DZ = C_DQKV + 3 * DN_W
C_GAB = C_DZ + DN_W
SM_DAB = IDX_HEAD_DIM
R_AQ = 0
R_IQ = R_AQ + ATT_Q_W
R_AV = R_IQ + IDX_Q_W
R_IW = R_AV + ATT_KV_W
R_DAB = R_IW + IDX_HEADS
R_END = R_DAB + 2 * DN_HEADS


def _in_proj_body(x_ref, g1_ref, w_ref, wt_ref, qg_ref, kg_ref,
                  qt_ref, iqt_ref, vt_ref, iwt_ref, dabt_ref, k_ref, ik_ref, dab_ref, dqkv_ref, dz_ref, gab_ref):
    x = x_ref[...]
    ms = jnp.mean(x * x, axis=-1, keepdims=True)
    h = (x * lax.rsqrt(ms + EPS) * g1_ref[...]).astype(BF16)

    rt = lax.dot_general(wt_ref[...], h, NT_DIMS, preferred_element_type=F32)
    qg = qg_ref[...] * (ATT_HEAD_DIM ** -0.5)
    for hd in range(ATT_HEADS):
        sl = slice(R_AQ + hd * ATT_HEAD_DIM, R_AQ + (hd + 1) * ATT_HEAD_DIM)
        blk = rt[sl]
        qt_ref[sl, :] = (blk * lax.rsqrt(jnp.mean(blk * blk, axis=0, keepdims=True) + EPS) * qg).astype(qt_ref.dtype)
    iqt_ref[...] = rt[R_IQ:R_AV].astype(iqt_ref.dtype)
    vt_ref[...] = rt[R_AV:R_IW].astype(vt_ref.dtype)
    iwt_ref[...] = rt[R_IW:R_DAB]
    dabt_ref[...] = rt[R_DAB:R_END]

    def proj(c0, c1):
        return jnp.dot(h, w_ref[:, c0:c1], preferred_element_type=F32)

    ak = proj(C_AK, C_SM)
    kg = kg_ref[...]
    for hd in range(ATT_KV_HEADS):
        sl = slice(hd * ATT_HEAD_DIM, (hd + 1) * ATT_HEAD_DIM)
        blk = ak[:, sl]
        k_ref[:, sl] = (blk * lax.rsqrt(jnp.mean(blk * blk, axis=-1, keepdims=True) + EPS) * kg).astype(k_ref.dtype)
    sm = proj(C_SM, C_DQKV)
    ik_ref[...] = sm[:, :IDX_HEAD_DIM].astype(ik_ref.dtype)
    dab_ref[...] = sm[:, SM_DAB:SM_DAB + 2 * DN_HEADS]
    dqkv_ref[...] = proj(C_DQKV, C_DZ)
    dz_ref[...] = proj(C_DZ, C_GAB)
    gab_ref[...] = proj(C_GAB, C_GAB + 2 * x.shape[1])


def _in_proj(xf, g1, w_in, q_gain, k_gain, tm):
    n, d = xf.shape
    cuts = np.cumsum([ATT_Q_W, ATT_KV_W, ATT_KV_W, IDX_Q_W, IDX_HEAD_DIM, IDX_HEADS,
                      DN_W, DN_W, DN_W, DN_W, DN_HEADS, DN_HEADS, d, d])[:-1].tolist()
    aq, ak, av, iq, ik, iw, dq, dk, dv, dz, da, db, ga, gb = jnp.split(w_in, cuts, axis=-1)
    pad = jnp.zeros((d, LANES - IDX_HEAD_DIM - 2 * DN_HEADS), w_in.dtype)
    w_all = jnp.concatenate([ak, ik, da, db, pad, dq, dk, dv, dz, ga, gb], axis=-1).astype(BF16)
    wt_all = jnp.concatenate([aq, iq, av, iw, da, db], axis=-1).T.astype(BF16)
    wtot = w_all.shape[1]
    row = lambda i: (i, 0)
    col = lambda i: (0, i)
    fixed = lambda i: (0, 0)
    out_shapes = (
        jax.ShapeDtypeStruct((ATT_Q_W, n), BF16),
        jax.ShapeDtypeStruct((IDX_Q_W, n), BF16),
        jax.ShapeDtypeStruct((ATT_KV_W, n), BF16),
        jax.ShapeDtypeStruct((IDX_HEADS, n), F32),
        jax.ShapeDtypeStruct((2 * DN_HEADS, n), F32),
        jax.ShapeDtypeStruct((n, ATT_KV_W), BF16),
        jax.ShapeDtypeStruct((n, IDX_HEAD_DIM), BF16),
        jax.ShapeDtypeStruct((n, 2 * DN_HEADS), F32),
        jax.ShapeDtypeStruct((n, 3 * DN_W), F32),
        jax.ShapeDtypeStruct((n, DN_W), F32),
        jax.ShapeDtypeStruct((n, 2 * d), F32),
    )
    out_specs = tuple(
        pl.BlockSpec((s.shape[0], tm), col) if s.shape[0] != n else pl.BlockSpec((tm, s.shape[1]), row)
        for s in out_shapes)
    return pl.pallas_call(
        _in_proj_body,
        grid=(n // tm,),
        in_specs=[
            pl.BlockSpec((tm, d), row),
            pl.BlockSpec((1, d), fixed),
            pl.BlockSpec((d, wtot), fixed),
            pl.BlockSpec((R_END, d), fixed),
            pl.BlockSpec((ATT_HEAD_DIM, 1), fixed),
            pl.BlockSpec((1, ATT_HEAD_DIM), fixed),
        ],
        out_specs=out_specs,
        out_shape=out_shapes,
        compiler_params=pltpu.CompilerParams(dimension_semantics=("parallel",), vmem_limit_bytes=VMEM_LIMIT),
    )(xf, g1.reshape(1, d), w_all, wt_all, q_gain.reshape(-1, 1), k_gain.reshape(1, -1))


FOLD_ROWS = 4 * SUBLANES
DSA_AHEAD = 2


def _dsa_body(qt_ref, iqt_ref, iwt_ref, k_ref, vt_ref, ik_ref, o_ref,
              key_ref, bias_ref, *acc_refs, tq, tk, ksel):
    i = pl.program_id(1)
    q0 = i * tq
    n_kb = (q0 + tq + tk - 1) // tk
    qpos = q0 + lax.broadcasted_iota(I32, (1, tq), 1)
    iwt = iwt_ref[...]

    def score_body(kb, carry):
        k0 = pl.multiple_of(kb * tk, tk)
        ikb = ik_ref[pl.ds(k0, tk), :]
        acc = jnp.zeros((tk, tq), F32)
        for hd in range(IDX_HEADS):
            sl = slice(hd * IDX_HEAD_DIM, (hd + 1) * IDX_HEAD_DIM)
            dots = jnp.dot(ikb, iqt_ref[sl, :], preferred_element_type=F32)
            acc = acc + iwt[hd:hd + 1, :] * jnp.maximum(dots, 0.0)
        bits = lax.bitcast_convert_type(acc, I32)
        keys = jnp.where(bits >= 0, bits, bits ^ INT_MAX)
        kpos = k0 + lax.broadcasted_iota(I32, (tk, 1), 0)
        key_ref[kb] = jnp.where(kpos <= qpos, keys, NEG_INF_KEY)
        return carry

    lax.fori_loop(0, n_kb, score_body, 0)

    def fold(x, op):
        return op(x.reshape(tk // FOLD_ROWS, FOLD_ROWS, tq), axis=0)

    def count_ge(thr):
        def body(kb, c):
            return c + fold(jnp.where(key_ref[kb] >= thr, 1.0, 0.0), jnp.sum)
        c = lax.fori_loop(0, n_kb, body, jnp.zeros((FOLD_ROWS, tq), F32))
        return jnp.sum(c, axis=0, keepdims=True)

    def bisect_cond(st):
        it, _, _, _, _, pending = st
        return jnp.logical_and(it < 32, jnp.max(pending) > 0.0)

    def probe(st):
        it, lo, hi, c_lo, c_hi, pending = st
        mid = (lo >> 1) + (hi >> 1) + (lo & hi & 1)
        c = count_ge(mid)
        live = pending > 0.0
        up = jnp.logical_and(live, c >= ksel)
        down = jnp.logical_and(live, c < ksel)
        c_lo = jnp.where(up, c, c_lo)
        return (it + 1, jnp.where(up, mid, lo), jnp.where(down, mid, hi), c_lo, jnp.where(down, c, c_hi),
                jnp.where(c_lo == ksel, 0.0, pending))

    def bisect_body(st):
        return probe(probe(st))

    lo0 = jnp.full((1, tq), NEG_INF_KEY + 1, I32)
    hi0 = jnp.full((1, tq), INT_MAX, I32)
    zero = jnp.zeros((1, tq), F32)
    pending0 = jnp.where(qpos + 1 > ksel, 1.0, 0.0)
    _, thr, _, c_thr, c_above, _ = lax.while_loop(bisect_cond, bisect_body, (0, lo0, hi0, zero, zero, pending0))

    need = ksel - c_above
    has_excess = jnp.max(jnp.where(c_thr > ksel, 1.0, 0.0)) > 0.0

    @pl.when(has_excess)
    def _():
        lower = jnp.where(lax.broadcasted_iota(I32, (tk, tk), 0) >= lax.broadcasted_iota(I32, (tk, tk), 1),
                          1.0, 0.0).astype(BF16)

        def body(kb, seen):
            keys = key_ref[kb]
            tie = keys == thr
            tie_f = jnp.where(tie, 1.0, 0.0)
            prefix = jnp.dot(lower, tie_f.astype(BF16), preferred_element_type=F32) + seen
            drop = jnp.logical_and(tie, prefix > need)
            key_ref[kb] = jnp.where(drop, NEG_INF_KEY, keys)
            return seen + jnp.sum(tie_f, axis=0, keepdims=True)

        lax.fori_loop(0, n_kb, body, jnp.zeros((1, tq), F32))

    for acc_ref in acc_refs:
        acc_ref[...] = jnp.zeros(acc_ref.shape, F32)
    grp = ATT_HEADS // ATT_KV_HEADS

    def att_body(kb, carry):
        m_all, l_all = carry
        k0 = pl.multiple_of(kb * tk, tk)
        kblk = k_ref[pl.ds(k0, tk), :]
        vtb = vt_ref[kb]
        bias_ref[...] = jnp.where(key_ref[kb] >= thr, 0.0, -1e30)
        m_rows, l_rows = [], []

        def logits(hd):
            g = hd // grp
            return jnp.dot(kblk[:, g * ATT_HEAD_DIM:(g + 1) * ATT_HEAD_DIM],
                           qt_ref[hd * ATT_HEAD_DIM:(hd + 1) * ATT_HEAD_DIM, :],
                           preferred_element_type=F32) + bias_ref[...]

        def accumulate(hd, alpha, p):
            g = hd // grp
            acc_ref = acc_refs[hd]
            acc_ref[...] = alpha * acc_ref[...] + jnp.dot(vtb[g * ATT_HEAD_DIM:(g + 1) * ATT_HEAD_DIM, :], p,
                                                          preferred_element_type=F32)

        queue = [logits(hd) for hd in range(DSA_AHEAD)]
        pending = None
        for hd in range(ATT_HEADS):
            s = queue.pop(0)
            if hd + DSA_AHEAD < ATT_HEADS:
                queue.append(logits(hd + DSA_AHEAD))
            m_old = m_all[hd:hd + 1, :]
            m_new = jnp.maximum(m_old, jnp.max(fold(s, jnp.max), axis=0, keepdims=True))
            p = jnp.exp(s - m_new)
            alpha = jnp.exp(m_old - m_new)
            l_rows.append(alpha * l_all[hd:hd + 1, :] + jnp.sum(fold(p, jnp.sum), axis=0, keepdims=True))
            m_rows.append(m_new)
            if pending is not None:
                accumulate(*pending)
            pending = (hd, alpha, p.astype(BF16))
        accumulate(*pending)
        return jnp.concatenate(m_rows, axis=0), jnp.concatenate(l_rows, axis=0)

    m0 = jnp.full((ATT_HEADS, tq), -1e30, F32)
    _, l_fin = lax.fori_loop(0, n_kb, att_body, (m0, jnp.zeros((ATT_HEADS, tq), F32)))
    for pair in range(ATT_HEADS // 2):
        rows = [acc_refs[hd][...] / l_fin[hd:hd + 1, :] for hd in (2 * pair, 2 * pair + 1)]
        o_ref[:, pair * LANES:(pair + 1) * LANES] = jnp.concatenate(rows, axis=0).T.astype(o_ref.dtype)


def _dsa(qt, iqt, iwt, k, vt, ik, b, t, tq, tk):
    n = b * t
    nq = t // tq
    nkb = t // tk
    ksel = min(TOPK_MAX, t // 4)
    qcol = lambda bi, i: (0, bi * nq + i)
    brow = lambda bi, i: (bi, 0)
    vtb = vt.reshape(ATT_KV_W, n // tk, tk).transpose(1, 0, 2)
    return pl.pallas_call(
        functools.partial(_dsa_body, tq=tq, tk=tk, ksel=ksel),
        grid=(b, nq),
        in_specs=[
            pl.BlockSpec((ATT_Q_W, tq), qcol),
            pl.BlockSpec((IDX_Q_W, tq), qcol),
            pl.BlockSpec((IDX_HEADS, tq), qcol),
            pl.BlockSpec((t, ATT_KV_W), brow),
            pl.BlockSpec((nkb, ATT_KV_W, tk), lambda bi, i: (bi, 0, 0)),
            pl.BlockSpec((t, IDX_HEAD_DIM), brow),
        ],
        out_specs=pl.BlockSpec((tq, ATT_Q_W), lambda bi, i: (bi * nq + i, 0)),
        out_shape=jax.ShapeDtypeStruct((n, ATT_Q_W), BF16),
        scratch_shapes=[
            pltpu.VMEM((nkb, tk, tq), I32),
            pltpu.VMEM((tk, tq), F32),
        ] + [pltpu.VMEM((ATT_HEAD_DIM, tq), F32) for _ in range(ATT_HEADS)],
        compiler_params=pltpu.CompilerParams(dimension_semantics=("parallel", "arbitrary"),
                                             vmem_limit_bytes=VMEM_LIMIT),
    )(qt, iqt, iwt, k, vtb, ik)


def _deltanet_body(x_ref, dz_ref, dab_ref, dabt_ref, cw_ref, alog_r_ref, bias_r_ref, alog_c_ref, bias_c_ref,
                   og_ref, y_ref, carry_ref, state_ref, *, tb):
    c = DN_CHUNK
    d = DN_HEAD_DIM
    nh = DN_HEADS

    @pl.when(pl.program_id(1) == 0)
    def _():
        carry_ref[...] = jnp.zeros(carry_ref.shape, F32)
        state_ref[...] = jnp.zeros(state_ref.shape, F32)

    xb = x_ref[...]
    xx = jnp.concatenate([carry_ref[...], xb], axis=0)
    cw = cw_ref[...]
    off = SUBLANES - (DN_CONV - 1)
    conv = cw[0:1] * xx[off:off + tb]
    for j in range(1, DN_CONV):
        conv = conv + cw[j:j + 1] * xx[off + j:off + j + tb]
    carry_ref[...] = xb[tb - SUBLANES:tb]
    qkv = conv * _sigmoid(conv)

    dab = dab_ref[...]
    g_col = -jnp.exp(alog_r_ref[...]) * _softplus(dab[:, 0:nh] + bias_r_ref[...])
    beta_col = _sigmoid(dab[:, nh:2 * nh])
    g_row = -jnp.exp(alog_c_ref[...]) * _softplus(dabt_ref[0:nh, :] + bias_c_ref[...])

    ri = lax.broadcasted_iota(I32, (tb, tb), 0)
    ci = lax.broadcasted_iota(I32, (tb, tb), 1)
    same_chunk = (ri // c) == (ci // c)
    lower_blk = jnp.where(jnp.logical_and(same_chunk, ri >= ci), 1.0, 0.0).astype(BF16)
    upper_blk = jnp.where(jnp.logical_and(same_chunk, ri <= ci), 1.0, 0.0).astype(BF16)
    gc_col = _dot_exact_lhs(lower_blk, g_col)
    gc_row = _dot_exact_rhs(g_row, upper_blk)

    r64 = lax.broadcasted_iota(I32, (c, c), 0)
    c64 = lax.broadcasted_iota(I32, (c, c), 1)
    tri = r64 >= c64
    strict = r64 > c64
    eye = jnp.where(r64 == c64, 1.0, 0.0)
    og = og_ref[...]

    inst = [(ch, hd) for ch in range(tb // c) for hd in range(nh)]
    rows = lambda ch: slice(ch * c, (ch + 1) * c)
    qs, ks, vs, bcols, gcols, decays, kbetas = [], [], [], [], [], [], []
    for ch, hd in inst:
        rs = rows(ch)
        qc = qkv[rs, hd * d:(hd + 1) * d]
        kc = qkv[rs, nh * d + hd * d:nh * d + (hd + 1) * d]
        qs.append(qc * lax.rsqrt(jnp.sum(qc * qc, axis=-1, keepdims=True) + EPS) * (d ** -0.5))
        ks.append(kc * lax.rsqrt(jnp.sum(kc * kc, axis=-1, keepdims=True) + EPS))
        vs.append(qkv[rs, 2 * nh * d + hd * d:2 * nh * d + (hd + 1) * d])
        bcols.append(beta_col[rs, hd:hd + 1])
        gcols.append(gc_col[rs, hd:hd + 1])
        decays.append(jnp.exp(jnp.where(tri, gcols[-1] - gc_row[hd:hd + 1, rs], -jnp.inf)))
        kbetas.append(ks[-1] * bcols[-1])
    kk = [_dot_nt(kbetas[i], ks[i]) for i in range(len(inst))]
    qk = [_dot_nt(qs[i], ks[i]) for i in range(len(inst))]
    a_mats = [jnp.where(strict, kk[i] * decays[i], 0.0) for i in range(len(inst))]
    intras = [jnp.where(tri, qk[i] * decays[i], 0.0) for i in range(len(inst))]
    xs = [eye - a for a in a_mats]
    ps = a_mats
    for _ in range(5):
        ps = [_dot_hi(p, p) for p in ps]
        xs = [x + _dot_hi(x, p) for x, p in zip(xs, ps)]
    egs = [jnp.exp(g) for g in gcols]
    sols = [_dot(xs[i], jnp.concatenate([vs[i] * bcols[i], kbetas[i] * egs[i]], axis=1)) for i in range(len(inst))]
    qes = [qs[i] * egs[i] for i in range(len(inst))]
    glasts = [g[c - 1:c] for g in gcols]
    kdecs = [ks[i] * jnp.exp(glasts[i] - gcols[i]) for i in range(len(inst))]

    states = [state_ref[hd] for hd in range(nh)]
    for ch in range(tb // c):
        ii = [ch * nh + hd for hd in range(nh)]
        both = [_dot(jnp.concatenate([sols[i][:, d:], qes[i]], axis=0), states[hd]) for hd, i in enumerate(ii)]
        v_new = [sols[i][:, :d] - both[hd][:c] for hd, i in enumerate(ii)]
        outs = [both[hd][c:] + _dot(intras[i], v_new[hd]) for hd, i in enumerate(ii)]
        states = [states[hd] * jnp.exp(glasts[i]) + lax.dot_general(
            kdecs[i].astype(BF16), v_new[hd].astype(BF16), TN_DIMS, preferred_element_type=F32)
            for hd, i in enumerate(ii)]
        for hd in range(nh):
            ls = slice(hd * d, (hd + 1) * d)
            o = outs[hd]
            on = o * lax.rsqrt(jnp.mean(o * o, axis=-1, keepdims=True) + EPS) * og
            z = dz_ref[rows(ch), ls]
            y_ref[rows(ch), ls] = (on * (z * _sigmoid(z))).astype(y_ref.dtype)
    for hd in range(nh):
        state_ref[hd] = states[hd]


def _deltanet(dqkv, dz, dab, dabt, conv_w, a_log, dt_bias, out_gain, b, t, tb):
    n = b * t
    nb = t // tb
    row = lambda bi, j: (bi * nb + j, 0)
    col = lambda bi, j: (0, bi * nb + j)
    fixed = lambda bi, j: (0, 0)
    nh = DN_HEADS
    return pl.pallas_call(
        functools.partial(_deltanet_body, tb=tb),
        grid=(b, nb),
        in_specs=[
            pl.BlockSpec((tb, 3 * DN_W), row),
            pl.BlockSpec((tb, DN_W), row),
            pl.BlockSpec((tb, 2 * nh), row),
            pl.BlockSpec((2 * nh, tb), col),
            pl.BlockSpec((DN_CONV, 3 * DN_W), fixed),
            pl.BlockSpec((1, nh), fixed),
            pl.BlockSpec((1, nh), fixed),
            pl.BlockSpec((nh, 1), fixed),
            pl.BlockSpec((nh, 1), fixed),
            pl.BlockSpec((1, DN_HEAD_DIM), fixed),
        ],
        out_specs=pl.BlockSpec((tb, DN_W), row),
        out_shape=jax.ShapeDtypeStruct((n, DN_W), BF16),
        scratch_shapes=[
            pltpu.VMEM((SUBLANES, 3 * DN_W), F32),
            pltpu.VMEM((nh, DN_HEAD_DIM, DN_HEAD_DIM), F32),
        ],
        compiler_params=pltpu.CompilerParams(dimension_semantics=("parallel", "arbitrary"),
                                             vmem_limit_bytes=VMEM_LIMIT),
    )(dqkv, dz, dab, dabt, conv_w, a_log.reshape(1, nh), dt_bias.reshape(1, nh),
      a_log.reshape(nh, 1), dt_bias.reshape(nh, 1), out_gain.reshape(1, -1))


def _merge_body(x_ref, ya_ref, yd_ref, gab_ref, wa_ref, wb_ref, wo_ref, o_ref):
    dm = x_ref.shape[1]
    gab = gab_ref[...]
    a = jnp.dot(ya_ref[...], wa_ref[...], preferred_element_type=F32)
    bb = jnp.dot(yd_ref[...], wb_ref[...], preferred_element_type=F32)
    merged = _sigmoid(gab[:, :dm]) * a + _sigmoid(gab[:, dm:]) * bb
    o_ref[...] = x_ref[...] + jnp.dot(merged.astype(BF16), wo_ref[...], preferred_element_type=F32)


def _merge(xf, y_att, y_dn, gab, w_a, w_b, w_o, tm):
    n, d = xf.shape
    row = lambda i: (i, 0)
    fixed = lambda i: (0, 0)
    return pl.pallas_call(
        _merge_body,
        grid=(n // tm,),
        in_specs=[
            pl.BlockSpec((tm, d), row),
            pl.BlockSpec((tm, ATT_Q_W), row),
            pl.BlockSpec((tm, DN_W), row),
            pl.BlockSpec((tm, 2 * d), row),
            pl.BlockSpec((ATT_Q_W, d), fixed),
            pl.BlockSpec((DN_W, d), fixed),
            pl.BlockSpec((d, d), fixed),
        ],
        out_specs=pl.BlockSpec((tm, d), row),
        out_shape=jax.ShapeDtypeStruct((n, d), F32),
        compiler_params=pltpu.CompilerParams(dimension_semantics=("parallel",), vmem_limit_bytes=VMEM_LIMIT),
    )(xf, y_att, y_dn, gab, w_a.astype(BF16), w_b.astype(BF16), w_o.astype(BF16))


def _top16_rows(s, payload=None):
    rows = lax.broadcasted_iota(I32, s.shape, 0)
    big = s.shape[0]
    vals, pays = [], []
    for _ in range(PEER_TOPK):
        m = jnp.max(s, axis=0, keepdims=True)
        am = jnp.min(jnp.where(s == m, rows, big), axis=0, keepdims=True)
        hit = rows == am
        vals.append(m)
        if payload is None:
            pays.append(am)
        else:
            pays.append(jnp.sum(jnp.where(hit, payload, 0), axis=0, keepdims=True))
        s = jnp.where(hit, -jnp.inf, s)
    return jnp.concatenate(vals, axis=0), jnp.concatenate(pays, axis=0)


def _pair_candidates(s0, i0, s1, i1):
    k = PEER_TOPK
    rows8 = lax.broadcasted_iota(I32, (SUBLANES, s0.shape[1]), 0)
    cs = [s0[0:1] + s1]
    ce = [i0[0:1] * PEER_N_KEYS + i1]
    for i in range(1, SUBLANES):
        valid = rows8 < k // (i + 1)
        cs.append(jnp.where(valid, s0[i:i + 1] + s1[0:SUBLANES], -jnp.inf))
        ce.append(i0[i:i + 1] * PEER_N_KEYS + i1[0:SUBLANES])
    cs.append(s0[SUBLANES:k] + s1[0:1])
    ce.append(i0[SUBLANES:k] * PEER_N_KEYS + i1[0:1])
    return jnp.concatenate(cs, axis=0), jnp.concatenate(ce, axis=0)


def _peer_route_body(x_ref, g2_ref, wq_ref, sk_ref, ids_ref, gates_ref, *, tm):
    x = x_ref[...]
    ms = jnp.mean(x * x, axis=-1, keepdims=True)
    h = (x * lax.rsqrt(ms + EPS) * g2_ref[...]).astype(BF16)
    q = jnp.dot(h, wq_ref[...], preferred_element_type=F32).astype(BF16)
    half = PEER_KEY_DIM // 2
    for hd in range(PEER_HEADS):
        tops = []
        for p in range(2):
            c0 = hd * PEER_KEY_DIM + p * half
            st = lax.dot_general(sk_ref[2 * hd + p], q[:, c0:c0 + half], NT_DIMS,
                                 preferred_element_type=F32)
            tops.append(_top16_rows(st))
        (s0, i0), (s1, i1) = tops
        cand_s, cand_e = _pair_candidates(s0, i0, s1, i1)
        best, expert = _top16_rows(cand_s, cand_e)
        e = jnp.exp(best - best[0:1])
        gate = e / jnp.sum(e, axis=0, keepdims=True)
        rs = slice(hd * PEER_TOPK, (hd + 1) * PEER_TOPK)
        ids_ref[rs, :] = expert
        gates_ref[rs, :] = gate


def _peer_route(x1, g2, w_query, sub_keys, tm):
    n, d = x1.shape
    nsel = PEER_HEADS * PEER_TOPK
    half = PEER_KEY_DIM // 2
    sk = sub_keys.reshape(PEER_HEADS * 2, PEER_N_KEYS, half).astype(BF16)
    return pl.pallas_call(
        functools.partial(_peer_route_body, tm=tm),
        grid=(n // tm,),
        in_specs=[
            pl.BlockSpec((tm, d), lambda i: (i, 0)),
            pl.BlockSpec((1, d), lambda i: (0, 0)),
            pl.BlockSpec((d, PEER_HEADS * PEER_KEY_DIM), lambda i: (0, 0)),
            pl.BlockSpec((PEER_HEADS * 2, PEER_N_KEYS, half), lambda i: (0, 0, 0)),
        ],
        out_specs=(pl.BlockSpec((nsel, tm), lambda i: (0, i)),
                   pl.BlockSpec((nsel, tm), lambda i: (0, i))),
        out_shape=(jax.ShapeDtypeStruct((nsel, n), I32), jax.ShapeDtypeStruct((nsel, n), F32)),
        compiler_params=pltpu.CompilerParams(dimension_semantics=("parallel",), vmem_limit_bytes=VMEM_LIMIT),
    )(x1, g2.reshape(1, d), w_query.astype(BF16), sk)


PEER_SLOTS = 16
PEER_WAIT_GROUP = 4


def _peer_apply_body(ids_ref, x_ref, g2_ref, gates_ref, uv_hbm, o_ref, sems, *slot_refs, tb):
    bufs, coefs = slot_refs[:PEER_SLOTS], slot_refs[PEER_SLOTS:]
    nsel = PEER_HEADS * PEER_TOPK
    nchunk = x_ref.shape[1]
    dm = nchunk * LANES

    ngrp = nsel // SUBLANES

    def issue(tok, slot, k0, k1):
        for kk in range(k0, k1):
            pltpu.make_async_copy(uv_hbm.at[ids_ref[tok, kk]], bufs[slot].at[kk], sems.at[slot]).start(
                priority=kk % 2)

    def wait_all(slot):
        pltpu.make_async_copy(uv_hbm.at[pl.ds(0, nsel)], bufs[slot], sems.at[slot]).wait()

    eye = lax.broadcasted_iota(I32, (nsel, nsel), 0) == lax.broadcasted_iota(I32, (nsel, nsel), 1)
    sub = lax.broadcasted_iota(I32, (SUBLANES, LANES), 0)
    masks = {k: (sub & k) == 0 for k in (4, 2, 1)}
    g2 = g2_ref[...]

    def merge(x, y, k):
        if k == 4:
            return jnp.where(masks[k], x, y) + pltpu.roll(jnp.where(masks[k], y, x), k, axis=0)
        return jnp.where(masks[k], x + pltpu.roll(x, SUBLANES - k, axis=0), y + pltpu.roll(y, k, axis=0))

    order = (0, 4, 2, 6, 1, 5, 3, 7)

    def u_of(word):
        return lax.bitcast_convert_type(word << 16, F32)

    def v_of(word):
        return lax.bitcast_convert_type(word & jnp.uint32(0xFFFF0000), F32)

    def normed(tok):
        xt = x_ref[tok]
        ssq = jnp.sum(jnp.sum(xt * xt, axis=1, keepdims=True), axis=0, keepdims=True)
        return xt * lax.rsqrt(ssq * (1.0 / dm) + EPS) * g2

    def step(t, slot, nxt, nxt_slot, h8, t_after):
        def prefetch(k0, k1):
            if nxt is not None:
                issue(nxt, nxt_slot, k0, k1)

        if slot % PEER_WAIT_GROUP == 0:
            for j in range(PEER_WAIT_GROUP):
                wait_all(slot + j)
        grow = gates_ref[t]
        gcol = jnp.sum(jnp.where(eye, jnp.broadcast_to(grow, (nsel, nsel)), 0.0), axis=1, keepdims=True)
        per_grp = nsel // (2 * ngrp)
        groups = []
        for grp in range(ngrp):
            prefetch(grp * per_grp, (grp + 1) * per_grp)
            p = [u_of(bufs[slot][grp * SUBLANES + order.index(j)]) * h8 for j in range(SUBLANES)]
            q4 = [merge(p[2 * i], p[2 * i + 1], 4) for i in range(4)]
            q2 = [merge(q4[2 * i], q4[2 * i + 1], 2) for i in range(2)]
            groups.append(merge(q2[0], q2[1], 1))
        colsum = jnp.concatenate(groups, axis=0)
        act = jnp.sum(colsum, axis=1, keepdims=True)
        gelu = 0.5 * act * (1.0 + lax.erf(act * (2.0 ** -0.5)))
        coef_ref = coefs[slot]
        coef_ref[...] = jnp.broadcast_to(gcol * gelu, (nsel, LANES))
        h8_after = normed(t_after)
        acc = jnp.zeros((nchunk, LANES), F32)
        for kk in range(nsel):
            if kk % 2 == 0:
                prefetch(nsel // 2 + kk // 2, nsel // 2 + kk // 2 + 1)
            acc = acc + coef_ref[kk:kk + 1, :] * v_of(bufs[slot][kk])
        o_ref[t] = x_ref[t] + acc
        return h8_after

    ahead = PEER_SLOTS - 1
    for s in range(ahead):
        issue(s, s, 0, nsel)

    def group_body(g, h8):
        t0 = g * PEER_SLOTS
        for s in range(PEER_SLOTS):
            h8 = step(t0 + s, s, t0 + s + ahead, (s + ahead) % PEER_SLOTS, h8, t0 + s + 1)
        return h8

    h8 = lax.fori_loop(0, tb // PEER_SLOTS - 1, group_body, normed(0))
    t0 = tb - PEER_SLOTS
    h8 = step(t0, 0, t0 + ahead, ahead % PEER_SLOTS, h8, t0 + 1)
    for s in range(1, PEER_SLOTS):
        h8 = step(t0 + s, s, None, None, h8, min(t0 + s + 1, tb - 1))


def _peer_apply(x1, g2, ids, gates, peer_u, peer_v, tb):
    n, d = x1.shape
    nsel = PEER_HEADS * PEER_TOPK
    ne = peer_u.shape[0]
    nchunk = d // LANES

    def half_words(tab):
        return lax.bitcast_convert_type(tab.astype(BF16), jnp.uint16).astype(jnp.uint32)

    uv = ((half_words(peer_v) << 16) | half_words(peer_u)).reshape(ne, nchunk, LANES)
    out = pl.pallas_call(
        functools.partial(_peer_apply_body, tb=tb),
        grid=(n // tb,),
        in_specs=[
            pl.BlockSpec((tb, nsel), lambda i: (i, 0), memory_space=pltpu.SMEM),
            pl.BlockSpec((tb, nchunk, LANES), lambda i: (i, 0, 0)),
            pl.BlockSpec((nchunk, LANES), lambda i: (0, 0)),
            pl.BlockSpec((tb, 1, nsel), lambda i: (i, 0, 0)),
            pl.BlockSpec(memory_space=pl.ANY),
        ],
        out_specs=pl.BlockSpec((tb, nchunk, LANES), lambda i: (i, 0, 0)),
        out_shape=jax.ShapeDtypeStruct((n, nchunk, LANES), F32),
        scratch_shapes=[
            pltpu.SemaphoreType.DMA((PEER_SLOTS,)),
        ] + [pltpu.VMEM((nsel, nchunk, LANES), jnp.uint32) for _ in range(PEER_SLOTS)]
        + [pltpu.VMEM((nsel, LANES), F32) for _ in range(PEER_SLOTS)],
        compiler_params=pltpu.CompilerParams(dimension_semantics=("arbitrary",), vmem_limit_bytes=VMEM_LIMIT),
    )(ids, x1.reshape(n, nchunk, LANES), g2.reshape(nchunk, LANES), gates.reshape(n, 1, nsel), uv)
    return out.reshape(n, d)


def _block_sizes(t):
    return dict(
        tm_proj=256,
        tq=min(512, t),
        tk=min(512, t),
        tb_dn=min(256, t),
        tm_merge=512,
        tm_route=LANES,
        tb_peer=512,
    )


def kernel(x, norm1_gain, w_in, q_norm_gain, k_norm_gain, dn_conv_w, dn_a_log, dn_dt_bias, dn_out_norm_gain,
           w_att_branch, w_dn_branch, w_o, norm2_gain, peer_w_query, peer_sub_keys, peer_u, peer_v):
    b, t, d = x.shape
    n = b * t
    bs = _block_sizes(t)
    xf = x.reshape(n, d)
    for layer in range(w_in.shape[0]):
        (qt, iqt, vt, iwt, dabt, k, ik, dab, dqkv, dz, gab) = _in_proj(
            xf, norm1_gain[layer], w_in[layer], q_norm_gain[layer], k_norm_gain[layer], bs["tm_proj"])
        y_att = _dsa(qt, iqt, iwt, k, vt, ik, b, t, bs["tq"], bs["tk"])
        y_dn = _deltanet(dqkv, dz, dab, dabt, dn_conv_w[layer], dn_a_log[layer], dn_dt_bias[layer],
                         dn_out_norm_gain[layer], b, t, bs["tb_dn"])
        x1 = _merge(xf, y_att, y_dn, gab, w_att_branch[layer], w_dn_branch[layer], w_o[layer], bs["tm_merge"])
        ids_t, gates_t = _peer_route(x1, norm2_gain[layer], peer_w_query[layer], peer_sub_keys[layer],
                                     bs["tm_route"])
        xf = _peer_apply(x1, norm2_gain[layer], ids_t.T, gates_t.T, peer_u[layer], peer_v[layer], bs["tb_peer"])
    return xf.reshape(b, t, d)
```

```python
import functools

import jax
import jax.numpy as jnp
import numpy as np
from jax import lax
from jax.experimental import pallas as pl
from jax.experimental.pallas import tpu as pltpu

F32 = jnp.float32
BF16 = jnp.bfloat16
I32 = jnp.int32

ATT_HEADS = 8
ATT_KV_HEADS = 2
ATT_HEAD_DIM = 64
IDX_HEADS = 8
IDX_HEAD_DIM = 64
TOPK_MAX = 256
DN_HEADS = 4
DN_HEAD_DIM = 128
DN_CONV = 4
DN_CHUNK = 64
PEER_HEADS = 8
PEER_N_KEYS = 128
PEER_KEY_DIM = 256
PEER_TOPK = 16
EPS = 1e-6

ATT_Q_W = ATT_HEADS * ATT_HEAD_DIM
ATT_KV_W = ATT_KV_HEADS * ATT_HEAD_DIM
IDX_Q_W = IDX_HEADS * IDX_HEAD_DIM
DN_W = DN_HEADS * DN_HEAD_DIM

LANES = 128
SUBLANES = 8
VMEM_LIMIT = 56 * 1024 * 1024

NEG_INF_KEY = int(np.int32(np.uint32(0xFF800000) ^ np.uint32(0x7FFFFFFF)))
INT_MAX = int(np.iinfo(np.int32).max)
KEY_BITS = 32
MASKED_LOGIT = -1e30

NT_DIMS = (((1,), (1,)), ((), ()))
TN_DIMS = (((0,), (0,)), ((), ()))


def _sigmoid(x):
    return 1.0 / (1.0 + jnp.exp(-x))


def _softplus(x):
    return jnp.maximum(x, 0.0) + jnp.log(1.0 + jnp.exp(-jnp.abs(x)))


def _dot(a, b):
    return jnp.dot(a.astype(BF16), b.astype(BF16), preferred_element_type=F32)


def _dot_nt(a, b):
    return lax.dot_general(a.astype(BF16), b.astype(BF16), NT_DIMS, preferred_element_type=F32)


def _split2(x):
    hi = x.astype(BF16)
    lo = (x - hi.astype(F32)).astype(BF16)
    return hi, lo


def _dot_hi(a, b):
    a1, a2 = _split2(a)
    b1, b2 = _split2(b)
    out = jnp.dot(a1, b1, preferred_element_type=F32)
    out = out + jnp.dot(a1, b2, preferred_element_type=F32)
    out = out + jnp.dot(a2, b1, preferred_element_type=F32)
    return out


def _dot_exact_rhs(a, b_exact):
    a1 = a.astype(BF16)
    r1 = a - a1.astype(F32)
    a2 = r1.astype(BF16)
    a3 = (r1 - a2.astype(F32)).astype(BF16)
    out = jnp.dot(a1, b_exact, preferred_element_type=F32)
    out = out + jnp.dot(a2, b_exact, preferred_element_type=F32)
    out = out + jnp.dot(a3, b_exact, preferred_element_type=F32)
    return out


def _dot_exact_lhs(a_exact, b):
    b1 = b.astype(BF16)
    r1 = b - b1.astype(F32)
    b2 = r1.astype(BF16)
    b3 = (r1 - b2.astype(F32)).astype(BF16)
    out = jnp.dot(a_exact, b1, preferred_element_type=F32)
    out = out + jnp.dot(a_exact, b2, preferred_element_type=F32)
    out = out + jnp.dot(a_exact, b3, preferred_element_type=F32)
    return out


C_AK = 0
C_SM = C_AK + ATT_KV_W
C_DQKV = C_SM + LANES
C_DZ = C_DQKV + 3 * DN_W
C_GAB = C_DZ + DN_W
SM_DAB = IDX_HEAD_DIM
R_AQ = 0
R_IQ = R_AQ + ATT_Q_W
R_AV = R_IQ + IDX_Q_W
R_IW = R_AV + ATT_KV_W
R_DAB = R_IW + IDX_HEADS
R_END = R_DAB + 2 * DN_HEADS


def _in_proj_body(x_ref, g1_ref, w_ref, wt_ref, qg_ref, kg_ref,
                  qt_ref, iqt_ref, vt_ref, iwt_ref, dabt_ref, k_ref, ik_ref, dab_ref, dqkv_ref, dz_ref, gab_ref):
    x = x_ref[...]
    ms = jnp.mean(x * x, axis=-1, keepdims=True)
    h = (x * lax.rsqrt(ms + EPS) * g1_ref[...]).astype(BF16)

    rt = lax.dot_general(wt_ref[...], h, NT_DIMS, preferred_element_type=F32)
    qg = qg_ref[...] * (ATT_HEAD_DIM ** -0.5)
    for hd in range(ATT_HEADS):
        sl = slice(R_AQ + hd * ATT_HEAD_DIM, R_AQ + (hd + 1) * ATT_HEAD_DIM)
        blk = rt[sl]
        qt_ref[sl, :] = (blk * lax.rsqrt(jnp.mean(blk * blk, axis=0, keepdims=True) + EPS) * qg).astype(qt_ref.dtype)
    iqt_ref[...] = rt[R_IQ:R_AV].astype(iqt_ref.dtype)
    vt_ref[...] = rt[R_AV:R_IW].astype(vt_ref.dtype)
    iwt_ref[...] = rt[R_IW:R_DAB]
    dabt_ref[...] = rt[R_DAB:R_END]

    def proj(c0, c1):
        return jnp.dot(h, w_ref[:, c0:c1], preferred_element_type=F32)

    ak = proj(C_AK, C_SM)
    kg = kg_ref[...]
    for hd in range(ATT_KV_HEADS):
        sl = slice(hd * ATT_HEAD_DIM, (hd + 1) * ATT_HEAD_DIM)
        blk = ak[:, sl]
        k_ref[:, sl] = (blk * lax.rsqrt(jnp.mean(blk * blk, axis=-1, keepdims=True) + EPS) * kg).astype(k_ref.dtype)
    sm = proj(C_SM, C_DQKV)
    ik_ref[...] = sm[:, :IDX_HEAD_DIM].astype(ik_ref.dtype)
    dab_ref[...] = sm[:, SM_DAB:SM_DAB + 2 * DN_HEADS]
    dqkv_ref[...] = proj(C_DQKV, C_DZ)
    dz_ref[...] = proj(C_DZ, C_GAB)
    gab_ref[...] = proj(C_GAB, C_GAB + 2 * x.shape[1])


def _in_proj(xf, g1, w_in, q_gain, k_gain, tm):
    n, d = xf.shape
    cuts = np.cumsum([ATT_Q_W, ATT_KV_W, ATT_KV_W, IDX_Q_W, IDX_HEAD_DIM, IDX_HEADS,
                      DN_W, DN_W, DN_W, DN_W, DN_HEADS, DN_HEADS, d, d])[:-1].tolist()
    aq, ak, av, iq, ik, iw, dq, dk, dv, dz, da, db, ga, gb = jnp.split(w_in, cuts, axis=-1)
    pad = jnp.zeros((d, LANES - IDX_HEAD_DIM - 2 * DN_HEADS), w_in.dtype)
    w_all = jnp.concatenate([ak, ik, da, db, pad, dq, dk, dv, dz, ga, gb], axis=-1).astype(BF16)
    wt_all = jnp.concatenate([aq, iq, av, iw, da, db], axis=-1).T.astype(BF16)
    wtot = w_all.shape[1]
    row = lambda i: (i, 0)
    col = lambda i: (0, i)
    fixed = lambda i: (0, 0)
    out_shapes = (
        jax.ShapeDtypeStruct((ATT_Q_W, n), BF16),
        jax.ShapeDtypeStruct((IDX_Q_W, n), BF16),
        jax.ShapeDtypeStruct((ATT_KV_W, n), BF16),
        jax.ShapeDtypeStruct((IDX_HEADS, n), F32),
        jax.ShapeDtypeStruct((2 * DN_HEADS, n), F32),
        jax.ShapeDtypeStruct((n, ATT_KV_W), BF16),
        jax.ShapeDtypeStruct((n, IDX_HEAD_DIM), BF16),
        jax.ShapeDtypeStruct((n, 2 * DN_HEADS), F32),
        jax.ShapeDtypeStruct((n, 3 * DN_W), F32),
        jax.ShapeDtypeStruct((n, DN_W), F32),
        jax.ShapeDtypeStruct((n, 2 * d), F32),
    )
    out_specs = tuple(
        pl.BlockSpec((s.shape[0], tm), col) if s.shape[0] != n else pl.BlockSpec((tm, s.shape[1]), row)
        for s in out_shapes)
    return pl.pallas_call(
        _in_proj_body,
        grid=(n // tm,),
        in_specs=[
            pl.BlockSpec((tm, d), row),
            pl.BlockSpec((1, d), fixed),
            pl.BlockSpec((d, wtot), fixed),
            pl.BlockSpec((R_END, d), fixed),
            pl.BlockSpec((ATT_HEAD_DIM, 1), fixed),
            pl.BlockSpec((1, ATT_HEAD_DIM), fixed),
        ],
        out_specs=out_specs,
        out_shape=out_shapes,
        compiler_params=pltpu.CompilerParams(dimension_semantics=("parallel",), vmem_limit_bytes=VMEM_LIMIT),
    )(xf, g1.reshape(1, d), w_all, wt_all, q_gain.reshape(-1, 1), k_gain.reshape(1, -1))


FOLD_ROWS = 4 * SUBLANES
DSA_AHEAD = 2


def _dsa_body(qt_ref, iqt_ref, iwt_ref, k_ref, vt_ref, ik_ref, o_ref,
              key_ref, bias_ref, *acc_refs, tq, tk, ksel):
    i = pl.program_id(1)
    q0 = i * tq
    n_kb = (q0 + tq + tk - 1) // tk
    qpos = q0 + lax.broadcasted_iota(I32, (1, tq), 1)
    iwt = iwt_ref[...]

    def score_body(kb, carry):
        k0 = pl.multiple_of(kb * tk, tk)
        ikb = ik_ref[pl.ds(k0, tk), :]
        acc = jnp.zeros((tk, tq), F32)
        for hd in range(IDX_HEADS):
            sl = slice(hd * IDX_HEAD_DIM, (hd + 1) * IDX_HEAD_DIM)
            dots = jnp.dot(ikb, iqt_ref[sl, :], preferred_element_type=F32)
            acc = acc + iwt[hd:hd + 1, :] * jnp.maximum(dots, 0.0)
        bits = lax.bitcast_convert_type(acc, I32)
        keys = jnp.where(bits >= 0, bits, bits ^ INT_MAX)
        kpos = k0 + lax.broadcasted_iota(I32, (tk, 1), 0)
        key_ref[kb] = jnp.where(kpos <= qpos, keys, NEG_INF_KEY)
        return carry

    lax.fori_loop(0, n_kb, score_body, 0)

    def fold(x, op):
        return op(x.reshape(tk // FOLD_ROWS, FOLD_ROWS, tq), axis=0)

    def count_ge(thr):
        def body(kb, c):
            return c + fold(jnp.where(key_ref[kb] >= thr, 1.0, 0.0), jnp.sum)
        c = lax.fori_loop(0, n_kb, body, jnp.zeros((FOLD_ROWS, tq), F32))
        return jnp.sum(c, axis=0, keepdims=True)

    def bisect_cond(st):
        it, _, _, _, _, pending = st
        return jnp.logical_and(it < KEY_BITS, jnp.max(pending) > 0.0)

    def probe(st):
        it, lo, hi, c_lo, c_hi, pending = st
        mid = (lo >> 1) + (hi >> 1) + (lo & hi & 1)
        c = count_ge(mid)
        live = pending > 0.0
        up = jnp.logical_and(live, c >= ksel)
        down = jnp.logical_and(live, c < ksel)
        c_lo = jnp.where(up, c, c_lo)
        return (it + 1, jnp.where(up, mid, lo), jnp.where(down, mid, hi), c_lo, jnp.where(down, c, c_hi),
                jnp.where(c_lo == ksel, 0.0, pending))

    def bisect_body(st):
        return probe(probe(st))

    lo0 = jnp.full((1, tq), NEG_INF_KEY + 1, I32)
    hi0 = jnp.full((1, tq), INT_MAX, I32)
    zero = jnp.zeros((1, tq), F32)
    pending0 = jnp.where(qpos + 1 > ksel, 1.0, 0.0)
    _, thr, _, c_thr, c_above, _ = lax.while_loop(bisect_cond, bisect_body, (0, lo0, hi0, zero, zero, pending0))

    need = ksel - c_above
    has_excess = jnp.max(jnp.where(c_thr > ksel, 1.0, 0.0)) > 0.0

    @pl.when(has_excess)
    def _():
        lower = jnp.where(lax.broadcasted_iota(I32, (tk, tk), 0) >= lax.broadcasted_iota(I32, (tk, tk), 1),
                          1.0, 0.0).astype(BF16)

        def body(kb, seen):
            keys = key_ref[kb]
            tie = keys == thr
            tie_f = jnp.where(tie, 1.0, 0.0)
            prefix = jnp.dot(lower, tie_f.astype(BF16), preferred_element_type=F32) + seen
            drop = jnp.logical_and(tie, prefix > need)
            key_ref[kb] = jnp.where(drop, NEG_INF_KEY, keys)
            return seen + jnp.sum(tie_f, axis=0, keepdims=True)

        lax.fori_loop(0, n_kb, body, jnp.zeros((1, tq), F32))

    for acc_ref in acc_refs:
        acc_ref[...] = jnp.zeros(acc_ref.shape, F32)
    grp = ATT_HEADS // ATT_KV_HEADS

    def att_body(kb, carry):
        m_all, l_all = carry
        k0 = pl.multiple_of(kb * tk, tk)
        kblk = k_ref[pl.ds(k0, tk), :]
        vtb = vt_ref[kb]
        bias_ref[...] = jnp.where(key_ref[kb] >= thr, 0.0, MASKED_LOGIT)
        m_rows, l_rows = [], []

        def logits(hd):
            g = hd // grp
            return jnp.dot(kblk[:, g * ATT_HEAD_DIM:(g + 1) * ATT_HEAD_DIM],
                           qt_ref[hd * ATT_HEAD_DIM:(hd + 1) * ATT_HEAD_DIM, :],
                           preferred_element_type=F32) + bias_ref[...]

        def accumulate(hd, alpha, p):
            g = hd // grp
            acc_ref = acc_refs[hd]
            acc_ref[...] = alpha * acc_ref[...] + jnp.dot(vtb[g * ATT_HEAD_DIM:(g + 1) * ATT_HEAD_DIM, :], p,
                                                          preferred_element_type=F32)

        queue = [logits(hd) for hd in range(DSA_AHEAD)]
        pending = None
        for hd in range(ATT_HEADS):
            s = queue.pop(0)
            if hd + DSA_AHEAD < ATT_HEADS:
                queue.append(logits(hd + DSA_AHEAD))
            m_old = m_all[hd:hd + 1, :]
            m_new = jnp.maximum(m_old, jnp.max(fold(s, jnp.max), axis=0, keepdims=True))
            p = jnp.exp(s - m_new)
            alpha = jnp.exp(m_old - m_new)
            l_rows.append(alpha * l_all[hd:hd + 1, :] + jnp.sum(fold(p, jnp.sum), axis=0, keepdims=True))
            m_rows.append(m_new)
            if pending is not None:
                accumulate(*pending)
            pending = (hd, alpha, p.astype(BF16))
        accumulate(*pending)
        return jnp.concatenate(m_rows, axis=0), jnp.concatenate(l_rows, axis=0)

    m0 = jnp.full((ATT_HEADS, tq), MASKED_LOGIT, F32)
    _, l_fin = lax.fori_loop(0, n_kb, att_body, (m0, jnp.zeros((ATT_HEADS, tq), F32)))
    for pair in range(ATT_HEADS // 2):
        rows = [acc_refs[hd][...] / l_fin[hd:hd + 1, :] for hd in (2 * pair, 2 * pair + 1)]
        o_ref[:, pair * LANES:(pair + 1) * LANES] = jnp.concatenate(rows, axis=0).T.astype(o_ref.dtype)


def _dsa(qt, iqt, iwt, k, vt, ik, b, t, tq, tk):
    n = b * t
    nq = t // tq
    nkb = t // tk
    ksel = min(TOPK_MAX, t // 4)
    qcol = lambda bi, i: (0, bi * nq + i)
    brow = lambda bi, i: (bi, 0)
    vtb = vt.reshape(ATT_KV_W, n // tk, tk).transpose(1, 0, 2)
    return pl.pallas_call(
        functools.partial(_dsa_body, tq=tq, tk=tk, ksel=ksel),
        grid=(b, nq),
        in_specs=[
            pl.BlockSpec((ATT_Q_W, tq), qcol),
            pl.BlockSpec((IDX_Q_W, tq), qcol),
            pl.BlockSpec((IDX_HEADS, tq), qcol),
            pl.BlockSpec((t, ATT_KV_W), brow),
            pl.BlockSpec((nkb, ATT_KV_W, tk), lambda bi, i: (bi, 0, 0)),
            pl.BlockSpec((t, IDX_HEAD_DIM), brow),
        ],
        out_specs=pl.BlockSpec((tq, ATT_Q_W), lambda bi, i: (bi * nq + i, 0)),
        out_shape=jax.ShapeDtypeStruct((n, ATT_Q_W), BF16),
        scratch_shapes=[
            pltpu.VMEM((nkb, tk, tq), I32),
            pltpu.VMEM((tk, tq), F32),
        ] + [pltpu.VMEM((ATT_HEAD_DIM, tq), F32) for _ in range(ATT_HEADS)],
        compiler_params=pltpu.CompilerParams(dimension_semantics=("parallel", "arbitrary"),
                                             vmem_limit_bytes=VMEM_LIMIT),
    )(qt, iqt, iwt, k, vtb, ik)


def _deltanet_body(x_ref, dz_ref, dab_ref, dabt_ref, cw_ref, alog_r_ref, bias_r_ref, alog_c_ref, bias_c_ref,
                   og_ref, y_ref, carry_ref, state_ref, *, tb):
    c = DN_CHUNK
    d = DN_HEAD_DIM
    nh = DN_HEADS

    @pl.when(pl.program_id(1) == 0)
    def _():
        carry_ref[...] = jnp.zeros(carry_ref.shape, F32)
        state_ref[...] = jnp.zeros(state_ref.shape, F32)

    xb = x_ref[...]
    xx = jnp.concatenate([carry_ref[...], xb], axis=0)
    cw = cw_ref[...]
    off = SUBLANES - (DN_CONV - 1)
    conv = cw[0:1] * xx[off:off + tb]
    for j in range(1, DN_CONV):
        conv = conv + cw[j:j + 1] * xx[off + j:off + j + tb]
    carry_ref[...] = xb[tb - SUBLANES:tb]
    qkv = conv * _sigmoid(conv)

    dab = dab_ref[...]
    g_col = -jnp.exp(alog_r_ref[...]) * _softplus(dab[:, 0:nh] + bias_r_ref[...])
    beta_col = _sigmoid(dab[:, nh:2 * nh])
    g_row = -jnp.exp(alog_c_ref[...]) * _softplus(dabt_ref[0:nh, :] + bias_c_ref[...])

    ri = lax.broadcasted_iota(I32, (tb, tb), 0)
    ci = lax.broadcasted_iota(I32, (tb, tb), 1)
    same_chunk = (ri // c) == (ci // c)
    lower_blk = jnp.where(jnp.logical_and(same_chunk, ri >= ci), 1.0, 0.0).astype(BF16)
    upper_blk = jnp.where(jnp.logical_and(same_chunk, ri <= ci), 1.0, 0.0).astype(BF16)
    gc_col = _dot_exact_lhs(lower_blk, g_col)
    gc_row = _dot_exact_rhs(g_row, upper_blk)

    r64 = lax.broadcasted_iota(I32, (c, c), 0)
    c64 = lax.broadcasted_iota(I32, (c, c), 1)
    tri = r64 >= c64
    strict = r64 > c64
    eye = jnp.where(r64 == c64, 1.0, 0.0)
    og = og_ref[...]

    inst = [(ch, hd) for ch in range(tb // c) for hd in range(nh)]
    rows = lambda ch: slice(ch * c, (ch + 1) * c)
    qs, ks, vs, bcols, gcols, decays, kbetas = [], [], [], [], [], [], []
    for ch, hd in inst:
        rs = rows(ch)
        qc = qkv[rs, hd * d:(hd + 1) * d]
        kc = qkv[rs, nh * d + hd * d:nh * d + (hd + 1) * d]
        qs.append(qc * lax.rsqrt(jnp.sum(qc * qc, axis=-1, keepdims=True) + EPS) * (d ** -0.5))
        ks.append(kc * lax.rsqrt(jnp.sum(kc * kc, axis=-1, keepdims=True) + EPS))
        vs.append(qkv[rs, 2 * nh * d + hd * d:2 * nh * d + (hd + 1) * d])
        bcols.append(beta_col[rs, hd:hd + 1])
        gcols.append(gc_col[rs, hd:hd + 1])
        decays.append(jnp.exp(jnp.where(tri, gcols[-1] - gc_row[hd:hd + 1, rs], -jnp.inf)))
        kbetas.append(ks[-1] * bcols[-1])
    kk = [_dot_nt(kbetas[i], ks[i]) for i in range(len(inst))]
    qk = [_dot_nt(qs[i], ks[i]) for i in range(len(inst))]
    a_mats = [jnp.where(strict, kk[i] * decays[i], 0.0) for i in range(len(inst))]
    intras = [jnp.where(tri, qk[i] * decays[i], 0.0) for i in range(len(inst))]
    xs = [eye - a for a in a_mats]
    ps = a_mats
    for _ in range(5):
        ps = [_dot_hi(p, p) for p in ps]
        xs = [x + _dot_hi(x, p) for x, p in zip(xs, ps)]
    egs = [jnp.exp(g) for g in gcols]
    sols = [_dot(xs[i], jnp.concatenate([vs[i] * bcols[i], kbetas[i] * egs[i]], axis=1)) for i in range(len(inst))]
    qes = [qs[i] * egs[i] for i in range(len(inst))]
    glasts = [g[c - 1:c] for g in gcols]
    kdecs = [ks[i] * jnp.exp(glasts[i] - gcols[i]) for i in range(len(inst))]

    states = [state_ref[hd] for hd in range(nh)]
    for ch in range(tb // c):
        ii = [ch * nh + hd for hd in range(nh)]
        both = [_dot(jnp.concatenate([sols[i][:, d:], qes[i]], axis=0), states[hd]) for hd, i in enumerate(ii)]
        v_new = [sols[i][:, :d] - both[hd][:c] for hd, i in enumerate(ii)]
        outs = [both[hd][c:] + _dot(intras[i], v_new[hd]) for hd, i in enumerate(ii)]
        states = [states[hd] * jnp.exp(glasts[i]) + lax.dot_general(
            kdecs[i].astype(BF16), v_new[hd].astype(BF16), TN_DIMS, preferred_element_type=F32)
            for hd, i in enumerate(ii)]
        for hd in range(nh):
            ls = slice(hd * d, (hd + 1) * d)
            o = outs[hd]
            on = o * lax.rsqrt(jnp.mean(o * o, axis=-1, keepdims=True) + EPS) * og
            z = dz_ref[rows(ch), ls]
            y_ref[rows(ch), ls] = (on * (z * _sigmoid(z))).astype(y_ref.dtype)
    for hd in range(nh):
        state_ref[hd] = states[hd]


def _deltanet(dqkv, dz, dab, dabt, conv_w, a_log, dt_bias, out_gain, b, t, tb):
    n = b * t
    nb = t // tb
    row = lambda bi, j: (bi * nb + j, 0)
    col = lambda bi, j: (0, bi * nb + j)
    fixed = lambda bi, j: (0, 0)
    nh = DN_HEADS
    return pl.pallas_call(
        functools.partial(_deltanet_body, tb=tb),
        grid=(b, nb),
        in_specs=[
            pl.BlockSpec((tb, 3 * DN_W), row),
            pl.BlockSpec((tb, DN_W), row),
            pl.BlockSpec((tb, 2 * nh), row),
            pl.BlockSpec((2 * nh, tb), col),
            pl.BlockSpec((DN_CONV, 3 * DN_W), fixed),
            pl.BlockSpec((1, nh), fixed),
            pl.BlockSpec((1, nh), fixed),
            pl.BlockSpec((nh, 1), fixed),
            pl.BlockSpec((nh, 1), fixed),
            pl.BlockSpec((1, DN_HEAD_DIM), fixed),
        ],
        out_specs=pl.BlockSpec((tb, DN_W), row),
        out_shape=jax.ShapeDtypeStruct((n, DN_W), BF16),
        scratch_shapes=[
            pltpu.VMEM((SUBLANES, 3 * DN_W), F32),
            pltpu.VMEM((nh, DN_HEAD_DIM, DN_HEAD_DIM), F32),
        ],
        compiler_params=pltpu.CompilerParams(dimension_semantics=("parallel", "arbitrary"),
                                             vmem_limit_bytes=VMEM_LIMIT),
    )(dqkv, dz, dab, dabt, conv_w, a_log.reshape(1, nh), dt_bias.reshape(1, nh),
      a_log.reshape(nh, 1), dt_bias.reshape(nh, 1), out_gain.reshape(1, -1))


def _merge_body(x_ref, ya_ref, yd_ref, gab_ref, wa_ref, wb_ref, wo_ref, o_ref, o3_ref):
    dm = x_ref.shape[1]
    gab = gab_ref[...]
    a = jnp.dot(ya_ref[...], wa_ref[...], preferred_element_type=F32)
    bb = jnp.dot(yd_ref[...], wb_ref[...], preferred_element_type=F32)
    merged = _sigmoid(gab[:, :dm]) * a + _sigmoid(gab[:, dm:]) * bb
    res = x_ref[...] + jnp.dot(merged.astype(BF16), wo_ref[...], preferred_element_type=F32)
    o_ref[...] = res
    for c in range(dm // LANES):
        o3_ref[:, c, :] = res[:, c * LANES:(c + 1) * LANES]


def _merge(xf, y_att, y_dn, gab, w_a, w_b, w_o, tm):
    n, d = xf.shape
    row = lambda i: (i, 0)
    fixed = lambda i: (0, 0)
    return pl.pallas_call(
        _merge_body,
        grid=(n // tm,),
        in_specs=[
            pl.BlockSpec((tm, d), row),
            pl.BlockSpec((tm, ATT_Q_W), row),
            pl.BlockSpec((tm, DN_W), row),
            pl.BlockSpec((tm, 2 * d), row),
            pl.BlockSpec((ATT_Q_W, d), fixed),
            pl.BlockSpec((DN_W, d), fixed),
            pl.BlockSpec((d, d), fixed),
        ],
        out_specs=(pl.BlockSpec((tm, d), row), pl.BlockSpec((tm, d // LANES, LANES), lambda i: (i, 0, 0))),
        out_shape=(jax.ShapeDtypeStruct((n, d), F32), jax.ShapeDtypeStruct((n, d // LANES, LANES), F32)),
        compiler_params=pltpu.CompilerParams(dimension_semantics=("parallel",), vmem_limit_bytes=VMEM_LIMIT),
    )(xf, y_att, y_dn, gab, w_a.astype(BF16), w_b.astype(BF16), w_o.astype(BF16))


def _top16_rows(s, payload=None):
    rows = lax.broadcasted_iota(I32, s.shape, 0)
    big = s.shape[0]
    vals, pays = [], []
    for _ in range(PEER_TOPK):
        m = jnp.max(s, axis=0, keepdims=True)
        am = jnp.min(jnp.where(s == m, rows, big), axis=0, keepdims=True)
        hit = rows == am
        vals.append(m)
        if payload is None:
            pays.append(am)
        else:
            pays.append(jnp.sum(jnp.where(hit, payload, 0), axis=0, keepdims=True))
        s = jnp.where(hit, -jnp.inf, s)
    return jnp.concatenate(vals, axis=0), jnp.concatenate(pays, axis=0)


def _pair_candidates(s0, i0, s1, i1):
    k = PEER_TOPK
    rows8 = lax.broadcasted_iota(I32, (SUBLANES, s0.shape[1]), 0)
    cs = [s0[0:1] + s1]
    ce = [i0[0:1] * PEER_N_KEYS + i1]
    for i in range(1, SUBLANES):
        valid = rows8 < k // (i + 1)
        cs.append(jnp.where(valid, s0[i:i + 1] + s1[0:SUBLANES], -jnp.inf))
        ce.append(i0[i:i + 1] * PEER_N_KEYS + i1[0:SUBLANES])
    cs.append(s0[SUBLANES:k] + s1[0:1])
    ce.append(i0[SUBLANES:k] * PEER_N_KEYS + i1[0:1])
    return jnp.concatenate(cs, axis=0), jnp.concatenate(ce, axis=0)


def _peer_route_body(x_ref, g2_ref, wq_ref, sk_ref, ids_ref, gates_ref, *, tm):
    x = x_ref[...]
    ms = jnp.mean(x * x, axis=-1, keepdims=True)
    h = (x * lax.rsqrt(ms + EPS) * g2_ref[...]).astype(BF16)
    q = jnp.dot(h, wq_ref[...], preferred_element_type=F32).astype(BF16)
    half = PEER_KEY_DIM // 2
    for hd in range(PEER_HEADS):
        tops = []
        for p in range(2):
            c0 = hd * PEER_KEY_DIM + p * half
            st = lax.dot_general(sk_ref[2 * hd + p], q[:, c0:c0 + half], NT_DIMS,
                                 preferred_element_type=F32)
            tops.append(_top16_rows(st))
        (s0, i0), (s1, i1) = tops
        cand_s, cand_e = _pair_candidates(s0, i0, s1, i1)
        best, expert = _top16_rows(cand_s, cand_e)
        e = jnp.exp(best - best[0:1])
        gate = e / jnp.sum(e, axis=0, keepdims=True)
        rs = slice(hd * PEER_TOPK, (hd + 1) * PEER_TOPK)
        ids_ref[rs, :] = expert
        gates_ref[rs, :] = gate


def _peer_route(x1, g2, w_query, sub_keys, tm):
    n, d = x1.shape
    nsel = PEER_HEADS * PEER_TOPK
    half = PEER_KEY_DIM // 2
    sk = sub_keys.reshape(PEER_HEADS * 2, PEER_N_KEYS, half).astype(BF16)
    return pl.pallas_call(
        functools.partial(_peer_route_body, tm=tm),
        grid=(n // tm,),
        in_specs=[
            pl.BlockSpec((tm, d), lambda i: (i, 0)),
            pl.BlockSpec((1, d), lambda i: (0, 0)),
            pl.BlockSpec((d, PEER_HEADS * PEER_KEY_DIM), lambda i: (0, 0)),
            pl.BlockSpec((PEER_HEADS * 2, PEER_N_KEYS, half), lambda i: (0, 0, 0)),
        ],
        out_specs=(pl.BlockSpec((nsel, tm), lambda i: (0, i)),
                   pl.BlockSpec((nsel, tm), lambda i: (0, i))),
        out_shape=(jax.ShapeDtypeStruct((nsel, n), I32), jax.ShapeDtypeStruct((nsel, n), F32)),
        compiler_params=pltpu.CompilerParams(dimension_semantics=("parallel",), vmem_limit_bytes=VMEM_LIMIT),
    )(x1, g2.reshape(1, d), w_query.astype(BF16), sk)


PEER_SLOTS = 16
PEER_WAIT_GROUP = 4


def _peer_apply_body(ids_ref, x_ref, g2_ref, gates_ref, uv_hbm, o_ref, sems, *slot_refs, tb):
    bufs, coefs = slot_refs[:PEER_SLOTS], slot_refs[PEER_SLOTS:]
    nsel = PEER_HEADS * PEER_TOPK
    nchunk = x_ref.shape[1]
    dm = nchunk * LANES

    ngrp = nsel // SUBLANES

    def issue(tok, slot, k0, k1):
        for kk in range(k0, k1):
            pltpu.make_async_copy(uv_hbm.at[ids_ref[tok, kk]], bufs[slot].at[kk], sems.at[slot]).start(
                priority=kk % 2)

    def wait_all(slot):
        pltpu.make_async_copy(uv_hbm.at[pl.ds(0, nsel)], bufs[slot], sems.at[slot]).wait()

    eye = lax.broadcasted_iota(I32, (nsel, nsel), 0) == lax.broadcasted_iota(I32, (nsel, nsel), 1)
    sub = lax.broadcasted_iota(I32, (SUBLANES, LANES), 0)
    masks = {k: (sub & k) == 0 for k in (4, 2, 1)}
    g2 = g2_ref[...]

    def merge(x, y, k):
        if k == 4:
            return jnp.where(masks[k], x, y) + pltpu.roll(jnp.where(masks[k], y, x), k, axis=0)
        return jnp.where(masks[k], x + pltpu.roll(x, SUBLANES - k, axis=0), y + pltpu.roll(y, k, axis=0))

    order = (0, 4, 2, 6, 1, 5, 3, 7)

    def u_of(word):
        return lax.bitcast_convert_type(word << 16, F32)

    def v_of(word):
        return lax.bitcast_convert_type(word & jnp.uint32(0xFFFF0000), F32)

    def normed(tok):
        xt = x_ref[tok]
        ssq = jnp.sum(jnp.sum(xt * xt, axis=1, keepdims=True), axis=0, keepdims=True)
        return xt * lax.rsqrt(ssq * (1.0 / dm) + EPS) * g2

    def step(t, slot, nxt, nxt_slot, h8, t_after):
        def prefetch(k0, k1):
            if nxt is not None:
                issue(nxt, nxt_slot, k0, k1)

        if slot % PEER_WAIT_GROUP == 0:
            for j in range(PEER_WAIT_GROUP):
                wait_all(slot + j)
        grow = gates_ref[t]
        gcol = jnp.sum(jnp.where(eye, jnp.broadcast_to(grow, (nsel, nsel)), 0.0), axis=1, keepdims=True)
        per_grp = nsel // (2 * ngrp)
        groups = []
        for grp in range(ngrp):
            prefetch(grp * per_grp, (grp + 1) * per_grp)
            p = [u_of(bufs[slot][grp * SUBLANES + order.index(j)]) * h8 for j in range(SUBLANES)]
            q4 = [merge(p[2 * i], p[2 * i + 1], 4) for i in range(4)]
            q2 = [merge(q4[2 * i], q4[2 * i + 1], 2) for i in range(2)]
            groups.append(merge(q2[0], q2[1], 1))
        colsum = jnp.concatenate(groups, axis=0)
        act = jnp.sum(colsum, axis=1, keepdims=True)
        gelu = 0.5 * act * (1.0 + lax.erf(act * (2.0 ** -0.5)))
        coef_ref = coefs[slot]
        coef_ref[...] = jnp.broadcast_to(gcol * gelu, (nsel, LANES))
        h8_after = normed(t_after)
        acc = jnp.zeros((nchunk, LANES), F32)
        for kk in range(nsel):
            if kk % 2 == 0:
                prefetch(nsel // 2 + kk // 2, nsel // 2 + kk // 2 + 1)
            acc = acc + coef_ref[kk:kk + 1, :] * v_of(bufs[slot][kk])
        o_ref[t] = x_ref[t] + acc
        return h8_after

    ahead = PEER_SLOTS - 1
    for s in range(ahead):
        issue(s, s, 0, nsel)

    def group_body(g, h8):
        t0 = g * PEER_SLOTS
        for s in range(PEER_SLOTS):
            h8 = step(t0 + s, s, t0 + s + ahead, (s + ahead) % PEER_SLOTS, h8, t0 + s + 1)
        return h8

    h8 = lax.fori_loop(0, tb // PEER_SLOTS - 1, group_body, normed(0))
    t0 = tb - PEER_SLOTS
    h8 = step(t0, 0, t0 + ahead, ahead % PEER_SLOTS, h8, t0 + 1)
    for s in range(1, PEER_SLOTS):
        h8 = step(t0 + s, s, None, None, h8, min(t0 + s + 1, tb - 1))


def _peer_apply(x1_tiles, g2, ids, gates, peer_u, peer_v, tb):
    n, nchunk, _ = x1_tiles.shape
    d = nchunk * LANES
    nsel = PEER_HEADS * PEER_TOPK
    ne = peer_u.shape[0]

    def half_words(tab):
        return lax.bitcast_convert_type(tab.astype(BF16), jnp.uint16).astype(jnp.uint32)

    uv = ((half_words(peer_v) << 16) | half_words(peer_u)).reshape(ne, nchunk, LANES)
    out = pl.pallas_call(
        functools.partial(_peer_apply_body, tb=tb),
        grid=(n // tb,),
        in_specs=[
            pl.BlockSpec((tb, nsel), lambda i: (i, 0), memory_space=pltpu.SMEM),
            pl.BlockSpec((tb, nchunk, LANES), lambda i: (i, 0, 0)),
            pl.BlockSpec((nchunk, LANES), lambda i: (0, 0)),
            pl.BlockSpec((tb, 1, nsel), lambda i: (i, 0, 0)),
            pl.BlockSpec(memory_space=pl.ANY),
        ],
        out_specs=pl.BlockSpec((tb, nchunk, LANES), lambda i: (i, 0, 0)),
        out_shape=jax.ShapeDtypeStruct((n, nchunk, LANES), F32),
        scratch_shapes=[
            pltpu.SemaphoreType.DMA((PEER_SLOTS,)),
        ] + [pltpu.VMEM((nsel, nchunk, LANES), jnp.uint32) for _ in range(PEER_SLOTS)]
        + [pltpu.VMEM((nsel, LANES), F32) for _ in range(PEER_SLOTS)],
        compiler_params=pltpu.CompilerParams(dimension_semantics=("arbitrary",), vmem_limit_bytes=VMEM_LIMIT),
    )(ids, x1_tiles, g2.reshape(nchunk, LANES), gates.reshape(n, 1, nsel), uv)
    return out.reshape(n, d)


def _block_sizes(t):
    return dict(
        tm_proj=256,
        tq=min(512, t),
        tk=min(512, t),
        tb_dn=min(256, t),
        tm_merge=512,
        tm_route=LANES,
        tb_peer=512,
    )


def kernel(x, norm1_gain, w_in, q_norm_gain, k_norm_gain, dn_conv_w, dn_a_log, dn_dt_bias, dn_out_norm_gain,
           w_att_branch, w_dn_branch, w_o, norm2_gain, peer_w_query, peer_sub_keys, peer_u, peer_v):
    b, t, d = x.shape
    n = b * t
    bs = _block_sizes(t)
    xf = x.reshape(n, d)
    for layer in range(w_in.shape[0]):
        (qt, iqt, vt, iwt, dabt, k, ik, dab, dqkv, dz, gab) = _in_proj(
            xf, norm1_gain[layer], w_in[layer], q_norm_gain[layer], k_norm_gain[layer], bs["tm_proj"])
        y_att = _dsa(qt, iqt, iwt, k, vt, ik, b, t, bs["tq"], bs["tk"])
        y_dn = _deltanet(dqkv, dz, dab, dabt, dn_conv_w[layer], dn_a_log[layer], dn_dt_bias[layer],
                         dn_out_norm_gain[layer], b, t, bs["tb_dn"])
        x1, x1_tiles = _merge(xf, y_att, y_dn, gab, w_att_branch[layer], w_dn_branch[layer], w_o[layer],
                              bs["tm_merge"])
        ids_t, gates_t = _peer_route(x1, norm2_gain[layer], peer_w_query[layer], peer_sub_keys[layer],
                                     bs["tm_route"])
        xf = _peer_apply(x1_tiles, norm2_gain[layer], ids_t.T, gates_t.T, peer_u[layer], peer_v[layer], bs["tb_peer"])
    return xf.reshape(b, t, d)
```

```python
import functools

import jax
import jax.numpy as jnp
import numpy as np
from jax import lax
from jax.experimental import pallas as pl
from jax.experimental.pallas import tpu as pltpu

F32 = jnp.float32
BF16 = jnp.bfloat16
I32 = jnp.int32

ATT_HEADS = 8
ATT_KV_HEADS = 2
ATT_HEAD_DIM = 64
IDX_HEADS = 8
IDX_HEAD_DIM = 64
TOPK_MAX = 256
DN_HEADS = 4
DN_HEAD_DIM = 128
DN_CONV = 4
DN_CHUNK = 64
PEER_HEADS = 8
PEER_N_KEYS = 128
PEER_KEY_DIM = 256
PEER_TOPK = 16
EPS = 1e-6

ATT_Q_W = ATT_HEADS * ATT_HEAD_DIM
ATT_KV_W = ATT_KV_HEADS * ATT_HEAD_DIM
IDX_Q_W = IDX_HEADS * IDX_HEAD_DIM
DN_W = DN_HEADS * DN_HEAD_DIM

LANES = 128
SUBLANES = 8
VMEM_LIMIT = 56 * 1024 * 1024

NEG_INF_KEY = int(np.int32(np.uint32(0xFF800000) ^ np.uint32(0x7FFFFFFF)))
INT_MAX = int(np.iinfo(np.int32).max)
KEY_BITS = 32
MASKED_LOGIT = -1e30

NT_DIMS = (((1,), (1,)), ((), ()))
TN_DIMS = (((0,), (0,)), ((), ()))


def _sigmoid(x):
    return 1.0 / (1.0 + jnp.exp(-x))


def _softplus(x):
    return jnp.maximum(x, 0.0) + jnp.log(1.0 + jnp.exp(-jnp.abs(x)))


def _dot(a, b):
    return jnp.dot(a.astype(BF16), b.astype(BF16), preferred_element_type=F32)


def _dot_nt(a, b):
    return lax.dot_general(a.astype(BF16), b.astype(BF16), NT_DIMS, preferred_element_type=F32)


def _split2(x):
    hi = x.astype(BF16)
    lo = (x - hi.astype(F32)).astype(BF16)
    return hi, lo


def _dot_hi(a, b):
    a1, a2 = _split2(a)
    b1, b2 = _split2(b)
    out = jnp.dot(a1, b1, preferred_element_type=F32)
    out = out + jnp.dot(a1, b2, preferred_element_type=F32)
    out = out + jnp.dot(a2, b1, preferred_element_type=F32)
    return out


def _dot_exact_rhs(a, b_exact):
    a1 = a.astype(BF16)
    r1 = a - a1.astype(F32)
    a2 = r1.astype(BF16)
    a3 = (r1 - a2.astype(F32)).astype(BF16)
    out = jnp.dot(a1, b_exact, preferred_element_type=F32)
    out = out + jnp.dot(a2, b_exact, preferred_element_type=F32)
    out = out + jnp.dot(a3, b_exact, preferred_element_type=F32)
    return out


def _dot_exact_lhs(a_exact, b):
    b1 = b.astype(BF16)
    r1 = b - b1.astype(F32)
    b2 = r1.astype(BF16)
    b3 = (r1 - b2.astype(F32)).astype(BF16)
    out = jnp.dot(a_exact, b1, preferred_element_type=F32)
    out = out + jnp.dot(a_exact, b2, preferred_element_type=F32)
    out = out + jnp.dot(a_exact, b3, preferred_element_type=F32)
    return out


C_AK = 0
C_SM = C_AK + ATT_KV_W
C_DQKV = C_SM + LANES
C_DZ = C_DQKV + 3 * DN_W
C_GAB = C_DZ + DN_W
SM_DAB = IDX_HEAD_DIM
R_AQ = 0
R_IQ = R_AQ + ATT_Q_W
R_AV = R_IQ + IDX_Q_W
R_IW = R_AV + ATT_KV_W
R_DAB = R_IW + IDX_HEADS
R_END = R_DAB + 2 * DN_HEADS


def _in_proj_body(x_ref, g1_ref, w_ref, wt_ref, qg_ref, kg_ref,
                  qt_ref, iqt_ref, vt_ref, iwt_ref, dabt_ref, k_ref, ik_ref, dab_ref, dqkv_ref, dz_ref, gab_ref):
    x = x_ref[...]
    ms = jnp.mean(x * x, axis=-1, keepdims=True)
    h = (x * lax.rsqrt(ms + EPS) * g1_ref[...]).astype(BF16)

    rt = lax.dot_general(wt_ref[...], h, NT_DIMS, preferred_element_type=F32)
    qg = qg_ref[...] * (ATT_HEAD_DIM ** -0.5)
    for hd in range(ATT_HEADS):
        sl = slice(R_AQ + hd * ATT_HEAD_DIM, R_AQ + (hd + 1) * ATT_HEAD_DIM)
        blk = rt[sl]
        qt_ref[sl, :] = (blk * lax.rsqrt(jnp.mean(blk * blk, axis=0, keepdims=True) + EPS) * qg).astype(qt_ref.dtype)
    iqt_ref[...] = rt[R_IQ:R_AV].astype(iqt_ref.dtype)
    vt_ref[...] = rt[R_AV:R_IW].astype(vt_ref.dtype)
    iwt_ref[...] = rt[R_IW:R_DAB]
    dabt_ref[...] = rt[R_DAB:R_END]

    def proj(c0, c1):
        return jnp.dot(h, w_ref[:, c0:c1], preferred_element_type=F32)

    ak = proj(C_AK, C_SM)
    kg = kg_ref[...]
    for hd in range(ATT_KV_HEADS):
        sl = slice(hd * ATT_HEAD_DIM, (hd + 1) * ATT_HEAD_DIM)
        blk = ak[:, sl]
        k_ref[:, sl] = (blk * lax.rsqrt(jnp.mean(blk * blk, axis=-1, keepdims=True) + EPS) * kg).astype(k_ref.dtype)
    sm = proj(C_SM, C_DQKV)
    ik_ref[...] = sm[:, :IDX_HEAD_DIM].astype(ik_ref.dtype)
    dab_ref[...] = sm[:, SM_DAB:SM_DAB + 2 * DN_HEADS]
    dqkv_ref[...] = proj(C_DQKV, C_DZ)
    dz_ref[...] = proj(C_DZ, C_GAB)
    gab_ref[...] = proj(C_GAB, C_GAB + 2 * x.shape[1])


def _in_proj(xf, g1, w_in, q_gain, k_gain, tm):
    n, d = xf.shape
    cuts = np.cumsum([ATT_Q_W, ATT_KV_W, ATT_KV_W, IDX_Q_W, IDX_HEAD_DIM, IDX_HEADS,
                      DN_W, DN_W, DN_W, DN_W, DN_HEADS, DN_HEADS, d, d])[:-1].tolist()
    aq, ak, av, iq, ik, iw, dq, dk, dv, dz, da, db, ga, gb = jnp.split(w_in, cuts, axis=-1)
    pad = jnp.zeros((d, LANES - IDX_HEAD_DIM - 2 * DN_HEADS), w_in.dtype)
    w_all = jnp.concatenate([ak, ik, da, db, pad, dq, dk, dv, dz, ga, gb], axis=-1).astype(BF16)
    wt_all = jnp.concatenate([aq, iq, av, iw, da, db], axis=-1).T.astype(BF16)
    wtot = w_all.shape[1]
    row = lambda i: (i, 0)
    col = lambda i: (0, i)
    fixed = lambda i: (0, 0)
    out_shapes = (
        jax.ShapeDtypeStruct((ATT_Q_W, n), BF16),
        jax.ShapeDtypeStruct((IDX_Q_W, n), BF16),
        jax.ShapeDtypeStruct((ATT_KV_W, n), BF16),
        jax.ShapeDtypeStruct((IDX_HEADS, n), F32),
        jax.ShapeDtypeStruct((2 * DN_HEADS, n), F32),
        jax.ShapeDtypeStruct((n, ATT_KV_W), BF16),
        jax.ShapeDtypeStruct((n, IDX_HEAD_DIM), BF16),
        jax.ShapeDtypeStruct((n, 2 * DN_HEADS), F32),
        jax.ShapeDtypeStruct((n, 3 * DN_W), F32),
        jax.ShapeDtypeStruct((n, DN_W), F32),
        jax.ShapeDtypeStruct((n, 2 * d), F32),
    )
    out_specs = tuple(
        pl.BlockSpec((s.shape[0], tm), col) if s.shape[0] != n else pl.BlockSpec((tm, s.shape[1]), row)
        for s in out_shapes)
    return pl.pallas_call(
        _in_proj_body,
        grid=(n // tm,),
        in_specs=[
            pl.BlockSpec((tm, d), row),
            pl.BlockSpec((1, d), fixed),
            pl.BlockSpec((d, wtot), fixed),
            pl.BlockSpec((R_END, d), fixed),
            pl.BlockSpec((ATT_HEAD_DIM, 1), fixed),
            pl.BlockSpec((1, ATT_HEAD_DIM), fixed),
        ],
        out_specs=out_specs,
        out_shape=out_shapes,
        compiler_params=pltpu.CompilerParams(dimension_semantics=("parallel",), vmem_limit_bytes=VMEM_LIMIT),
    )(xf, g1.reshape(1, d), w_all, wt_all, q_gain.reshape(-1, 1), k_gain.reshape(1, -1))


FOLD_ROWS = 4 * SUBLANES
DSA_AHEAD = 2


def _dsa_body(qt_ref, iqt_ref, iwt_ref, k_ref, vt_ref, ik_ref, o_ref,
              key_ref, bias_ref, *acc_refs, tq, tk, ksel):
    i = pl.program_id(1)
    q0 = i * tq
    n_kb = (q0 + tq + tk - 1) // tk
    qpos = q0 + lax.broadcasted_iota(I32, (1, tq), 1)
    iwt = iwt_ref[...]

    def score_body(kb, carry):
        k0 = pl.multiple_of(kb * tk, tk)
        ikb = ik_ref[pl.ds(k0, tk), :]
        acc = jnp.zeros((tk, tq), F32)
        for hd in range(IDX_HEADS):
            sl = slice(hd * IDX_HEAD_DIM, (hd + 1) * IDX_HEAD_DIM)
            dots = jnp.dot(ikb, iqt_ref[sl, :], preferred_element_type=F32)
            acc = acc + iwt[hd:hd + 1, :] * jnp.maximum(dots, 0.0)
        bits = lax.bitcast_convert_type(acc, I32)
        keys = jnp.where(bits >= 0, bits, bits ^ INT_MAX)
        kpos = k0 + lax.broadcasted_iota(I32, (tk, 1), 0)
        key_ref[kb] = jnp.where(kpos <= qpos, keys, NEG_INF_KEY)
        return carry

    lax.fori_loop(0, n_kb, score_body, 0)

    def fold(x, op):
        return op(x.reshape(tk // FOLD_ROWS, FOLD_ROWS, tq), axis=0)

    def count_ge(thr):
        def body(kb, c):
            return c + fold(jnp.where(key_ref[kb] >= thr, 1.0, 0.0), jnp.sum)
        c = lax.fori_loop(0, n_kb, body, jnp.zeros((FOLD_ROWS, tq), F32))
        return jnp.sum(c, axis=0, keepdims=True)

    def bisect_cond(st):
        it, _, _, _, _, pending = st
        return jnp.logical_and(it < KEY_BITS, jnp.max(pending) > 0.0)

    def probe(st):
        it, lo, hi, c_lo, c_hi, pending = st
        mid = (lo >> 1) + (hi >> 1) + (lo & hi & 1)
        c = count_ge(mid)
        live = pending > 0.0
        up = jnp.logical_and(live, c >= ksel)
        down = jnp.logical_and(live, c < ksel)
        c_lo = jnp.where(up, c, c_lo)
        return (it + 1, jnp.where(up, mid, lo), jnp.where(down, mid, hi), c_lo, jnp.where(down, c, c_hi),
                jnp.where(c_lo == ksel, 0.0, pending))

    def bisect_body(st):
        return probe(probe(st))

    lo0 = jnp.full((1, tq), NEG_INF_KEY + 1, I32)
    hi0 = jnp.full((1, tq), INT_MAX, I32)
    zero = jnp.zeros((1, tq), F32)
    pending0 = jnp.where(qpos + 1 > ksel, 1.0, 0.0)
    _, thr, _, c_thr, c_above, _ = lax.while_loop(bisect_cond, bisect_body, (0, lo0, hi0, zero, zero, pending0))

    need = ksel - c_above
    has_excess = jnp.max(jnp.where(c_thr > ksel, 1.0, 0.0)) > 0.0

    @pl.when(has_excess)
    def _():
        lower = jnp.where(lax.broadcasted_iota(I32, (tk, tk), 0) >= lax.broadcasted_iota(I32, (tk, tk), 1),
                          1.0, 0.0).astype(BF16)

        def body(kb, seen):
            keys = key_ref[kb]
            tie = keys == thr
            tie_f = jnp.where(tie, 1.0, 0.0)
            prefix = jnp.dot(lower, tie_f.astype(BF16), preferred_element_type=F32) + seen
            drop = jnp.logical_and(tie, prefix > need)
            key_ref[kb] = jnp.where(drop, NEG_INF_KEY, keys)
            return seen + jnp.sum(tie_f, axis=0, keepdims=True)

        lax.fori_loop(0, n_kb, body, jnp.zeros((1, tq), F32))

    for acc_ref in acc_refs:
        acc_ref[...] = jnp.zeros(acc_ref.shape, F32)
    grp = ATT_HEADS // ATT_KV_HEADS

    def att_body(kb, carry):
        m_all, l_all = carry
        k0 = pl.multiple_of(kb * tk, tk)
        kblk = k_ref[pl.ds(k0, tk), :]
        vtb = vt_ref[kb]
        bias_ref[...] = jnp.where(key_ref[kb] >= thr, 0.0, MASKED_LOGIT)
        m_rows, l_rows = [], []

        def logits(hd):
            g = hd // grp
            return jnp.dot(kblk[:, g * ATT_HEAD_DIM:(g + 1) * ATT_HEAD_DIM],
                           qt_ref[hd * ATT_HEAD_DIM:(hd + 1) * ATT_HEAD_DIM, :],
                           preferred_element_type=F32) + bias_ref[...]

        def accumulate(hd, alpha, p):
            g = hd // grp
            acc_ref = acc_refs[hd]
            acc_ref[...] = alpha * acc_ref[...] + jnp.dot(vtb[g * ATT_HEAD_DIM:(g + 1) * ATT_HEAD_DIM, :], p,
                                                          preferred_element_type=F32)

        queue = [logits(hd) for hd in range(DSA_AHEAD)]
        pending = None
        for hd in range(ATT_HEADS):
            s = queue.pop(0)
            if hd + DSA_AHEAD < ATT_HEADS:
                queue.append(logits(hd + DSA_AHEAD))
            m_old = m_all[hd:hd + 1, :]
            m_new = jnp.maximum(m_old, jnp.max(fold(s, jnp.max), axis=0, keepdims=True))
            p = jnp.exp(s - m_new)
            alpha = jnp.exp(m_old - m_new)
            l_rows.append(alpha * l_all[hd:hd + 1, :] + jnp.sum(fold(p, jnp.sum), axis=0, keepdims=True))
            m_rows.append(m_new)
            if pending is not None:
                accumulate(*pending)
            pending = (hd, alpha, p.astype(BF16))
        accumulate(*pending)
        return jnp.concatenate(m_rows, axis=0), jnp.concatenate(l_rows, axis=0)

    m0 = jnp.full((ATT_HEADS, tq), MASKED_LOGIT, F32)
    _, l_fin = lax.fori_loop(0, n_kb, att_body, (m0, jnp.zeros((ATT_HEADS, tq), F32)))
    for pair in range(ATT_HEADS // 2):
        rows = [acc_refs[hd][...] / l_fin[hd:hd + 1, :] for hd in (2 * pair, 2 * pair + 1)]
        o_ref[:, pair * LANES:(pair + 1) * LANES] = jnp.concatenate(rows, axis=0).T.astype(o_ref.dtype)


def _dsa(qt, iqt, iwt, k, vt, ik, b, t, tq, tk):
    n = b * t
    nq = t // tq
    nkb = t // tk
    ksel = min(TOPK_MAX, t // 4)
    qcol = lambda bi, i: (0, bi * nq + i)
    brow = lambda bi, i: (bi, 0)
    vtb = vt.reshape(ATT_KV_W, n // tk, tk).transpose(1, 0, 2)
    return pl.pallas_call(
        functools.partial(_dsa_body, tq=tq, tk=tk, ksel=ksel),
        grid=(b, nq),
        in_specs=[
            pl.BlockSpec((ATT_Q_W, tq), qcol),
            pl.BlockSpec((IDX_Q_W, tq), qcol),
            pl.BlockSpec((IDX_HEADS, tq), qcol),
            pl.BlockSpec((t, ATT_KV_W), brow),
            pl.BlockSpec((nkb, ATT_KV_W, tk), lambda bi, i: (bi, 0, 0)),
            pl.BlockSpec((t, IDX_HEAD_DIM), brow),
        ],
        out_specs=pl.BlockSpec((tq, ATT_Q_W), lambda bi, i: (bi * nq + i, 0)),
        out_shape=jax.ShapeDtypeStruct((n, ATT_Q_W), BF16),
        scratch_shapes=[
            pltpu.VMEM((nkb, tk, tq), I32),
            pltpu.VMEM((tk, tq), F32),
        ] + [pltpu.VMEM((ATT_HEAD_DIM, tq), F32) for _ in range(ATT_HEADS)],
        compiler_params=pltpu.CompilerParams(dimension_semantics=("parallel", "arbitrary"),
                                             vmem_limit_bytes=VMEM_LIMIT),
    )(qt, iqt, iwt, k, vtb, ik)


def _deltanet_body(x_ref, dz_ref, dab_ref, dabt_ref, cw_ref, alog_r_ref, bias_r_ref, alog_c_ref, bias_c_ref,
                   og_ref, y_ref, carry_ref, state_ref, *, tb):
    c = DN_CHUNK
    d = DN_HEAD_DIM
    nh = DN_HEADS

    @pl.when(pl.program_id(1) == 0)
    def _():
        carry_ref[...] = jnp.zeros(carry_ref.shape, F32)
        state_ref[...] = jnp.zeros(state_ref.shape, F32)

    xb = x_ref[...]
    xx = jnp.concatenate([carry_ref[...], xb], axis=0)
    cw = cw_ref[...]
    off = SUBLANES - (DN_CONV - 1)
    conv = cw[0:1] * xx[off:off + tb]
    for j in range(1, DN_CONV):
        conv = conv + cw[j:j + 1] * xx[off + j:off + j + tb]
    carry_ref[...] = xb[tb - SUBLANES:tb]
    qkv = conv * _sigmoid(conv)

    dab = dab_ref[...]
    g_col = -jnp.exp(alog_r_ref[...]) * _softplus(dab[:, 0:nh] + bias_r_ref[...])
    beta_col = _sigmoid(dab[:, nh:2 * nh])
    g_row = -jnp.exp(alog_c_ref[...]) * _softplus(dabt_ref[0:nh, :] + bias_c_ref[...])

    ri = lax.broadcasted_iota(I32, (tb, tb), 0)
    ci = lax.broadcasted_iota(I32, (tb, tb), 1)
    same_chunk = (ri // c) == (ci // c)
    lower_blk = jnp.where(jnp.logical_and(same_chunk, ri >= ci), 1.0, 0.0).astype(BF16)
    upper_blk = jnp.where(jnp.logical_and(same_chunk, ri <= ci), 1.0, 0.0).astype(BF16)
    gc_col = _dot_exact_lhs(lower_blk, g_col)
    gc_row = _dot_exact_rhs(g_row, upper_blk)

    r64 = lax.broadcasted_iota(I32, (c, c), 0)
    c64 = lax.broadcasted_iota(I32, (c, c), 1)
    tri = r64 >= c64
    strict = r64 > c64
    eye = jnp.where(r64 == c64, 1.0, 0.0)
    og = og_ref[...]

    inst = [(ch, hd) for ch in range(tb // c) for hd in range(nh)]
    rows = lambda ch: slice(ch * c, (ch + 1) * c)
    qs, ks, vs, bcols, gcols, decays, kbetas = [], [], [], [], [], [], []
    for ch, hd in inst:
        rs = rows(ch)
        qc = qkv[rs, hd * d:(hd + 1) * d]
        kc = qkv[rs, nh * d + hd * d:nh * d + (hd + 1) * d]
        qs.append(qc * lax.rsqrt(jnp.sum(qc * qc, axis=-1, keepdims=True) + EPS) * (d ** -0.5))
        ks.append(kc * lax.rsqrt(jnp.sum(kc * kc, axis=-1, keepdims=True) + EPS))
        vs.append(qkv[rs, 2 * nh * d + hd * d:2 * nh * d + (hd + 1) * d])
        bcols.append(beta_col[rs, hd:hd + 1])
        gcols.append(gc_col[rs, hd:hd + 1])
        decays.append(jnp.exp(jnp.where(tri, gcols[-1] - gc_row[hd:hd + 1, rs], -jnp.inf)))
        kbetas.append(ks[-1] * bcols[-1])
    kk = [_dot_nt(kbetas[i], ks[i]) for i in range(len(inst))]
    qk = [_dot_nt(qs[i], ks[i]) for i in range(len(inst))]
    a_mats = [jnp.where(strict, kk[i] * decays[i], 0.0) for i in range(len(inst))]
    intras = [jnp.where(tri, qk[i] * decays[i], 0.0) for i in range(len(inst))]
    xs = [eye - a for a in a_mats]
    ps = a_mats
    for _ in range(5):
        ps = [_dot_hi(p, p) for p in ps]
        xs = [x + _dot_hi(x, p) for x, p in zip(xs, ps)]
    egs = [jnp.exp(g) for g in gcols]
    sols = [_dot(xs[i], jnp.concatenate([vs[i] * bcols[i], kbetas[i] * egs[i]], axis=1)) for i in range(len(inst))]
    qes = [qs[i] * egs[i] for i in range(len(inst))]
    glasts = [g[c - 1:c] for g in gcols]
    kdecs = [ks[i] * jnp.exp(glasts[i] - gcols[i]) for i in range(len(inst))]

    states = [state_ref[hd] for hd in range(nh)]
    for ch in range(tb // c):
        ii = [ch * nh + hd for hd in range(nh)]
        both = [_dot(jnp.concatenate([sols[i][:, d:], qes[i]], axis=0), states[hd]) for hd, i in enumerate(ii)]
        v_new = [sols[i][:, :d] - both[hd][:c] for hd, i in enumerate(ii)]
        outs = [both[hd][c:] + _dot(intras[i], v_new[hd]) for hd, i in enumerate(ii)]
        states = [states[hd] * jnp.exp(glasts[i]) + lax.dot_general(
            kdecs[i].astype(BF16), v_new[hd].astype(BF16), TN_DIMS, preferred_element_type=F32)
            for hd, i in enumerate(ii)]
        for hd in range(nh):
            ls = slice(hd * d, (hd + 1) * d)
            o = outs[hd]
            on = o * lax.rsqrt(jnp.mean(o * o, axis=-1, keepdims=True) + EPS) * og
            z = dz_ref[rows(ch), ls]
            y_ref[rows(ch), ls] = (on * (z * _sigmoid(z))).astype(y_ref.dtype)
    for hd in range(nh):
        state_ref[hd] = states[hd]


def _deltanet(dqkv, dz, dab, dabt, conv_w, a_log, dt_bias, out_gain, b, t, tb):
    n = b * t
    nb = t // tb
    row = lambda bi, j: (bi * nb + j, 0)
    col = lambda bi, j: (0, bi * nb + j)
    fixed = lambda bi, j: (0, 0)
    nh = DN_HEADS
    return pl.pallas_call(
        functools.partial(_deltanet_body, tb=tb),
        grid=(b, nb),
        in_specs=[
            pl.BlockSpec((tb, 3 * DN_W), row),
            pl.BlockSpec((tb, DN_W), row),
            pl.BlockSpec((tb, 2 * nh), row),
            pl.BlockSpec((2 * nh, tb), col),
            pl.BlockSpec((DN_CONV, 3 * DN_W), fixed),
            pl.BlockSpec((1, nh), fixed),
            pl.BlockSpec((1, nh), fixed),
            pl.BlockSpec((nh, 1), fixed),
            pl.BlockSpec((nh, 1), fixed),
            pl.BlockSpec((1, DN_HEAD_DIM), fixed),
        ],
        out_specs=pl.BlockSpec((tb, DN_W), row),
        out_shape=jax.ShapeDtypeStruct((n, DN_W), BF16),
        scratch_shapes=[
            pltpu.VMEM((SUBLANES, 3 * DN_W), F32),
            pltpu.VMEM((nh, DN_HEAD_DIM, DN_HEAD_DIM), F32),
        ],
        compiler_params=pltpu.CompilerParams(dimension_semantics=("parallel", "arbitrary"),
                                             vmem_limit_bytes=VMEM_LIMIT),
    )(dqkv, dz, dab, dabt, conv_w, a_log.reshape(1, nh), dt_bias.reshape(1, nh),
      a_log.reshape(nh, 1), dt_bias.reshape(nh, 1), out_gain.reshape(1, -1))


def _merge_body(x_ref, ya_ref, yd_ref, gab_ref, wa_ref, wb_ref, wo_ref, o_ref, o3_ref):
    dm = x_ref.shape[1]
    gab = gab_ref[...]
    a = jnp.dot(ya_ref[...], wa_ref[...], preferred_element_type=F32)
    bb = jnp.dot(yd_ref[...], wb_ref[...], preferred_element_type=F32)
    merged = _sigmoid(gab[:, :dm]) * a + _sigmoid(gab[:, dm:]) * bb
    res = x_ref[...] + jnp.dot(merged.astype(BF16), wo_ref[...], preferred_element_type=F32)
    o_ref[...] = res
    for c in range(dm // LANES):
        o3_ref[:, c, :] = res[:, c * LANES:(c + 1) * LANES]


def _merge(xf, y_att, y_dn, gab, w_a, w_b, w_o, tm):
    n, d = xf.shape
    row = lambda i: (i, 0)
    fixed = lambda i: (0, 0)
    return pl.pallas_call(
        _merge_body,
        grid=(n // tm,),
        in_specs=[
            pl.BlockSpec((tm, d), row),
            pl.BlockSpec((tm, ATT_Q_W), row),
            pl.BlockSpec((tm, DN_W), row),
            pl.BlockSpec((tm, 2 * d), row),
            pl.BlockSpec((ATT_Q_W, d), fixed),
            pl.BlockSpec((DN_W, d), fixed),
            pl.BlockSpec((d, d), fixed),
        ],
        out_specs=(pl.BlockSpec((tm, d), row), pl.BlockSpec((tm, d // LANES, LANES), lambda i: (i, 0, 0))),
        out_shape=(jax.ShapeDtypeStruct((n, d), F32), jax.ShapeDtypeStruct((n, d // LANES, LANES), F32)),
        compiler_params=pltpu.CompilerParams(dimension_semantics=("parallel",), vmem_limit_bytes=VMEM_LIMIT),
    )(xf, y_att, y_dn, gab, w_a.astype(BF16), w_b.astype(BF16), w_o.astype(BF16))


def _top16_rows(s, payload=None):
    rows = lax.broadcasted_iota(I32, s.shape, 0)
    big = s.shape[0]
    vals, pays = [], []
    for _ in range(PEER_TOPK):
        m = jnp.max(s, axis=0, keepdims=True)
        am = jnp.min(jnp.where(s == m, rows, big), axis=0, keepdims=True)
        hit = rows == am
        vals.append(m)
        if payload is None:
            pays.append(am)
        else:
            pays.append(jnp.sum(jnp.where(hit, payload, 0), axis=0, keepdims=True))
        s = jnp.where(hit, -jnp.inf, s)
    return jnp.concatenate(vals, axis=0), jnp.concatenate(pays, axis=0)


def _pair_candidates(s0, i0, s1, i1):
    k = PEER_TOPK
    rows8 = lax.broadcasted_iota(I32, (SUBLANES, s0.shape[1]), 0)
    cs = [s0[0:1] + s1]
    ce = [i0[0:1] * PEER_N_KEYS + i1]
    for i in range(1, SUBLANES):
        valid = rows8 < k // (i + 1)
        cs.append(jnp.where(valid, s0[i:i + 1] + s1[0:SUBLANES], -jnp.inf))
        ce.append(i0[i:i + 1] * PEER_N_KEYS + i1[0:SUBLANES])
    cs.append(s0[SUBLANES:k] + s1[0:1])
    ce.append(i0[SUBLANES:k] * PEER_N_KEYS + i1[0:1])
    return jnp.concatenate(cs, axis=0), jnp.concatenate(ce, axis=0)


def _peer_route_body(x_ref, g2_ref, wq_ref, sk_ref, ids_ref, gates_ref, *, tm):
    x = x_ref[...]
    ms = jnp.mean(x * x, axis=-1, keepdims=True)
    h = (x * lax.rsqrt(ms + EPS) * g2_ref[...]).astype(BF16)
    q = jnp.dot(h, wq_ref[...], preferred_element_type=F32).astype(BF16)
    half = PEER_KEY_DIM // 2
    for hd in range(PEER_HEADS):
        tops = []
        for p in range(2):
            c0 = hd * PEER_KEY_DIM + p * half
            st = lax.dot_general(sk_ref[2 * hd + p], q[:, c0:c0 + half], NT_DIMS,
                                 preferred_element_type=F32)
            tops.append(_top16_rows(st))
        (s0, i0), (s1, i1) = tops
        cand_s, cand_e = _pair_candidates(s0, i0, s1, i1)
        best, expert = _top16_rows(cand_s, cand_e)
        e = jnp.exp(best - best[0:1])
        gate = e / jnp.sum(e, axis=0, keepdims=True)
        rs = slice(hd * PEER_TOPK, (hd + 1) * PEER_TOPK)
        ids_ref[rs, :] = expert
        gates_ref[rs, :] = gate


def _peer_route(x1, g2, w_query, sub_keys, tm):
    n, d = x1.shape
    nsel = PEER_HEADS * PEER_TOPK
    half = PEER_KEY_DIM // 2
    sk = sub_keys.reshape(PEER_HEADS * 2, PEER_N_KEYS, half).astype(BF16)
    return pl.pallas_call(
        functools.partial(_peer_route_body, tm=tm),
        grid=(n // tm,),
        in_specs=[
            pl.BlockSpec((tm, d), lambda i: (i, 0)),
            pl.BlockSpec((1, d), lambda i: (0, 0)),
            pl.BlockSpec((d, PEER_HEADS * PEER_KEY_DIM), lambda i: (0, 0)),
            pl.BlockSpec((PEER_HEADS * 2, PEER_N_KEYS, half), lambda i: (0, 0, 0)),
        ],
        out_specs=(pl.BlockSpec((nsel, tm), lambda i: (0, i)),
                   pl.BlockSpec((nsel, tm), lambda i: (0, i))),
        out_shape=(jax.ShapeDtypeStruct((nsel, n), I32), jax.ShapeDtypeStruct((nsel, n), F32)),
        compiler_params=pltpu.CompilerParams(dimension_semantics=("parallel",), vmem_limit_bytes=VMEM_LIMIT),
    )(x1, g2.reshape(1, d), w_query.astype(BF16), sk)


PEER_SLOTS = 16
PEER_WAIT_GROUP = 4


def _peer_apply_body(ids_ref, x_ref, g2_ref, gates_ref, uv_hbm, o_ref, sems, tiles_ref, *slot_refs, tb):
    bufs, coefs = slot_refs[:PEER_SLOTS], slot_refs[PEER_SLOTS:]
    nsel = PEER_HEADS * PEER_TOPK
    nchunk = x_ref.shape[1]
    dm = nchunk * LANES

    ngrp = nsel // SUBLANES

    def issue(tok, slot, k0, k1):
        for kk in range(k0, k1):
            pltpu.make_async_copy(uv_hbm.at[ids_ref[tok, kk]], bufs[slot].at[kk], sems.at[slot]).start(
                priority=kk % 2)

    def wait_all(slot):
        pltpu.make_async_copy(uv_hbm.at[pl.ds(0, nsel)], bufs[slot], sems.at[slot]).wait()

    eye = lax.broadcasted_iota(I32, (nsel, nsel), 0) == lax.broadcasted_iota(I32, (nsel, nsel), 1)
    sub = lax.broadcasted_iota(I32, (SUBLANES, LANES), 0)
    masks = {k: (sub & k) == 0 for k in (4, 2, 1)}
    g2 = g2_ref[...]

    def merge(x, y, k):
        if k == 4:
            return jnp.where(masks[k], x, y) + pltpu.roll(jnp.where(masks[k], y, x), k, axis=0)
        return jnp.where(masks[k], x + pltpu.roll(x, SUBLANES - k, axis=0), y + pltpu.roll(y, k, axis=0))

    order = (0, 4, 2, 6, 1, 5, 3, 7)

    def u_of(word):
        return lax.bitcast_convert_type(word << 16, F32)

    def v_of(word):
        return lax.bitcast_convert_type(word & jnp.uint32(0xFFFF0000), F32)

    def normed(tok):
        xt = x_ref[tok]
        ssq = jnp.sum(jnp.sum(xt * xt, axis=1, keepdims=True), axis=0, keepdims=True)
        return xt * lax.rsqrt(ssq * (1.0 / dm) + EPS) * g2

    def step(t, slot, nxt, nxt_slot, h8, t_after):
        def prefetch(k0, k1):
            if nxt is not None:
                issue(nxt, nxt_slot, k0, k1)

        if slot % PEER_WAIT_GROUP == 0:
            for j in range(PEER_WAIT_GROUP):
                wait_all(slot + j)
        grow = gates_ref[t]
        gcol = jnp.sum(jnp.where(eye, jnp.broadcast_to(grow, (nsel, nsel)), 0.0), axis=1, keepdims=True)
        per_grp = nsel // (2 * ngrp)
        groups = []
        for grp in range(ngrp):
            prefetch(grp * per_grp, (grp + 1) * per_grp)
            p = [u_of(bufs[slot][grp * SUBLANES + order.index(j)]) * h8 for j in range(SUBLANES)]
            q4 = [merge(p[2 * i], p[2 * i + 1], 4) for i in range(4)]
            q2 = [merge(q4[2 * i], q4[2 * i + 1], 2) for i in range(2)]
            groups.append(merge(q2[0], q2[1], 1))
        colsum = jnp.concatenate(groups, axis=0)
        act = jnp.sum(colsum, axis=1, keepdims=True)
        gelu = 0.5 * act * (1.0 + lax.erf(act * (2.0 ** -0.5)))
        coef_ref = coefs[slot]
        coef_ref[...] = jnp.broadcast_to(gcol * gelu, (nsel, LANES))
        h8_after = normed(t_after)
        acc = jnp.zeros((nchunk, LANES), F32)
        for kk in range(nsel):
            if kk % 2 == 0:
                prefetch(nsel // 2 + kk // 2, nsel // 2 + kk // 2 + 1)
            acc = acc + coef_ref[kk:kk + 1, :] * v_of(bufs[slot][kk])
        tiles_ref[t] = x_ref[t] + acc
        return h8_after

    ahead = PEER_SLOTS - 1
    for s in range(ahead):
        issue(s, s, 0, nsel)

    def group_body(g, h8):
        t0 = g * PEER_SLOTS
        for s in range(PEER_SLOTS):
            h8 = step(t0 + s, s, t0 + s + ahead, (s + ahead) % PEER_SLOTS, h8, t0 + s + 1)
        return h8

    h8 = lax.fori_loop(0, tb // PEER_SLOTS - 1, group_body, normed(0))
    t0 = tb - PEER_SLOTS
    h8 = step(t0, 0, t0 + ahead, ahead % PEER_SLOTS, h8, t0 + 1)
    for s in range(1, PEER_SLOTS):
        h8 = step(t0 + s, s, None, None, h8, min(t0 + s + 1, tb - 1))
    for c in range(nchunk):
        o_ref[:, c * LANES:(c + 1) * LANES] = tiles_ref[:, c, :]


def _peer_apply(x1_tiles, g2, ids, gates, peer_u, peer_v, tb):
    n, nchunk, _ = x1_tiles.shape
    d = nchunk * LANES
    nsel = PEER_HEADS * PEER_TOPK
    ne = peer_u.shape[0]

    def half_words(tab):
        return lax.bitcast_convert_type(tab.astype(BF16), jnp.uint16).astype(jnp.uint32)

    uv = ((half_words(peer_v) << 16) | half_words(peer_u)).reshape(ne, nchunk, LANES)
    out = pl.pallas_call(
        functools.partial(_peer_apply_body, tb=tb),
        grid=(n // tb,),
        in_specs=[
            pl.BlockSpec((tb, nsel), lambda i: (i, 0), memory_space=pltpu.SMEM),
            pl.BlockSpec((tb, nchunk, LANES), lambda i: (i, 0, 0)),
            pl.BlockSpec((nchunk, LANES), lambda i: (0, 0)),
            pl.BlockSpec((tb, 1, nsel), lambda i: (i, 0, 0)),
            pl.BlockSpec(memory_space=pl.ANY),
        ],
        out_specs=pl.BlockSpec((tb, d), lambda i: (i, 0)),
        out_shape=jax.ShapeDtypeStruct((n, d), F32),
        scratch_shapes=[
            pltpu.SemaphoreType.DMA((PEER_SLOTS,)),
            pltpu.VMEM((tb, nchunk, LANES), F32),
        ] + [pltpu.VMEM((nsel, nchunk, LANES), jnp.uint32) for _ in range(PEER_SLOTS)]
        + [pltpu.VMEM((nsel, LANES), F32) for _ in range(PEER_SLOTS)],
        compiler_params=pltpu.CompilerParams(dimension_semantics=("arbitrary",), vmem_limit_bytes=VMEM_LIMIT),
    )(ids, x1_tiles, g2.reshape(nchunk, LANES), gates.reshape(n, 1, nsel), uv)
    return out


def _block_sizes(t):
    return dict(
        tm_proj=256,
        tq=min(512, t),
        tk=min(512, t),
        tb_dn=min(256, t),
        tm_merge=512,
        tm_route=LANES,
        tb_peer=512,
    )


def kernel(x, norm1_gain, w_in, q_norm_gain, k_norm_gain, dn_conv_w, dn_a_log, dn_dt_bias, dn_out_norm_gain,
           w_att_branch, w_dn_branch, w_o, norm2_gain, peer_w_query, peer_sub_keys, peer_u, peer_v):
    b, t, d = x.shape
    n = b * t
    bs = _block_sizes(t)
    xf = x.reshape(n, d)
    for layer in range(w_in.shape[0]):
        (qt, iqt, vt, iwt, dabt, k, ik, dab, dqkv, dz, gab) = _in_proj(
            xf, norm1_gain[layer], w_in[layer], q_norm_gain[layer], k_norm_gain[layer], bs["tm_proj"])
        y_att = _dsa(qt, iqt, iwt, k, vt, ik, b, t, bs["tq"], bs["tk"])
        y_dn = _deltanet(dqkv, dz, dab, dabt, dn_conv_w[layer], dn_a_log[layer], dn_dt_bias[layer],
                         dn_out_norm_gain[layer], b, t, bs["tb_dn"])
        x1, x1_tiles = _merge(xf, y_att, y_dn, gab, w_att_branch[layer], w_dn_branch[layer], w_o[layer],
                              bs["tm_merge"])
        ids_t, gates_t = _peer_route(x1, norm2_gain[layer], peer_w_query[layer], peer_sub_keys[layer],
                                     bs["tm_route"])
        xf = _peer_apply(x1_tiles, norm2_gain[layer], ids_t.T, gates_t.T, peer_u[layer], peer_v[layer], bs["tb_peer"])
    return xf.reshape(b, t, d)
```
